```python
import math
import jax, jax.numpy as jnp
from jax import lax
import numpy as np

D_MODEL = 1024
BATCH = 8
SEQ = 4096
DEPTH = 1

N_META = 16
ATTN_HEADS = 4
HEAD_DIM = 64
V_DIM = 2 * HEAD_DIM
ATTN_WIDTH = ATTN_HEADS * V_DIM
CONV_WIDTH = 512
CONV_GROUPS = 8
CONV_K = 3
N_BRANCH = 2
N_BUCKETS = 32
MAX_DISTANCE = 128
Q_BLOCK = 128
EPS = 1e-6
NEG_INF = -1e30

SPLIT_SIZES = (
    ATTN_HEADS * 2 * HEAD_DIM,
    ATTN_HEADS * 2 * HEAD_DIM,
    ATTN_WIDTH,
    ATTN_WIDTH,
    CONV_WIDTH,
    CONV_WIDTH,
    CONV_WIDTH,
    CONV_WIDTH,
    N_BRANCH * D_MODEL,
)
IN_COLS = int(sum(SPLIT_SIZES))

kernel_name = "hybrid_diffattn_shortconv_gated_merge"


def rms_norm(x, g):
    xf = x.astype(jnp.float32)
    r = xf * lax.rsqrt(jnp.mean(xf * xf, axis=-1, keepdims=True) + EPS)
    return (r * g.astype(jnp.float32)).astype(x.dtype)


def rel_bucket(n):
    max_exact = N_BUCKETS // 2
    nf = jnp.maximum(n, max_exact).astype(jnp.float32)
    large = max_exact + (jnp.log(nf / max_exact) / math.log(MAX_DISTANCE / max_exact)
                         * (N_BUCKETS - max_exact)).astype(jnp.int32)
    large = jnp.minimum(large, N_BUCKETS - 1)
    return jnp.where(n < max_exact, n, large)


def diff_attention(q, k, v, rel_bias, lam):
    B, L, H, _, Dh = q.shape
    nblk = L // Q_BLOCK
    scale = Dh ** -0.5
    qt = jnp.transpose(q.astype(jnp.float32), (0, 2, 3, 1, 4))
    kt = jnp.transpose(k.astype(jnp.float32), (0, 2, 3, 1, 4))
    vt = jnp.transpose(v.astype(jnp.float32), (0, 2, 1, 3))
    qb = qt.reshape(B, H, 2, nblk, Q_BLOCK, Dh)
    qb = jnp.moveaxis(qb, 3, 0)
    offsets = jnp.arange(nblk, dtype=jnp.int32) * Q_BLOCK
    kpos = jnp.arange(L, dtype=jnp.int32)
    bias_tab = rel_bias.astype(jnp.float32)

    def block(args):
        q_blk, q0 = args
        s = jnp.einsum('bhcqd,bhckd->bhcqk', q_blk, kt) * scale
        qpos = q0 + jnp.arange(Q_BLOCK, dtype=jnp.int32)
        dist = qpos[:, None] - kpos[None, :]
        bias = bias_tab[rel_bucket(jnp.maximum(dist, 0))]
        s = s + jnp.transpose(bias, (2, 3, 0, 1))[None]
        s = jnp.where((dist >= 0)[None, None, None], s, NEG_INF)
        p = jax.nn.softmax(s, axis=-1)
        a = p[:, :, 0] - lam * p[:, :, 1]
        return jnp.einsum('bhqk,bhkv->bhqv', a, vt)

    out = lax.map(block, (qb, offsets))
    out = jnp.transpose(out, (1, 0, 3, 2, 4)).reshape(B, L, H, -1)
    return out.astype(v.dtype)


def short_conv(u, w):
    L = u.shape[1]
    up = jnp.pad(u, ((0, 0), (CONV_K - 1, 0), (0, 0)))
    y = w[0] * up[:, 0:L]
    for j in range(1, CONV_K):
        y = y + w[j] * up[:, j:j + L]
    return y


def setup_inputs(seed: int = 0) -> dict:
    key = jax.random.key(seed)
    ks = jax.random.split(key, 16)
    f32 = jnp.float32
    x = jax.random.normal(ks[0], (BATCH, SEQ, D_MODEL), f32)
    meta_tokens = jax.random.normal(ks[1], (N_META, D_MODEL), f32)
    rel_bias = 0.5 * jax.random.normal(ks[2], (N_BUCKETS, ATTN_HEADS, 2), f32)
    norm_g = 1.0 + 0.02 * jax.random.normal(ks[3], (DEPTH, D_MODEL), f32)
    w_in = jax.random.normal(ks[4], (DEPTH, D_MODEL, IN_COLS), f32) * D_MODEL ** -0.5
    q_norm_g = 1.0 + 0.02 * jax.random.normal(ks[5], (DEPTH, HEAD_DIM), f32)
    k_norm_g = 1.0 + 0.02 * jax.random.normal(ks[6], (DEPTH, HEAD_DIM), f32)
    lambda_q1 = 0.1 * jax.random.normal(ks[7], (DEPTH, HEAD_DIM), f32)
    lambda_k1 = 0.1 * jax.random.normal(ks[8], (DEPTH, HEAD_DIM), f32)
    lambda_q2 = 0.1 * jax.random.normal(ks[9], (DEPTH, HEAD_DIM), f32)
    lambda_k2 = 0.1 * jax.random.normal(ks[10], (DEPTH, HEAD_DIM), f32)
    subln_g = 1.0 + 0.02 * jax.random.normal(ks[11], (DEPTH, V_DIM), f32)
    conv_w = jax.random.normal(ks[12], (DEPTH, CONV_K, CONV_WIDTH), f32) * CONV_K ** -0.5
    w_branch = jax.random.normal(ks[13], (DEPTH, N_BRANCH, ATTN_WIDTH, D_MODEL), f32) * ATTN_WIDTH ** -0.5
    w_out = jax.random.normal(ks[14], (DEPTH, D_MODEL, D_MODEL), f32) * D_MODEL ** -0.5
    return {"x": x, "meta_tokens": meta_tokens, "rel_bias": rel_bias, "norm_g": norm_g,
            "w_in": w_in, "q_norm_g": q_norm_g, "k_norm_g": k_norm_g,
            "lambda_q1": lambda_q1, "lambda_k1": lambda_k1, "lambda_q2": lambda_q2,
            "lambda_k2": lambda_k2, "subln_g": subln_g, "conv_w": conv_w,
            "w_branch": w_branch, "w_out": w_out}


def reference(x, meta_tokens, rel_bias, norm_g, w_in, q_norm_g, k_norm_g,
              lambda_q1, lambda_k1, lambda_q2, lambda_k2, subln_g, conv_w,
              w_branch, w_out):
    B, S, D = x.shape
    L = N_META + S
    Lp = ((L + Q_BLOCK - 1) // Q_BLOCK) * Q_BLOCK
    meta = jnp.broadcast_to(meta_tokens.astype(x.dtype)[None], (B, N_META, D))
    h = jnp.concatenate([meta, x], axis=1)
    h = jnp.pad(h, ((0, 0), (0, Lp - L), (0, 0)))
    split_idx = [int(i) for i in np.cumsum(SPLIT_SIZES)[:-1]]

    for layer in range(DEPTH):
        lam_init = 0.8 - 0.6 * math.exp(-0.3 * layer)
        xn = rms_norm(h, norm_g[layer])
        proj = jnp.einsum('bld,dc->blc', xn, w_in[layer])
        q, k, v, g_attn, c_b, c_c, c_h, g_conv, g_merge = jnp.split(proj, split_idx, axis=-1)

        q = rms_norm(q.reshape(B, Lp, ATTN_HEADS, 2, HEAD_DIM), q_norm_g[layer])
        k = rms_norm(k.reshape(B, Lp, ATTN_HEADS, 2, HEAD_DIM), k_norm_g[layer])
        v = v.reshape(B, Lp, ATTN_HEADS, V_DIM)
        lam = (jnp.exp(jnp.sum(lambda_q1[layer].astype(jnp.float32) * lambda_k1[layer].astype(jnp.float32)))
               - jnp.exp(jnp.sum(lambda_q2[layer].astype(jnp.float32) * lambda_k2[layer].astype(jnp.float32)))
               + lam_init)
        a = diff_attention(q, k, v, rel_bias, lam)
        a = rms_norm(a, subln_g[layer]) * (1.0 - lam_init)
        a = a.reshape(B, Lp, ATTN_WIDTH) * jax.nn.silu(g_attn)

        c = c_b * short_conv(c_c * c_h, conv_w[layer].astype(h.dtype))
        c = c * jax.nn.silu(g_conv)

        br = jnp.stack([a, c], axis=2)
        y = jnp.einsum('blnc,ncd->blnd', br, w_branch[layer])
        gate = jax.nn.sigmoid(g_merge.reshape(B, Lp, N_BRANCH, D))
        merged = jnp.sum(gate * y, axis=2)
        h = h + jnp.einsum('bld,de->ble', merged, w_out[layer])

    return h[:, N_META:N_META + S]
```

```python
import functools
import math

import numpy as np
import jax
import jax.numpy as jnp
from jax import lax
from jax.experimental import pallas as pl
from jax.experimental.pallas import tpu as pltpu

N_META = 16
ATTN_HEADS = 4
HEAD_DIM = 64
V_DIM = 2 * HEAD_DIM
ATTN_WIDTH = ATTN_HEADS * V_DIM
CONV_WIDTH = 512
CONV_K = 3
N_BRANCH = 2
N_BUCKETS = 32
MAX_DISTANCE = 128
EPS = 1e-6
NEG_INF = -1e30
LAM_INIT = 0.8 - 0.6 * math.exp(-0.3 * 0)
LOG2E = math.log2(math.e)

COL_Q, COL_K, COL_V, COL_GA, COL_CB, COL_CC, COL_CH, COL_GC, COL_GM = (
    0, 512, 1024, 1536, 2048, 2560, 3072, 3584, 4096)
SEG = 512

LANES = 128
VMEM_LIMIT = 56 * 1024 * 1024

ATTN_TILE = 256
ROW_TILE = 512

F32 = jnp.float32
BF16 = jnp.bfloat16


def _bucket_thresholds():
    max_exact = N_BUCKETS // 2
    n = np.arange(0, 2 * MAX_DISTANCE)
    nf = np.maximum(n, max_exact).astype(np.float32)
    large = max_exact + (np.log(nf / np.float32(max_exact)) / np.float32(math.log(MAX_DISTANCE / max_exact))
                         * np.float32(N_BUCKETS - max_exact)).astype(np.int32)
    bucket = np.where(n < max_exact, n, np.minimum(large, N_BUCKETS - 1))
    return [int(np.argmax(bucket >= b)) for b in range(N_BUCKETS)]


_THR = _bucket_thresholds()


def _sigmoid(x):
    return 1.0 / (1.0 + jnp.exp(-x))


def _rms_rows(x, g):
    return x * lax.rsqrt(jnp.mean(x * x, axis=-1, keepdims=True) + EPS) * g


def _group_rms(x, gmean, g):
    ms = jnp.dot((x * x).astype(BF16), gmean, preferred_element_type=F32)
    return x * lax.rsqrt(ms + EPS) * g


def _prologue_kernel(rb_ref, meta_ref, ng_ref, wk_ref, wv_ref, wcc_ref, wch_ref, kg_ref, gmean_ref,
                     lamv_ref, kmeta_ref, vmeta_ref, umeta_ref, lam_ref, bdiag_ref, bnear_ref,
                     bmeta_ref, *, tile):
    xn = _rms_rows(meta_ref[...], ng_ref[...]).astype(BF16)
    k = jnp.dot(xn, wk_ref[...], preferred_element_type=F32)
    k = _group_rms(k, gmean_ref[...], kg_ref[...])
    v = jnp.dot(xn, wv_ref[...], preferred_element_type=F32)
    cc = jnp.dot(xn, wcc_ref[...], preferred_element_type=F32)
    ch = jnp.dot(xn, wch_ref[...], preferred_element_type=F32)
    kmeta_ref[...] = jnp.zeros(kmeta_ref.shape, BF16)
    vmeta_ref[...] = jnp.zeros(vmeta_ref.shape, BF16)
    kmeta_ref[0:N_META, :] = k.astype(BF16)
    vmeta_ref[0:N_META, :] = v.astype(BF16)
    umeta_ref[...] = cc * ch

    lv = lamv_ref[...]
    s1 = jnp.sum(lv[0:1] * lv[1:2], axis=-1, keepdims=True)
    s2 = jnp.sum(lv[2:3] * lv[3:4], axis=-1, keepdims=True)
    lam_ref[...] = jnp.broadcast_to(jnp.exp(s1) - jnp.exp(s2) + LAM_INIT, lam_ref.shape)

    row = lax.broadcasted_iota(jnp.int32, (LANES, LANES), 0)
    col = lax.broadcasted_iota(jnp.int32, (LANES, LANES), 1)
    d_diag = row - col
    d_sub = d_diag + LANES
    d_meta = d_diag + N_META
    nblk = tile // LANES
    neg = jnp.full((LANES, LANES), NEG_INF, F32)
    zero = jnp.zeros((LANES, LANES), F32)

    def toeplitz(dist, hc):
        far = rb_ref[N_BUCKETS - 1, hc]
        t = jnp.full((LANES, LANES), (rb_ref[0, hc] - far) * LOG2E, F32)
        for b in range(1, N_BUCKETS):
            t = jnp.where(dist >= _THR[b], (rb_ref[b, hc] - far) * LOG2E, t)
        return t

    for hc in range(2 * ATTN_HEADS):
        t0 = jnp.where(d_diag >= 0, toeplitz(d_diag, hc), NEG_INF)
        t1 = toeplitz(d_sub, hc)
        tm = jnp.where(col < N_META, toeplitz(d_meta, hc), NEG_INF)
        for i in range(nblk):
            for j in range(nblk):
                blk = t0 if i == j else (t1 if i == j + 1 else (zero if i > j else neg))
                bdiag_ref[hc, i * LANES:(i + 1) * LANES, j * LANES:(j + 1) * LANES] = blk
                near = t1 if (i == 0 and j == nblk - 1) else zero
                bnear_ref[1, hc, i * LANES:(i + 1) * LANES, j * LANES:(j + 1) * LANES] = near
                bnear_ref[0, hc, i * LANES:(i + 1) * LANES, j * LANES:(j + 1) * LANES] = neg
            mrest = jnp.where(col < N_META, 0.0, NEG_INF)
            bmeta_ref[0, hc, i * LANES:(i + 1) * LANES, :] = tm if i == 0 else mrest
            bmeta_ref[1, hc, i * LANES:(i + 1) * LANES, :] = mrest


def _prologue(rb, meta, ng, w_bf, kg, gmean, lamv, tile):
    d = meta.shape[1]
    wspec = lambda j: pl.BlockSpec((d, SEG), lambda i, j=j: (0, j))
    full = lambda shape: pl.BlockSpec(shape, lambda i: (0,) * len(shape))
    nhc = 2 * ATTN_HEADS
    out_shape = (
        jax.ShapeDtypeStruct((LANES, SEG), BF16),
        jax.ShapeDtypeStruct((LANES, SEG), BF16),
        jax.ShapeDtypeStruct((N_META, SEG), F32),
        jax.ShapeDtypeStruct((1, LANES), F32),
        jax.ShapeDtypeStruct((nhc, tile, tile), F32),
        jax.ShapeDtypeStruct((2, nhc, tile, tile), F32),
        jax.ShapeDtypeStruct((2, nhc, tile, LANES), F32),
    )
    return pl.pallas_call(
        functools.partial(_prologue_kernel, tile=tile),
        grid=(1,),
        in_specs=[
            pl.BlockSpec(memory_space=pltpu.SMEM),
            full(meta.shape), full(ng.shape),
            wspec(COL_K // SEG), wspec(COL_V // SEG), wspec(COL_CC // SEG), wspec(COL_CH // SEG),
            full(kg.shape), full(gmean.shape), full(lamv.shape),
        ],
        out_specs=tuple(full(s.shape) for s in out_shape),
        out_shape=out_shape,
        compiler_params=pltpu.CompilerParams(vmem_limit_bytes=VMEM_LIMIT),
        name="prologue",
    )(rb, meta, ng, w_bf, w_bf, w_bf, w_bf, kg, gmean, lamv)


def _in_proj_kernel(x_ref, ng_ref, w_ref, qg_ref, kg_ref, gmean_ref, cw_ref, umeta_ref,
                    q_ref, k_ref, v_ref, ga_ref, c_ref, gm_ref, ubuf, *, rows):
    t = pl.program_id(1)
    xn = _rms_rows(x_ref[0], ng_ref[...]).astype(BF16)

    def proj(lo, n=SEG):
        return jnp.dot(xn, w_ref[:, lo:lo + n], preferred_element_type=F32)

    gmean = gmean_ref[...]
    q_ref[0] = _group_rms(proj(COL_Q), gmean, qg_ref[...]).astype(BF16)
    k_ref[0] = _group_rms(proj(COL_K), gmean, kg_ref[...]).astype(BF16)
    v_ref[0] = proj(COL_V).astype(BF16)
    ga = proj(COL_GA)
    ga_ref[0] = (ga * _sigmoid(ga)).astype(BF16)

    @pl.when(t == 0)
    def _():
        ubuf[0:8, :] = umeta_ref[N_META - 8:N_META, :]

    @pl.when(t > 0)
    def _():
        ubuf[0:8, :] = ubuf[rows:rows + 8, :]

    u = proj(COL_CC) * proj(COL_CH)
    ubuf[8:8 + rows, :] = u
    cw = cw_ref[...]
    conv = cw[0:1] * ubuf[6:6 + rows, :] + cw[1:2] * ubuf[7:7 + rows, :] + cw[2:3] * u
    gc = proj(COL_GC)
    c_ref[0] = (proj(COL_CB) * conv * (gc * _sigmoid(gc))).astype(BF16)

    for j in range(gm_ref.shape[2] // SEG):
        gm_ref[0, :, j * SEG:(j + 1) * SEG] = _sigmoid(proj(COL_GM + j * SEG)).astype(BF16)


def _in_proj(x, ng, w_bf, qg, kg, gmean, cw, umeta, rows):
    b, s, d = x.shape
    ncols = w_bf.shape[1]
    gm_cols = ncols - COL_GM
    const = lambda shape: pl.BlockSpec(shape, lambda i, j: (0,) * len(shape))
    seg_out = pl.BlockSpec((1, rows, SEG), lambda i, j: (i, j, 0))
    out_shape = tuple(jax.ShapeDtypeStruct((b, s, SEG), BF16) for _ in range(5)) + (
        jax.ShapeDtypeStruct((b, s, gm_cols), BF16),)
    return pl.pallas_call(
        functools.partial(_in_proj_kernel, rows=rows),
        grid=(b, s // rows),
        in_specs=[
            pl.BlockSpec((1, rows, d), lambda i, j: (i, j, 0)),
            const(ng.shape),
            pl.BlockSpec(w_bf.shape, lambda i, j: (0, 0), pipeline_mode=pl.Buffered(1)),
            const(qg.shape), const(kg.shape), const(gmean.shape), const(cw.shape), const(umeta.shape),
        ],
        out_specs=(seg_out,) * 5 + (pl.BlockSpec((1, rows, gm_cols), lambda i, j: (i, j, 0)),),
        out_shape=out_shape,
        scratch_shapes=[pltpu.VMEM((rows + 8, SEG), F32)],
        compiler_params=pltpu.CompilerParams(
            dimension_semantics=("arbitrary", "arbitrary"), vmem_limit_bytes=VMEM_LIMIT),
        name="in_proj",
    )(x, ng, w_bf, qg, kg, gmean, cw, umeta)


def _attn_kernel(q_ref, k_ref, v_ref, kmeta_ref, vmeta_ref, ga_ref, bdiag_ref, bnear_ref, bmeta_ref,
                 lam_ref, sg_ref, o_ref, *, tile):
    qt = pl.program_id(2)
    q = q_ref[0]
    lane = lax.broadcasted_iota(jnp.int32, q.shape, 1)
    zero = jnp.zeros_like(q)
    qc = (jnp.where(lane < HEAD_DIM, q, zero), jnp.where(lane >= HEAD_DIM, q, zero))
    ones = jnp.ones((tile, LANES), BF16)

    def scores(c, kk):
        return lax.dot_general(qc[c], kk, (((1,), (1,)), ((), ())), preferred_element_type=F32)

    def pv(p, vv):
        vaug = jnp.concatenate([vv, ones[:vv.shape[0]]], axis=1)
        return jnp.dot(p, vaug, preferred_element_type=F32)

    kk, vv = kmeta_ref[...], vmeta_ref[...]
    m, acc = [], []
    for c in range(2):
        s = scores(c, kk) + bmeta_ref[0, c]
        mc = jnp.max(s, axis=-1, keepdims=True)
        p = jnp.exp2(s - mc).astype(BF16)
        m.append(mc)
        acc.append(pv(p, vv))

    def update(carry, kk, vv, bias):
        m, acc = carry
        m_out, acc_out = [], []
        for c in range(2):
            s = scores(c, kk)
            if bias is not None:
                s = s + bias[c]
            m_new = jnp.maximum(m[c], jnp.max(s, axis=-1, keepdims=True))
            alpha = jnp.exp2(m[c] - m_new)
            p = jnp.exp2(s - m_new).astype(BF16)
            m_out.append(m_new)
            acc_out.append(alpha * acc[c] + pv(p, vv))
        return tuple(m_out), tuple(acc_out)

    def tile_rows(kt):
        start = pl.multiple_of(kt * tile, tile)
        return k_ref[0, pl.ds(start, tile), :], v_ref[0, pl.ds(start, tile), :]

    def far_step(kt, carry):
        kk, vv = tile_rows(kt)
        return update(carry, kk, vv, None)

    carry = lax.fori_loop(0, jnp.maximum(qt - 1, 0), far_step, (tuple(m), tuple(acc)))
    kk, vv = tile_rows(jnp.maximum(qt - 1, 0))
    carry = update(carry, kk, vv, (bnear_ref[0, 0], bnear_ref[0, 1]))
    kk, vv = tile_rows(qt)
    m, acc = update(carry, kk, vv, (bdiag_ref[0], bdiag_ref[1]))

    o = [acc[c][:, :V_DIM] / acc[c][:, V_DIM:] for c in range(2)]
    a = o[0] - lam_ref[...] * o[1]
    a = _rms_rows(a, sg_ref[...])
    o_ref[0] = (a * ga_ref[0].astype(F32)).astype(BF16)


def _attention(q, k, v, kmeta, vmeta, ga, bdiag, bnear, bmeta, lam, sg, tile):
    b, s, _ = q.shape
    nq = s // tile
    tile_spec = pl.BlockSpec((1, tile, V_DIM), lambda i, h, t: (i, t, h))
    seq_spec = pl.BlockSpec((1, s, V_DIM), lambda i, h, t: (i, 0, h))
    meta_spec = pl.BlockSpec((LANES, V_DIM), lambda i, h, t: (0, h))
    row_spec = pl.BlockSpec((1, LANES), lambda i, h, t: (0, 0))
    return pl.pallas_call(
        functools.partial(_attn_kernel, tile=tile),
        grid=(b, ATTN_HEADS, nq),
        in_specs=[
            tile_spec, seq_spec, seq_spec, meta_spec, meta_spec, tile_spec,
            pl.BlockSpec((2, tile, tile), lambda i, h, t: (h, 0, 0)),
            pl.BlockSpec((1, 2, tile, tile), lambda i, h, t: (jnp.minimum(t, 1), h, 0, 0)),
            pl.BlockSpec((1, 2, tile, LANES), lambda i, h, t: (jnp.minimum(t, 1), h, 0, 0)),
            row_spec, row_spec,
        ],
        out_specs=tile_spec,
        out_shape=jax.ShapeDtypeStruct((b, s, ATTN_WIDTH), BF16),
        compiler_params=pltpu.CompilerParams(
            dimension_semantics=("arbitrary", "arbitrary", "arbitrary"), vmem_limit_bytes=VMEM_LIMIT),
        name="attention",
    )(q, k, v, kmeta, vmeta, ga, bdiag, bnear, bmeta, lam, sg)


def _merge_kernel(a_ref, c_ref, gm_ref, x_ref, wb_ref, wo_ref, o_ref):
    d = x_ref.shape[1]
    ya = jnp.dot(a_ref[...], wb_ref[0], preferred_element_type=F32)
    yc = jnp.dot(c_ref[...], wb_ref[1], preferred_element_type=F32)
    merged = gm_ref[:, 0:d].astype(F32) * ya + gm_ref[:, d:2 * d].astype(F32) * yc
    o_ref[...] = x_ref[...] + jnp.dot(merged.astype(BF16), wo_ref[...], preferred_element_type=F32)


def _merge(a, c, gm, x, wb, wo, rows):
    n, d = x.shape
    row_spec = lambda width: pl.BlockSpec((rows, width), lambda i: (i, 0))
    return pl.pallas_call(
        _merge_kernel,
        grid=(n // rows,),
        in_specs=[
            row_spec(a.shape[1]), row_spec(c.shape[1]), row_spec(gm.shape[1]), row_spec(d),
            pl.BlockSpec(wb.shape, lambda i: (0, 0, 0)),
            pl.BlockSpec(wo.shape, lambda i: (0, 0)),
        ],
        out_specs=row_spec(d),
        out_shape=jax.ShapeDtypeStruct((n, d), F32),
        compiler_params=pltpu.CompilerParams(
            dimension_semantics=("arbitrary",), vmem_limit_bytes=VMEM_LIMIT),
        name="merge",
    )(a, c, gm, x, wb, wo)


def kernel(x, meta_tokens, rel_bias, norm_g, w_in, q_norm_g, k_norm_g, lambda_q1, lambda_k1,
           lambda_q2, lambda_k2, subln_g, conv_w, w_branch, w_out):
    b, s, d = x.shape
    assert norm_g.shape[0] == 1, "single layer only"
    assert meta_tokens.shape[0] == N_META
    tile = min(ATTN_TILE, s)
    rows = min(ROW_TILE, s)
    assert s % tile == 0 and s % rows == 0 and tile % LANES == 0

    w_bf = w_in[0].astype(BF16)
    groups = SEG // HEAD_DIM
    qg = jnp.tile(q_norm_g[0].astype(F32) * (HEAD_DIM ** -0.5 * LOG2E), groups)[None]
    kg = jnp.tile(k_norm_g[0].astype(F32), groups)[None]
    gidx = np.arange(SEG) // HEAD_DIM
    gmean = jnp.asarray((gidx[:, None] == gidx[None, :]).astype(np.float32) / HEAD_DIM, BF16)
    lamv = jnp.stack([lambda_q1[0], lambda_k1[0], lambda_q2[0], lambda_k2[0]]).astype(F32)
    rb = rel_bias.astype(F32).reshape(N_BUCKETS, 2 * ATTN_HEADS)
    ng = norm_g.astype(F32)

    kmeta, vmeta, umeta, lam, bdiag, bnear, bmeta = _prologue(
        rb, meta_tokens.astype(F32), ng, w_bf, kg, gmean, lamv, tile)
    q, k, v, ga, c, gm = _in_proj(x, ng, w_bf, qg, kg, gmean, conv_w[0].astype(F32), umeta, rows)
    sg = subln_g.astype(F32) * (1.0 - LAM_INIT)
    a = _attention(q, k, v, kmeta, vmeta, ga, bdiag, bnear, bmeta, lam, sg, tile)
    out = _merge(a.reshape(b * s, -1), c.reshape(b * s, -1), gm.reshape(b * s, -1),
                 x.reshape(b * s, d), w_branch[0].astype(BF16), w_out[0].astype(BF16), rows)
    return out.reshape(b, s, d)
```

```python
import functools
import math

import numpy as np
import jax
import jax.numpy as jnp
from jax import lax
from jax.experimental import pallas as pl
from jax.experimental.pallas import tpu as pltpu

N_META = 16
ATTN_HEADS = 4
HEAD_DIM = 64
V_DIM = 2 * HEAD_DIM
ATTN_WIDTH = ATTN_HEADS * V_DIM
CONV_WIDTH = 512
CONV_K = 3
N_BRANCH = 2
N_BUCKETS = 32
MAX_DISTANCE = 128
EPS = 1e-6
NEG_INF = -1e30
LAM_INIT = 0.8 - 0.6 * math.exp(-0.3 * 0)
LOG2E = math.log2(math.e)

COL_Q, COL_K, COL_V, COL_GA, COL_CB, COL_CC, COL_CH, COL_GC, COL_GM = (
    0, 512, 1024, 1536, 2048, 2560, 3072, 3584, 4096)
SEG = 512

LANES = 128
VMEM_LIMIT = 56 * 1024 * 1024

ATTN_TILE = 512
ROW_TILE = 512

F32 = jnp.float32
BF16 = jnp.bfloat16


def _bucket_thresholds():
    max_exact = N_BUCKETS // 2
    n = np.arange(0, 2 * MAX_DISTANCE)
    nf = np.maximum(n, max_exact).astype(np.float32)
    large = max_exact + (np.log(nf / np.float32(max_exact)) / np.float32(math.log(MAX_DISTANCE / max_exact))
                         * np.float32(N_BUCKETS - max_exact)).astype(np.int32)
    bucket = np.where(n < max_exact, n, np.minimum(large, N_BUCKETS - 1))
    return [int(np.argmax(bucket >= b)) for b in range(N_BUCKETS)]


_THR = _bucket_thresholds()
assert _THR[N_BUCKETS - 1] <= LANES


def _sigmoid(x):
    return 1.0 / (1.0 + jnp.exp(-x))


def _rms_rows(x, g):
    return x * lax.rsqrt(jnp.mean(x * x, axis=-1, keepdims=True) + EPS) * g


def _group_rms(x, gmean, g):
    ms = jnp.dot((x * x).astype(BF16), gmean, preferred_element_type=F32)
    return x * lax.rsqrt(ms + EPS) * g


def _prologue_kernel(rb_ref, meta_ref, ng_ref, wk_ref, wv_ref, wcc_ref, wch_ref, kg_ref, gmean_ref,
                     lamv_ref, kmeta_ref, vmeta_ref, umeta_ref, lam_ref, t0_ref, t1_ref, tm_ref):
    xn = _rms_rows(meta_ref[...], ng_ref[...]).astype(BF16)
    k = jnp.dot(xn, wk_ref[...], preferred_element_type=F32)
    k = _group_rms(k, gmean_ref[...], kg_ref[...])
    v = jnp.dot(xn, wv_ref[...], preferred_element_type=F32)
    cc = jnp.dot(xn, wcc_ref[...], preferred_element_type=F32)
    ch = jnp.dot(xn, wch_ref[...], preferred_element_type=F32)
    kmeta_ref[...] = jnp.zeros(kmeta_ref.shape, BF16)
    vmeta_ref[...] = jnp.zeros(vmeta_ref.shape, BF16)
    kmeta_ref[0:N_META, :] = k.astype(BF16)
    vmeta_ref[0:N_META, :] = v.astype(BF16)
    umeta_ref[...] = cc * ch

    lv = lamv_ref[...]
    s1 = jnp.sum(lv[0:1] * lv[1:2], axis=-1, keepdims=True)
    s2 = jnp.sum(lv[2:3] * lv[3:4], axis=-1, keepdims=True)
    lam_ref[...] = jnp.broadcast_to(jnp.exp(s1) - jnp.exp(s2) + LAM_INIT, lam_ref.shape)

    row = lax.broadcasted_iota(jnp.int32, (LANES, LANES), 0)
    col = lax.broadcasted_iota(jnp.int32, (LANES, LANES), 1)
    d_diag = row - col
    d_sub = d_diag + LANES
    d_meta = d_diag + N_META

    def toeplitz(dist, hc):
        far = rb_ref[N_BUCKETS - 1, hc]
        t = jnp.full((LANES, LANES), (rb_ref[0, hc] - far) * LOG2E, F32)
        for b in range(1, N_BUCKETS):
            t = jnp.where(dist >= _THR[b], (rb_ref[b, hc] - far) * LOG2E, t)
        return t

    for hc in range(2 * ATTN_HEADS):
        t0_ref[hc] = jnp.where(d_diag >= 0, toeplitz(d_diag, hc), NEG_INF)
        t1_ref[hc] = toeplitz(d_sub, hc)
        tm_ref[hc] = jnp.where(col < N_META, toeplitz(d_meta, hc), NEG_INF)


def _prologue(rb, meta, ng, w_bf, kg, gmean, lamv):
    d = meta.shape[1]
    wspec = lambda j: pl.BlockSpec((d, SEG), lambda i, j=j: (0, j))
    full = lambda shape: pl.BlockSpec(shape, lambda i: (0,) * len(shape))
    blocks = jax.ShapeDtypeStruct((2 * ATTN_HEADS, LANES, LANES), F32)
    out_shape = (
        jax.ShapeDtypeStruct((LANES, SEG), BF16),
        jax.ShapeDtypeStruct((LANES, SEG), BF16),
        jax.ShapeDtypeStruct((N_META, SEG), F32),
        jax.ShapeDtypeStruct((1, LANES), F32),
        blocks,
        blocks,
        blocks,
    )
    return pl.pallas_call(
        _prologue_kernel,
        grid=(1,),
        in_specs=[
            pl.BlockSpec(memory_space=pltpu.SMEM),
            full(meta.shape), full(ng.shape),
            wspec(COL_K // SEG), wspec(COL_V // SEG), wspec(COL_CC // SEG), wspec(COL_CH // SEG),
            full(kg.shape), full(gmean.shape), full(lamv.shape),
        ],
        out_specs=tuple(full(s.shape) for s in out_shape),
        out_shape=out_shape,
        compiler_params=pltpu.CompilerParams(vmem_limit_bytes=VMEM_LIMIT),
        name="prologue",
    )(rb, meta, ng, w_bf, w_bf, w_bf, w_bf, kg, gmean, lamv)


def _in_proj_kernel(x_ref, ng_ref, w_ref, qg_ref, kg_ref, gmean_ref, cw_ref, umeta_ref,
                    q_ref, k_ref, v_ref, ga_ref, c_ref, gm_ref, ubuf, *, rows):
    t = pl.program_id(1)
    xn = _rms_rows(x_ref[0], ng_ref[...]).astype(BF16)

    def proj(lo, n=SEG):
        return jnp.dot(xn, w_ref[:, lo:lo + n], preferred_element_type=F32)

    gmean = gmean_ref[...]
    q_ref[0] = _group_rms(proj(COL_Q), gmean, qg_ref[...]).astype(BF16)
    k_ref[0] = _group_rms(proj(COL_K), gmean, kg_ref[...]).astype(BF16)
    v_ref[0] = proj(COL_V).astype(BF16)
    ga = proj(COL_GA)
    ga_ref[0] = (ga * _sigmoid(ga)).astype(BF16)

    @pl.when(t == 0)
    def _():
        ubuf[0:8, :] = umeta_ref[N_META - 8:N_META, :]

    @pl.when(t > 0)
    def _():
        ubuf[0:8, :] = ubuf[rows:rows + 8, :]

    u = proj(COL_CC) * proj(COL_CH)
    ubuf[8:8 + rows, :] = u
    cw = cw_ref[...]
    conv = cw[0:1] * ubuf[6:6 + rows, :] + cw[1:2] * ubuf[7:7 + rows, :] + cw[2:3] * u
    gc = proj(COL_GC)
    c_ref[0] = (proj(COL_CB) * conv * (gc * _sigmoid(gc))).astype(BF16)

    for j in range(gm_ref.shape[2] // SEG):
        gm_ref[0, :, j * SEG:(j + 1) * SEG] = _sigmoid(proj(COL_GM + j * SEG)).astype(BF16)


def _in_proj(x, ng, w_bf, qg, kg, gmean, cw, umeta, rows):
    b, s, d = x.shape
    ncols = w_bf.shape[1]
    gm_cols = ncols - COL_GM
    const = lambda shape: pl.BlockSpec(shape, lambda i, j: (0,) * len(shape))
    seg_out = pl.BlockSpec((1, rows, SEG), lambda i, j: (i, j, 0))
    out_shape = tuple(jax.ShapeDtypeStruct((b, s, SEG), BF16) for _ in range(5)) + (
        jax.ShapeDtypeStruct((b, s, gm_cols), BF16),)
    return pl.pallas_call(
        functools.partial(_in_proj_kernel, rows=rows),
        grid=(b, s // rows),
        in_specs=[
            pl.BlockSpec((1, rows, d), lambda i, j: (i, j, 0)),
            const(ng.shape),
            pl.BlockSpec(w_bf.shape, lambda i, j: (0, 0), pipeline_mode=pl.Buffered(1)),
            const(qg.shape), const(kg.shape), const(gmean.shape), const(cw.shape), const(umeta.shape),
        ],
        out_specs=(seg_out,) * 5 + (pl.BlockSpec((1, rows, gm_cols), lambda i, j: (i, j, 0)),),
        out_shape=out_shape,
        scratch_shapes=[pltpu.VMEM((rows + 8, SEG), F32)],
        compiler_params=pltpu.CompilerParams(
            dimension_semantics=("arbitrary", "arbitrary"), vmem_limit_bytes=VMEM_LIMIT),
        name="in_proj",
    )(x, ng, w_bf, qg, kg, gmean, cw, umeta)


def _map_blocks(x, fn):
    nr, nc = x.shape[0] // LANES, x.shape[1] // LANES
    return jnp.concatenate([
        jnp.concatenate([fn(i, j, x[i * LANES:(i + 1) * LANES, j * LANES:(j + 1) * LANES])
                         for j in range(nc)], axis=1) for i in range(nr)], axis=0)


def _attn_kernel(q_ref, k_ref, v_ref, kmeta_ref, vmeta_ref, ga_ref, t0_ref, t1_ref, tm_ref,
                 lam_ref, sg_ref, o_ref, s_buf, acc_buf, m_buf, *, tile):
    qt = pl.program_id(2)
    nblk = tile // LANES
    q = q_ref[0]
    lane = lax.broadcasted_iota(jnp.int32, q.shape, 1)
    zero = jnp.zeros_like(q)
    qc = (jnp.where(lane < HEAD_DIM, q, zero), jnp.where(lane >= HEAD_DIM, q, zero))

    def scores(c, kk):
        return lax.dot_general(qc[c], kk, (((1,), (1,)), ((), ())), preferred_element_type=F32)

    def tile_keys(kt):
        return k_ref[0, pl.ds(pl.multiple_of(kt * tile, tile), tile), :]

    def tile_values(kt):
        return v_ref[0, pl.ds(pl.multiple_of(kt * tile, tile), tile), :]

    def store_scores(slot, kt):
        kk = tile_keys(kt)
        for c in range(2):
            s_buf[slot, c] = scores(c, kk)

    def pv(p, vv):
        vaug = jnp.concatenate([vv, jnp.ones(vv.shape, BF16)], axis=1)
        return jnp.dot(p, vaug, preferred_element_type=F32)

    def rowmax(s):
        return jnp.max(s, axis=-1, keepdims=True)

    def lanes(x, n):
        return jnp.concatenate([x] * n, axis=1)

    store_scores(0, 0)

    km, vm = kmeta_ref[...], vmeta_ref[...]
    lane_m = lax.broadcasted_iota(jnp.int32, (1, LANES), 1)
    mrow = jnp.where(lane_m < N_META, 0.0, NEG_INF)
    for c in range(2):
        s_m = scores(c, km)
        top = s_m[:LANES] + jnp.where(qt == 0, tm_ref[c], mrow)
        s_m = jnp.concatenate([top, s_m[LANES:] + mrow], axis=0)
        mc = jnp.broadcast_to(rowmax(s_m), (tile, LANES))
        m_buf[c] = mc
        acc_buf[c] = pv(jnp.exp2(s_m - mc).astype(BF16), vm)

    def far_update(slot, kt):
        vv = tile_values(kt)
        for c in range(2):
            m_old = m_buf[c]
            m_new = jnp.maximum(m_old, rowmax(s_buf[slot, c]))
            m_buf[c] = m_new
            p = jnp.exp2(s_buf[slot, c] - lanes(m_new, nblk)).astype(BF16)
            acc_buf[c] = lanes(jnp.exp2(m_old - m_new), 2) * acc_buf[c] + pv(p, vv)

    n_far = jnp.maximum(qt - 1, 0)

    def far_pair(i, carry):
        store_scores(1, 2 * i + 1)
        far_update(0, 2 * i)
        store_scores(0, 2 * i + 2)
        far_update(1, 2 * i + 1)
        return carry

    lax.fori_loop(0, n_far // 2, far_pair, 0)

    @pl.when(n_far % 2 == 1)
    def _():
        store_scores(1, n_far)
        far_update(0, n_far - 1)

    near_slot = n_far % 2
    kk_own = tile_keys(qt)
    v_near, v_own = tile_values(n_far), tile_values(qt)
    near_off = jnp.where(qt == 0, NEG_INF, 0.0)
    o = []
    for c in range(2):
        t0, t1 = t0_ref[c], t1_ref[c]
        s_n = _map_blocks(s_buf[near_slot, c] + near_off,
                          lambda i, j, blk: blk + t1 if (i == 0 and j == nblk - 1) else blk)
        s_o = _map_blocks(scores(c, kk_own),
                          lambda i, j, blk: blk + t0 if i == j else (blk + t1 if i == j + 1 else blk))
        own_max = [rowmax(jnp.concatenate(
            [s_o[i * LANES:(i + 1) * LANES, j * LANES:(j + 1) * LANES] for j in range(i + 1)], axis=1))
            for i in range(nblk)]
        m_old = m_buf[c]
        m_new = jnp.maximum(jnp.maximum(m_old, rowmax(s_n)), jnp.concatenate(own_max, axis=0))
        m_full = lanes(m_new, nblk)
        p_n = jnp.exp2(s_n - m_full).astype(BF16)
        p_o = _map_blocks(jnp.exp2(s_o - m_full).astype(BF16),
                          lambda i, j, blk: blk if i >= j else jnp.zeros_like(blk))
        a = lanes(jnp.exp2(m_old - m_new), 2) * acc_buf[c] + pv(p_n, v_near) + pv(p_o, v_own)
        o.append(a[:, :V_DIM] / a[:, V_DIM:])

    a = o[0] - lam_ref[...] * o[1]
    a = _rms_rows(a, sg_ref[...])
    o_ref[0] = (a * ga_ref[0].astype(F32)).astype(BF16)


def _attention(q, k, v, kmeta, vmeta, ga, t0, t1, tm, lam, sg, tile):
    b, s, _ = q.shape
    nq = s // tile
    tile_spec = pl.BlockSpec((1, tile, V_DIM), lambda i, h, t: (i, t, h))
    seq_spec = pl.BlockSpec((1, s, V_DIM), lambda i, h, t: (i, 0, h))
    meta_spec = pl.BlockSpec((LANES, V_DIM), lambda i, h, t: (0, h))
    bias_spec = pl.BlockSpec((2, LANES, LANES), lambda i, h, t: (h, 0, 0))
    row_spec = pl.BlockSpec((1, LANES), lambda i, h, t: (0, 0))
    return pl.pallas_call(
        functools.partial(_attn_kernel, tile=tile),
        grid=(b, ATTN_HEADS, nq),
        in_specs=[
            tile_spec, seq_spec, seq_spec, meta_spec, meta_spec, tile_spec,
            bias_spec, bias_spec, bias_spec, row_spec, row_spec,
        ],
        out_specs=tile_spec,
        out_shape=jax.ShapeDtypeStruct((b, s, ATTN_WIDTH), BF16),
        scratch_shapes=[
            pltpu.VMEM((2, 2, tile, tile), F32),
            pltpu.VMEM((2, tile, 2 * V_DIM), F32),
            pltpu.VMEM((2, tile, LANES), F32),
        ],
        compiler_params=pltpu.CompilerParams(
            dimension_semantics=("arbitrary", "arbitrary", "arbitrary"), vmem_limit_bytes=VMEM_LIMIT),
        name="attention",
    )(q, k, v, kmeta, vmeta, ga, t0, t1, tm, lam, sg)


def _merge_kernel(a_ref, c_ref, gm_ref, x_ref, wb_ref, wo_ref, o_ref):
    d = x_ref.shape[1]
    ya = jnp.dot(a_ref[...], wb_ref[0], preferred_element_type=F32)
    yc = jnp.dot(c_ref[...], wb_ref[1], preferred_element_type=F32)
    merged = gm_ref[:, 0:d].astype(F32) * ya + gm_ref[:, d:2 * d].astype(F32) * yc
    o_ref[...] = x_ref[...] + jnp.dot(merged.astype(BF16), wo_ref[...], preferred_element_type=F32)


def _merge(a, c, gm, x, wb, wo, rows):
    n, d = x.shape
    row_spec = lambda width: pl.BlockSpec((rows, width), lambda i: (i, 0))
    return pl.pallas_call(
        _merge_kernel,
        grid=(n // rows,),
        in_specs=[
            row_spec(a.shape[1]), row_spec(c.shape[1]), row_spec(gm.shape[1]), row_spec(d),
            pl.BlockSpec(wb.shape, lambda i: (0, 0, 0)),
            pl.BlockSpec(wo.shape, lambda i: (0, 0)),
        ],
        out_specs=row_spec(d),
        out_shape=jax.ShapeDtypeStruct((n, d), F32),
        compiler_params=pltpu.CompilerParams(
            dimension_semantics=("arbitrary",), vmem_limit_bytes=VMEM_LIMIT),
        name="merge",
    )(a, c, gm, x, wb, wo)


def kernel(x, meta_tokens, rel_bias, norm_g, w_in, q_norm_g, k_norm_g, lambda_q1, lambda_k1,
           lambda_q2, lambda_k2, subln_g, conv_w, w_branch, w_out):
    b, s, d = x.shape
    assert norm_g.shape[0] == 1, "single layer only"
    assert meta_tokens.shape[0] == N_META
    tile = min(ATTN_TILE, s)
    rows = min(ROW_TILE, s)
    assert s % tile == 0 and s % rows == 0 and tile % LANES == 0

    w_bf = w_in[0].astype(BF16)
    groups = SEG // HEAD_DIM
    qg = jnp.tile(q_norm_g[0].astype(F32) * (HEAD_DIM ** -0.5 * LOG2E), groups)[None]
    kg = jnp.tile(k_norm_g[0].astype(F32), groups)[None]
    gidx = np.arange(SEG) // HEAD_DIM
    gmean = jnp.asarray((gidx[:, None] == gidx[None, :]).astype(np.float32) / HEAD_DIM, BF16)
    lamv = jnp.stack([lambda_q1[0], lambda_k1[0], lambda_q2[0], lambda_k2[0]]).astype(F32)
    rb = rel_bias.astype(F32).reshape(N_BUCKETS, 2 * ATTN_HEADS)
    ng = norm_g.astype(F32)

    kmeta, vmeta, umeta, lam, t0, t1, tm = _prologue(
        rb, meta_tokens.astype(F32), ng, w_bf, kg, gmean, lamv)
    q, k, v, ga, c, gm = _in_proj(x, ng, w_bf, qg, kg, gmean, conv_w[0].astype(F32), umeta, rows)
    sg = subln_g.astype(F32) * (1.0 - LAM_INIT)
    a = _attention(q, k, v, kmeta, vmeta, ga, t0, t1, tm, lam, sg, tile)
    out = _merge(a.reshape(b * s, -1), c.reshape(b * s, -1), gm.reshape(b * s, -1),
                 x.reshape(b * s, d), w_branch[0].astype(BF16), w_out[0].astype(BF16), rows)
    return out.reshape(b, s, d)
```

```python
import functools
import math

import numpy as np
import jax
import jax.numpy as jnp
from jax import lax
from jax.experimental import pallas as pl
from jax.experimental.pallas import tpu as pltpu

N_META = 16
ATTN_HEADS = 4
HEAD_DIM = 64
V_DIM = 2 * HEAD_DIM
ATTN_WIDTH = ATTN_HEADS * V_DIM
CONV_WIDTH = 512
CONV_K = 3
N_BRANCH = 2
N_BUCKETS = 32
MAX_DISTANCE = 128
EPS = 1e-6
NEG_INF = -1e30
LAM_INIT = 0.8 - 0.6 * math.exp(-0.3 * 0)
LOG2E = math.log2(math.e)

COL_Q, COL_K, COL_V, COL_GA, COL_CB, COL_CC, COL_CH, COL_GC, COL_GM = (
    0, 512, 1024, 1536, 2048, 2560, 3072, 3584, 4096)
SEG = 512

LANES = 128
MXU_DIM = 256
VMEM_LIMIT = 56 * 1024 * 1024

ATTN_TILE = 512
ROW_TILE = 512

F32 = jnp.float32
BF16 = jnp.bfloat16


def _bucket_thresholds():
    max_exact = N_BUCKETS // 2
    n = np.arange(0, 2 * MAX_DISTANCE)
    nf = np.maximum(n, max_exact).astype(np.float32)
    large = max_exact + (np.log(nf / np.float32(max_exact)) / np.float32(math.log(MAX_DISTANCE / max_exact))
                         * np.float32(N_BUCKETS - max_exact)).astype(np.int32)
    bucket = np.where(n < max_exact, n, np.minimum(large, N_BUCKETS - 1))
    return [int(np.argmax(bucket >= b)) for b in range(N_BUCKETS)]


_THR = _bucket_thresholds()
assert _THR[N_BUCKETS - 1] <= LANES


def _sigmoid(x):
    return 1.0 / (1.0 + jnp.exp(-x))


def _rms_rows(x, g):
    return x * lax.rsqrt(jnp.mean(x * x, axis=-1, keepdims=True) + EPS) * g


def _group_rms(x, gmean, g):
    ms = jnp.dot((x * x).astype(BF16), gmean, preferred_element_type=F32)
    return x * lax.rsqrt(ms + EPS) * g


def _prologue_kernel(rb_ref, meta_ref, ng_ref, wk_ref, wv_ref, wcc_ref, wch_ref, kg_ref, gmean_ref,
                     lamv_ref, kmeta_ref, vmeta_ref, umeta_ref, lam_ref, t0_ref, t1_ref, tm_ref):
    xn = _rms_rows(meta_ref[...], ng_ref[...]).astype(BF16)
    k = jnp.dot(xn, wk_ref[...], preferred_element_type=F32)
    k = _group_rms(k, gmean_ref[...], kg_ref[...])
    v = jnp.dot(xn, wv_ref[...], preferred_element_type=F32)
    cc = jnp.dot(xn, wcc_ref[...], preferred_element_type=F32)
    ch = jnp.dot(xn, wch_ref[...], preferred_element_type=F32)
    kmeta_ref[...] = jnp.zeros(kmeta_ref.shape, BF16)
    vmeta_ref[...] = jnp.zeros(vmeta_ref.shape, BF16)
    kmeta_ref[0:N_META, :] = k.astype(BF16)
    vmeta_ref[0:N_META, :] = v.astype(BF16)
    umeta_ref[...] = cc * ch

    lv = lamv_ref[...]
    s1 = jnp.sum(lv[0:1] * lv[1:2], axis=-1, keepdims=True)
    s2 = jnp.sum(lv[2:3] * lv[3:4], axis=-1, keepdims=True)
    lam_ref[...] = jnp.broadcast_to(jnp.exp(s1) - jnp.exp(s2) + LAM_INIT, lam_ref.shape)

    row = lax.broadcasted_iota(jnp.int32, (LANES, LANES), 0)
    col = lax.broadcasted_iota(jnp.int32, (LANES, LANES), 1)
    d_diag = row - col
    d_sub = d_diag + LANES
    d_meta = d_diag + N_META

    def toeplitz(dist, hc):
        far = rb_ref[N_BUCKETS - 1, hc]
        t = jnp.full((LANES, LANES), (rb_ref[0, hc] - far) * LOG2E, F32)
        for b in range(1, N_BUCKETS):
            t = jnp.where(dist >= _THR[b], (rb_ref[b, hc] - far) * LOG2E, t)
        return t

    for hc in range(2 * ATTN_HEADS):
        t0_ref[hc] = jnp.where(d_diag >= 0, toeplitz(d_diag, hc), NEG_INF)
        t1_ref[hc] = toeplitz(d_sub, hc)
        tm_ref[hc] = jnp.where(col < N_META, toeplitz(d_meta, hc), NEG_INF)


def _prologue(rb, meta, ng, w_bf, kg, gmean, lamv):
    d = meta.shape[1]
    wspec = lambda j: pl.BlockSpec((d, SEG), lambda i, j=j: (0, j))
    full = lambda shape: pl.BlockSpec(shape, lambda i: (0,) * len(shape))
    blocks = jax.ShapeDtypeStruct((2 * ATTN_HEADS, LANES, LANES), F32)
    out_shape = (
        jax.ShapeDtypeStruct((LANES, SEG), BF16),
        jax.ShapeDtypeStruct((LANES, SEG), BF16),
        jax.ShapeDtypeStruct((N_META, SEG), F32),
        jax.ShapeDtypeStruct((1, LANES), F32),
        blocks,
        blocks,
        blocks,
    )
    return pl.pallas_call(
        _prologue_kernel,
        grid=(1,),
        in_specs=[
            pl.BlockSpec(memory_space=pltpu.SMEM),
            full(meta.shape), full(ng.shape),
            wspec(COL_K // SEG), wspec(COL_V // SEG), wspec(COL_CC // SEG), wspec(COL_CH // SEG),
            full(kg.shape), full(gmean.shape), full(lamv.shape),
        ],
        out_specs=tuple(full(s.shape) for s in out_shape),
        out_shape=out_shape,
        compiler_params=pltpu.CompilerParams(vmem_limit_bytes=VMEM_LIMIT),
        name="prologue",
    )(rb, meta, ng, w_bf, w_bf, w_bf, w_bf, kg, gmean, lamv)


def _in_proj_kernel(x_ref, ng_ref, w_ref, qg_ref, kg_ref, gmean_ref, cw_ref, umeta_ref,
                    q_ref, k_ref, v_ref, ga_ref, c_ref, gm_ref, ubuf, *, rows):
    t = pl.program_id(1)
    xn = _rms_rows(x_ref[0], ng_ref[...]).astype(BF16)

    def proj(lo, n=SEG):
        return jnp.dot(xn, w_ref[:, lo:lo + n], preferred_element_type=F32)

    gmean = gmean_ref[...]
    q_ref[0] = _group_rms(proj(COL_Q), gmean, qg_ref[...]).astype(BF16)
    k_ref[0] = _group_rms(proj(COL_K), gmean, kg_ref[...]).astype(BF16)
    v_ref[0] = proj(COL_V).astype(BF16)
    ga = proj(COL_GA)
    ga_ref[0] = (ga * _sigmoid(ga)).astype(BF16)

    @pl.when(t == 0)
    def _():
        ubuf[0:8, :] = umeta_ref[N_META - 8:N_META, :]

    @pl.when(t > 0)
    def _():
        ubuf[0:8, :] = ubuf[rows:rows + 8, :]

    u = proj(COL_CC) * proj(COL_CH)
    ubuf[8:8 + rows, :] = u
    cw = cw_ref[...]
    conv = cw[0:1] * ubuf[6:6 + rows, :] + cw[1:2] * ubuf[7:7 + rows, :] + cw[2:3] * u
    gc = proj(COL_GC)
    c_ref[0] = (proj(COL_CB) * conv * (gc * _sigmoid(gc))).astype(BF16)

    for j in range(gm_ref.shape[2] // SEG):
        gm_ref[0, :, j * SEG:(j + 1) * SEG] = _sigmoid(proj(COL_GM + j * SEG)).astype(BF16)


def _in_proj(x, ng, w_bf, qg, kg, gmean, cw, umeta, rows):
    b, s, d = x.shape
    ncols = w_bf.shape[1]
    gm_cols = ncols - COL_GM
    const = lambda shape: pl.BlockSpec(shape, lambda i, j: (0,) * len(shape))
    seg_out = pl.BlockSpec((1, rows, SEG), lambda i, j: (i, j, 0))
    out_shape = tuple(jax.ShapeDtypeStruct((b, s, SEG), BF16) for _ in range(5)) + (
        jax.ShapeDtypeStruct((b, s, gm_cols), BF16),)
    return pl.pallas_call(
        functools.partial(_in_proj_kernel, rows=rows),
        grid=(b, s // rows),
        in_specs=[
            pl.BlockSpec((1, rows, d), lambda i, j: (i, j, 0)),
            const(ng.shape),
            pl.BlockSpec(w_bf.shape, lambda i, j: (0, 0), pipeline_mode=pl.Buffered(1)),
            const(qg.shape), const(kg.shape), const(gmean.shape), const(cw.shape), const(umeta.shape),
        ],
        out_specs=(seg_out,) * 5 + (pl.BlockSpec((1, rows, gm_cols), lambda i, j: (i, j, 0)),),
        out_shape=out_shape,
        scratch_shapes=[pltpu.VMEM((rows + 8, SEG), F32)],
        compiler_params=pltpu.CompilerParams(
            dimension_semantics=("arbitrary", "arbitrary"), vmem_limit_bytes=VMEM_LIMIT),
        name="in_proj",
    )(x, ng, w_bf, qg, kg, gmean, cw, umeta)


def _attn_kernel(qa_ref, qb_ref, k_ref, v_ref, kmeta_ref, vmeta_ref, gaa_ref, gab_ref,
                 t0_ref, t1_ref, tm_ref, lam_ref, sg_ref, o_ref,
                 kbuf, vbuf, qm_buf, s_buf, own_buf, acc_buf, m_buf, *, tile, nq):
    pr = pl.program_id(2)
    nblk = tile // LANES
    n_far = nq - 1
    qts = (pr, nq - 1 - pr)

    @pl.when(pr == 0)
    def _():
        kbuf[0:LANES, :] = kmeta_ref[...]
        kbuf[LANES:, :] = k_ref[0]
        vbuf[0:LANES, 0:V_DIM] = vmeta_ref[...]
        vbuf[LANES:, 0:V_DIM] = v_ref[0]
        vbuf[:, V_DIM:] = jnp.ones((vbuf.shape[0], V_DIM), BF16)

    lane = lax.broadcasted_iota(jnp.int32, (tile, V_DIM), 1)
    for z, q_ref in enumerate((qa_ref, qb_ref)):
        q = q_ref[0]
        zero = jnp.zeros_like(q)
        qm_buf[z, 0] = jnp.where(lane < HEAD_DIM, q, zero)
        qm_buf[z, 1] = jnp.where(lane >= HEAD_DIM, q, zero)
    lane_m = lax.broadcasted_iota(jnp.int32, (1, LANES), 1)
    mrow = jnp.where(lane_m < N_META, 0.0, NEG_INF)

    def scores(qm, kk):
        return lax.dot_general(qm, kk, (((1,), (1,)), ((), ())), preferred_element_type=F32)

    def rowmax(s):
        return jnp.max(s, axis=-1, keepdims=True)

    def lanes(x, n):
        return jnp.concatenate([x] * n, axis=1)


    def far_item(i):
        z = jnp.where(i >= pr, 1, 0)
        return z, i - z * pr

    def far_rows(kt):
        return pl.ds(pl.multiple_of(kt * tile, tile), tile)

    def store_scores(slot, i):
        z, kt = far_item(i)
        kk = kbuf[far_rows(kt), :]
        meta_mask = jnp.where(kt == 0, mrow, 0.0)
        for c in range(2):
            s = scores(qm_buf[z, c], kk)
            s_buf[slot, c] = jnp.concatenate([s[:, :LANES] + meta_mask, s[:, LANES:]], axis=1)

    def far_update(slot, i, last=False):
        z, kt = (1, i - pr) if last else far_item(i)
        vv = vbuf[far_rows(kt), :]
        accs = []
        for c in range(2):
            m_old = m_buf[z, c]
            m_new = jnp.maximum(m_old, rowmax(s_buf[slot, c]))
            p = jnp.exp2(s_buf[slot, c] - lanes(m_new, nblk)).astype(BF16)
            acc = (lanes(jnp.exp2(m_old - m_new), 2) * acc_buf[z, c]
                   + jnp.dot(p, vv, preferred_element_type=F32))
            if last:
                accs.append(acc)
            else:
                m_buf[z, c] = m_new
                acc_buf[z, c] = acc
        return accs


    chunk_blocks = MXU_DIM // LANES
    chunks = [(lo, min(lo + chunk_blocks, nblk + 1), max(lo - 1, 0))
              for lo in range(0, nblk + 1, chunk_blocks)]

    def own_rows(z, lo, hi):
        start = qts[z] * tile + lo * LANES
        return pl.ds(pl.multiple_of(start, LANES), (hi - lo) * LANES)

    def store_own(z):
        for lo, hi, i_min in chunks:
            kk = kbuf[own_rows(z, lo, hi), :]
            for c in range(2):
                own_buf[z, c, i_min * LANES:, lo * LANES:hi * LANES] = scores(
                    qm_buf[z, c, i_min * LANES:, :], kk)

    def init_own(z, hc0):
        first = (qts[z] == 0) if z == 0 else None
        for c in range(2):
            t0, t1, tm = t0_ref[hc0 + c], t1_ref[hc0 + c], tm_ref[hc0 + c]

            def biased(i, j):
                blk = own_buf[z, c, i * LANES:(i + 1) * LANES, j * LANES:(j + 1) * LANES]
                if j == i + 1:
                    return blk + t0
                if j == 0 and first is not None:
                    return blk + (jnp.where(first, tm, t1) if i == 0 else jnp.where(first, mrow, 0.0))
                return blk + t1 if j == i else blk

            pblk, m_rows = {}, []
            for i in range(nblk):
                row = [biased(i, j) for j in range(i + 2)]
                m_i = jnp.broadcast_to(rowmax(jnp.concatenate(row, axis=1)), (LANES, LANES))
                m_rows.append(m_i)
                for j, blk in enumerate(row):
                    pblk[i, j] = jnp.exp2(blk - m_i).astype(BF16)
            m_buf[z, c] = jnp.concatenate(m_rows, axis=0)

            zeros = jnp.zeros((LANES, LANES), BF16)
            outs = [None] * nblk
            for lo, hi, i_min in chunks:
                p = jnp.concatenate([
                    jnp.concatenate([pblk.get((i, j), zeros) for j in range(lo, hi)], axis=1)
                    for i in range(i_min, nblk)], axis=0)
                o = jnp.dot(p, vbuf[own_rows(z, lo, hi), :], preferred_element_type=F32)
                for i in range(i_min, nblk):
                    piece = o[(i - i_min) * LANES:(i - i_min + 1) * LANES]
                    outs[i] = piece if outs[i] is None else outs[i] + piece
            acc_buf[z, c] = jnp.concatenate(outs, axis=0)

    def finalize(z, accs, ga_ref):
        o = [acc[:, :V_DIM] / acc[:, V_DIM:] for acc in accs]
        a = _rms_rows(o[0] - lam_ref[...] * o[1], sg_ref[...])
        o_ref[0, z] = (a * ga_ref[0].astype(F32)).astype(BF16)

    store_own(0)
    store_own(1)
    store_scores(0, 0)
    init_own(0, 0)
    init_own(1, 0)

    def far_pair(j, carry):
        i = 2 * j
        store_scores(1, i + 1)
        far_update(0, i)
        store_scores(0, i + 2)
        far_update(1, i + 1)
        return carry

    lax.fori_loop(0, (n_far - 1) // 2, far_pair, 0)
    last = far_update(0, n_far - 1, last=True)
    finalize(0, [acc_buf[0, 0], acc_buf[0, 1]], gaa_ref)
    finalize(1, last, gab_ref)


def _attention(q, k, v, kmeta, vmeta, ga, t0, t1, tm, lam, sg, tile):
    b, s, _ = q.shape
    nq = s // tile
    assert nq % 2 == 0
    tile_a = pl.BlockSpec((1, tile, V_DIM), lambda i, h, p: (i, p, h))
    tile_b = pl.BlockSpec((1, tile, V_DIM), lambda i, h, p: (i, nq - 1 - p, h))
    seq_spec = pl.BlockSpec((1, s, V_DIM), lambda i, h, p: (i, 0, h))
    meta_spec = pl.BlockSpec((LANES, V_DIM), lambda i, h, p: (0, h))
    bias_spec = pl.BlockSpec((2, LANES, LANES), lambda i, h, p: (h, 0, 0))
    row_spec = pl.BlockSpec((1, LANES), lambda i, h, p: (0, 0))
    out = jax.ShapeDtypeStruct((b, 2, s // 2, ATTN_WIDTH), BF16)
    return pl.pallas_call(
        functools.partial(_attn_kernel, tile=tile, nq=nq),
        grid=(b, ATTN_HEADS, nq // 2),
        in_specs=[
            tile_a, tile_b, seq_spec, seq_spec, meta_spec, meta_spec, tile_a, tile_b,
            bias_spec, bias_spec, bias_spec, row_spec, row_spec,
        ],
        out_specs=pl.BlockSpec((1, 2, tile, V_DIM), lambda i, h, p: (i, 0, p, h)),
        out_shape=out,
        scratch_shapes=[
            pltpu.VMEM((LANES + s, V_DIM), BF16),
            pltpu.VMEM((LANES + s, 2 * V_DIM), BF16),
            pltpu.VMEM((2, 2, tile, V_DIM), BF16),
            pltpu.VMEM((2, 2, tile, tile), F32),
            pltpu.VMEM((2, 2, tile, tile + LANES), F32),
            pltpu.VMEM((2, 2, tile, 2 * V_DIM), F32),
            pltpu.VMEM((2, 2, tile, LANES), F32),
        ],
        compiler_params=pltpu.CompilerParams(
            dimension_semantics=("arbitrary", "arbitrary", "arbitrary"), vmem_limit_bytes=VMEM_LIMIT),
        name="attention",
    )(q, q, k, v, kmeta, vmeta, ga, ga, t0, t1, tm, lam, sg)


def _merge_kernel(a_ref, c_ref, gm_ref, x_ref, wb_ref, wo_ref, o_ref):
    d = x_ref.shape[1]
    ya = jnp.dot(a_ref[0, 0], wb_ref[0], preferred_element_type=F32)
    yc = jnp.dot(c_ref[...], wb_ref[1], preferred_element_type=F32)
    merged = gm_ref[:, 0:d].astype(F32) * ya + gm_ref[:, d:2 * d].astype(F32) * yc
    o_ref[...] = x_ref[...] + jnp.dot(merged.astype(BF16), wo_ref[...], preferred_element_type=F32)


def _merge(a, c, gm, x, wb, wo, rows):
    n, d = x.shape
    row_spec = lambda width: pl.BlockSpec((rows, width), lambda i: (i, 0))
    half = a.shape[2] // rows
    nq = 2 * half

    def a_index(i):
        j = i % nq
        return i // nq, j // half, jnp.where(j < half, j, nq - 1 - j), 0

    return pl.pallas_call(
        _merge_kernel,
        grid=(n // rows,),
        in_specs=[
            pl.BlockSpec((1, 1, rows, a.shape[3]), a_index),
            row_spec(c.shape[1]), row_spec(gm.shape[1]), row_spec(d),
            pl.BlockSpec(wb.shape, lambda i: (0, 0, 0)),
            pl.BlockSpec(wo.shape, lambda i: (0, 0)),
        ],
        out_specs=row_spec(d),
        out_shape=jax.ShapeDtypeStruct((n, d), F32),
        compiler_params=pltpu.CompilerParams(
            dimension_semantics=("arbitrary",), vmem_limit_bytes=VMEM_LIMIT),
        name="merge",
    )(a, c, gm, x, wb, wo)


def kernel(x, meta_tokens, rel_bias, norm_g, w_in, q_norm_g, k_norm_g, lambda_q1, lambda_k1,
           lambda_q2, lambda_k2, subln_g, conv_w, w_branch, w_out):
    b, s, d = x.shape
    assert norm_g.shape[0] == 1, "single layer only"
    assert meta_tokens.shape[0] == N_META
    tile = min(ATTN_TILE, s // 2)
    rows = min(ROW_TILE, tile)
    assert s % (2 * tile) == 0 and rows == tile and tile % MXU_DIM == 0

    w_bf = w_in[0].astype(BF16)
    groups = SEG // HEAD_DIM
    qg = jnp.tile(q_norm_g[0].astype(F32) * (HEAD_DIM ** -0.5 * LOG2E), groups)[None]
    kg = jnp.tile(k_norm_g[0].astype(F32), groups)[None]
    gidx = np.arange(SEG) // HEAD_DIM
    gmean = jnp.asarray((gidx[:, None] == gidx[None, :]).astype(np.float32) / HEAD_DIM, BF16)
    lamv = jnp.stack([lambda_q1[0], lambda_k1[0], lambda_q2[0], lambda_k2[0]]).astype(F32)
    rb = rel_bias.astype(F32).reshape(N_BUCKETS, 2 * ATTN_HEADS)
    ng = norm_g.astype(F32)

    kmeta, vmeta, umeta, lam, t0, t1, tm = _prologue(
        rb, meta_tokens.astype(F32), ng, w_bf, kg, gmean, lamv)
    q, k, v, ga, c, gm = _in_proj(x, ng, w_bf, qg, kg, gmean, conv_w[0].astype(F32), umeta, rows)
    sg = subln_g.astype(F32) * (1.0 - LAM_INIT)
    a = _attention(q, k, v, kmeta, vmeta, ga, t0, t1, tm, lam, sg, tile)
    out = _merge(a, c.reshape(b * s, -1), gm.reshape(b * s, -1),
                 x.reshape(b * s, d), w_branch[0].astype(BF16), w_out[0].astype(BF16), rows)
    return out.reshape(b, s, d)
```

```python
import functools
import math

import numpy as np
import jax
import jax.numpy as jnp
from jax import lax
from jax.experimental import pallas as pl
from jax.experimental.pallas import tpu as pltpu

N_META = 16
ATTN_HEADS = 4
HEAD_DIM = 64
V_DIM = 2 * HEAD_DIM
ATTN_WIDTH = ATTN_HEADS * V_DIM
CONV_WIDTH = 512
CONV_K = 3
N_BRANCH = 2
N_BUCKETS = 32
MAX_DISTANCE = 128
EPS = 1e-6
NEG_INF = -1e30
LAM_INIT = 0.8 - 0.6 * math.exp(-0.3 * 0)
LOG2E = math.log2(math.e)

COL_Q, COL_K, COL_V, COL_GA, COL_CB, COL_CC, COL_CH, COL_GC, COL_GM = (
    0, 512, 1024, 1536, 2048, 2560, 3072, 3584, 4096)
SEG = 512

LANES = 128
MXU_DIM = 256
VMEM_LIMIT = 56 * 1024 * 1024

ATTN_TILE = 512
ROW_TILE = 512

F32 = jnp.float32
BF16 = jnp.bfloat16


def _bucket_thresholds():
    max_exact = N_BUCKETS // 2
    n = np.arange(0, 2 * MAX_DISTANCE)
    nf = np.maximum(n, max_exact).astype(np.float32)
    large = max_exact + (np.log(nf / np.float32(max_exact)) / np.float32(math.log(MAX_DISTANCE / max_exact))
                         * np.float32(N_BUCKETS - max_exact)).astype(np.int32)
    bucket = np.where(n < max_exact, n, np.minimum(large, N_BUCKETS - 1))
    return [int(np.argmax(bucket >= b)) for b in range(N_BUCKETS)]


_THR = _bucket_thresholds()
assert _THR[N_BUCKETS - 1] <= LANES


def _sigmoid(x):
    return 1.0 / (1.0 + jnp.exp(-x))


def _rms_rows(x, g):
    return x * lax.rsqrt(jnp.mean(x * x, axis=-1, keepdims=True) + EPS) * g


def _group_rms(x, gmean, g):
    sq = (x * x).astype(BF16)
    ms = jnp.concatenate([
        jnp.dot(sq[:, lo:lo + MXU_DIM], gmean, preferred_element_type=F32)
        for lo in range(0, x.shape[1], MXU_DIM)], axis=1)
    return x * lax.rsqrt(ms + EPS) * g


def _prologue_kernel(rb_ref, meta_ref, ng_ref, wk_ref, wv_ref, wcc_ref, wch_ref, kg_ref, gmean_ref,
                     lamv_ref, kmeta_ref, vmeta_ref, umeta_ref, lam_ref, t0_ref, t1_ref, tm_ref):
    xn = _rms_rows(meta_ref[...], ng_ref[...]).astype(BF16)
    k = jnp.dot(xn, wk_ref[...], preferred_element_type=F32)
    k = _group_rms(k, gmean_ref[...], kg_ref[...])
    v = jnp.dot(xn, wv_ref[...], preferred_element_type=F32)
    cc = jnp.dot(xn, wcc_ref[...], preferred_element_type=F32)
    ch = jnp.dot(xn, wch_ref[...], preferred_element_type=F32)
    kmeta_ref[...] = jnp.zeros(kmeta_ref.shape, BF16)
    vmeta_ref[...] = jnp.zeros(vmeta_ref.shape, BF16)
    kmeta_ref[0:N_META, :] = k.astype(BF16)
    vmeta_ref[0:N_META, :] = v.astype(BF16)
    umeta_ref[...] = cc * ch

    lv = lamv_ref[...]
    s1 = jnp.sum(lv[0:1] * lv[1:2], axis=-1, keepdims=True)
    s2 = jnp.sum(lv[2:3] * lv[3:4], axis=-1, keepdims=True)
    lam_ref[...] = jnp.broadcast_to(jnp.exp(s1) - jnp.exp(s2) + LAM_INIT, lam_ref.shape)

    row = lax.broadcasted_iota(jnp.int32, (LANES, LANES), 0)
    col = lax.broadcasted_iota(jnp.int32, (LANES, LANES), 1)
    d_diag = row - col
    d_sub = d_diag + LANES
    d_meta = d_diag + N_META

    def toeplitz(dist, hc):
        far = rb_ref[N_BUCKETS - 1, hc]
        t = jnp.full((LANES, LANES), (rb_ref[0, hc] - far) * LOG2E, F32)
        for b in range(1, N_BUCKETS):
            t = jnp.where(dist >= _THR[b], (rb_ref[b, hc] - far) * LOG2E, t)
        return t

    for hc in range(2 * ATTN_HEADS):
        t0_ref[hc] = jnp.where(d_diag >= 0, toeplitz(d_diag, hc), NEG_INF)
        t1_ref[hc] = toeplitz(d_sub, hc)
        tm_ref[hc] = jnp.where(col < N_META, toeplitz(d_meta, hc), NEG_INF)


def _prologue(rb, meta, ng, w_bf, kg, gmean, lamv):
    d = meta.shape[1]
    wspec = lambda j: pl.BlockSpec((d, SEG), lambda i, j=j: (0, j))
    full = lambda shape: pl.BlockSpec(shape, lambda i: (0,) * len(shape))
    blocks = jax.ShapeDtypeStruct((2 * ATTN_HEADS, LANES, LANES), F32)
    out_shape = (
        jax.ShapeDtypeStruct((LANES, SEG), BF16),
        jax.ShapeDtypeStruct((LANES, SEG), BF16),
        jax.ShapeDtypeStruct((N_META, SEG), F32),
        jax.ShapeDtypeStruct((1, LANES), F32),
        blocks,
        blocks,
        blocks,
    )
    return pl.pallas_call(
        _prologue_kernel,
        grid=(1,),
        in_specs=[
            pl.BlockSpec(memory_space=pltpu.SMEM),
            full(meta.shape), full(ng.shape),
            wspec(COL_K // SEG), wspec(COL_V // SEG), wspec(COL_CC // SEG), wspec(COL_CH // SEG),
            full(kg.shape), full(gmean.shape), full(lamv.shape),
        ],
        out_specs=tuple(full(s.shape) for s in out_shape),
        out_shape=out_shape,
        compiler_params=pltpu.CompilerParams(vmem_limit_bytes=VMEM_LIMIT),
        name="prologue",
    )(rb, meta, ng, w_bf, w_bf, w_bf, w_bf, kg, gmean, lamv)


def _in_proj_kernel(x_ref, ng_ref, w_ref, qg_ref, kg_ref, gmean_ref, cw_ref, umeta_ref,
                    q_ref, k_ref, v_ref, ga_ref, c_ref, gm_ref, ubuf, *, rows):
    t = pl.program_id(1)
    xn = _rms_rows(x_ref[0], ng_ref[...]).astype(BF16)

    def proj(lo, n=SEG):
        return jnp.dot(xn, w_ref[:, lo:lo + n], preferred_element_type=F32)

    gmean = gmean_ref[...]
    q_ref[0] = _group_rms(proj(COL_Q), gmean, qg_ref[...]).astype(BF16)
    k_ref[0] = _group_rms(proj(COL_K), gmean, kg_ref[...]).astype(BF16)
    v_ref[0] = proj(COL_V).astype(BF16)
    ga = proj(COL_GA)
    ga_ref[0] = (ga * _sigmoid(ga)).astype(BF16)

    @pl.when(t == 0)
    def _():
        ubuf[0:8, :] = umeta_ref[N_META - 8:N_META, :]

    @pl.when(t > 0)
    def _():
        ubuf[0:8, :] = ubuf[rows:rows + 8, :]

    u = proj(COL_CC) * proj(COL_CH)
    ubuf[8:8 + rows, :] = u
    cw = cw_ref[...]
    conv = cw[0:1] * ubuf[6:6 + rows, :] + cw[1:2] * ubuf[7:7 + rows, :] + cw[2:3] * u
    gc = proj(COL_GC)
    c_ref[0] = (proj(COL_CB) * conv * (gc * _sigmoid(gc))).astype(BF16)

    for j in range(gm_ref.shape[2] // SEG):
        gm_ref[0, :, j * SEG:(j + 1) * SEG] = _sigmoid(proj(COL_GM + j * SEG)).astype(BF16)


def _in_proj(x, ng, w_bf, qg, kg, gmean, cw, umeta, rows):
    b, s, d = x.shape
    ncols = w_bf.shape[1]
    gm_cols = ncols - COL_GM
    const = lambda shape: pl.BlockSpec(shape, lambda i, j: (0,) * len(shape))
    seg_out = pl.BlockSpec((1, rows, SEG), lambda i, j: (i, j, 0))
    out_shape = tuple(jax.ShapeDtypeStruct((b, s, SEG), BF16) for _ in range(5)) + (
        jax.ShapeDtypeStruct((b, s, gm_cols), BF16),)
    return pl.pallas_call(
        functools.partial(_in_proj_kernel, rows=rows),
        grid=(b, s // rows),
        in_specs=[
            pl.BlockSpec((1, rows, d), lambda i, j: (i, j, 0)),
            const(ng.shape),
            pl.BlockSpec(w_bf.shape, lambda i, j: (0, 0), pipeline_mode=pl.Buffered(1)),
            const(qg.shape), const(kg.shape), const(gmean.shape), const(cw.shape), const(umeta.shape),
        ],
        out_specs=(seg_out,) * 5 + (pl.BlockSpec((1, rows, gm_cols), lambda i, j: (i, j, 0)),),
        out_shape=out_shape,
        scratch_shapes=[pltpu.VMEM((rows + 8, SEG), F32)],
        compiler_params=pltpu.CompilerParams(
            dimension_semantics=("arbitrary", "arbitrary"), vmem_limit_bytes=VMEM_LIMIT),
        name="in_proj",
    )(x, ng, w_bf, qg, kg, gmean, cw, umeta)


def _attn_kernel(qa_ref, qb_ref, k_ref, v_ref, kmeta_ref, vmeta_ref, gaa_ref, gab_ref,
                 t0_ref, t1_ref, tm_ref, lam_ref, sg_ref, o_ref,
                 kbuf, vbuf, qm_buf, s_buf, own_buf, acc_buf, m_buf, *, tile, nq):
    pr = pl.program_id(2)
    nblk = tile // LANES
    n_far = nq - 1
    qts = (pr, nq - 1 - pr)

    @pl.when(pr == 0)
    def _():
        kbuf[0:LANES, :] = kmeta_ref[...]
        kbuf[LANES:, :] = k_ref[0]
        vbuf[0:LANES, 0:V_DIM] = vmeta_ref[...]
        vbuf[LANES:, 0:V_DIM] = v_ref[0]
        vbuf[:, V_DIM:] = jnp.ones((vbuf.shape[0], V_DIM), BF16)

    lane = lax.broadcasted_iota(jnp.int32, (tile, V_DIM), 1)
    for z, q_ref in enumerate((qa_ref, qb_ref)):
        q = q_ref[0]
        zero = jnp.zeros_like(q)
        qm_buf[z, 0] = jnp.where(lane < HEAD_DIM, q, zero)
        qm_buf[z, 1] = jnp.where(lane >= HEAD_DIM, q, zero)
    lane_m = lax.broadcasted_iota(jnp.int32, (1, LANES), 1)
    mrow = jnp.where(lane_m < N_META, 0.0, NEG_INF)

    def scores(qm, kk):
        return lax.dot_general(qm, kk, (((1,), (1,)), ((), ())), preferred_element_type=F32)

    def rowmax(s):
        return jnp.max(s, axis=-1, keepdims=True)

    def lanes(x, n):
        return jnp.concatenate([x] * n, axis=1)


    def far_item(i):
        z = jnp.where(i >= pr, 1, 0)
        return z, i - z * pr

    def far_rows(kt):
        return pl.ds(pl.multiple_of(kt * tile, tile), tile)

    def store_scores(slot, i):
        z, kt = far_item(i)
        kk = kbuf[far_rows(kt), :]
        meta_mask = jnp.where(kt == 0, mrow, 0.0)
        for c in range(2):
            s = scores(qm_buf[z, c], kk)
            s_buf[slot, c] = jnp.concatenate([s[:, :LANES] + meta_mask, s[:, LANES:]], axis=1)

    def far_update(slot, i, last=False):
        z, kt = (1, i - pr) if last else far_item(i)
        vv = vbuf[far_rows(kt), :]
        accs = []
        for c in range(2):
            m_old = m_buf[z, c]
            m_new = jnp.maximum(m_old, rowmax(s_buf[slot, c]))
            p = jnp.exp2((s_buf[slot, c] - lanes(m_new, nblk)).astype(BF16))
            acc = (lanes(jnp.exp2(m_old - m_new), 2) * acc_buf[z, c]
                   + jnp.dot(p, vv, preferred_element_type=F32))
            if last:
                accs.append(acc)
            else:
                m_buf[z, c] = m_new
                acc_buf[z, c] = acc
        return accs


    chunk_blocks = MXU_DIM // LANES
    chunks = [(lo, min(lo + chunk_blocks, nblk + 1), max(lo - 1, 0))
              for lo in range(0, nblk + 1, chunk_blocks)]

    def own_rows(z, lo, hi):
        start = qts[z] * tile + lo * LANES
        return pl.ds(pl.multiple_of(start, LANES), (hi - lo) * LANES)

    def store_own(z):
        for lo, hi, i_min in chunks:
            kk = kbuf[own_rows(z, lo, hi), :]
            for c in range(2):
                own_buf[z, c, i_min * LANES:, lo * LANES:hi * LANES] = scores(
                    qm_buf[z, c, i_min * LANES:, :], kk)

    def init_own(z, hc0):
        first = (qts[z] == 0) if z == 0 else None
        for c in range(2):
            t0, t1, tm = t0_ref[hc0 + c], t1_ref[hc0 + c], tm_ref[hc0 + c]

            def biased(i, j):
                blk = own_buf[z, c, i * LANES:(i + 1) * LANES, j * LANES:(j + 1) * LANES]
                if j == i + 1:
                    return blk + t0
                if j == 0 and first is not None:
                    return blk + (jnp.where(first, tm, t1) if i == 0 else jnp.where(first, mrow, 0.0))
                return blk + t1 if j == i else blk

            pblk, m_rows = {}, []
            for i in range(nblk):
                row = [biased(i, j) for j in range(i + 2)]
                m_i = jnp.broadcast_to(rowmax(jnp.concatenate(row, axis=1)), (LANES, LANES))
                m_rows.append(m_i)
                for j, blk in enumerate(row):
                    pblk[i, j] = jnp.exp2((blk - m_i).astype(BF16))
            m_buf[z, c] = jnp.concatenate(m_rows, axis=0)

            zeros = jnp.zeros((LANES, LANES), BF16)
            outs = [None] * nblk
            for lo, hi, i_min in chunks:
                p = jnp.concatenate([
                    jnp.concatenate([pblk.get((i, j), zeros) for j in range(lo, hi)], axis=1)
                    for i in range(i_min, nblk)], axis=0)
                o = jnp.dot(p, vbuf[own_rows(z, lo, hi), :], preferred_element_type=F32)
                for i in range(i_min, nblk):
                    piece = o[(i - i_min) * LANES:(i - i_min + 1) * LANES]
                    outs[i] = piece if outs[i] is None else outs[i] + piece
            acc_buf[z, c] = jnp.concatenate(outs, axis=0)

    def finalize(z, accs, ga_ref):
        o = [acc[:, :V_DIM] / acc[:, V_DIM:] for acc in accs]
        a = _rms_rows(o[0] - lam_ref[...] * o[1], sg_ref[...])
        o_ref[0, z] = (a * ga_ref[0].astype(F32)).astype(BF16)

    store_own(0)
    store_own(1)
    store_scores(0, 0)
    init_own(0, 0)
    init_own(1, 0)

    def far_pair(j, carry):
        i = 2 * j
        store_scores(1, i + 1)
        far_update(0, i)
        store_scores(0, i + 2)
        far_update(1, i + 1)
        return carry

    lax.fori_loop(0, (n_far - 1) // 2, far_pair, 0)
    last = far_update(0, n_far - 1, last=True)
    finalize(0, [acc_buf[0, 0], acc_buf[0, 1]], gaa_ref)
    finalize(1, last, gab_ref)


def _attention(q, k, v, kmeta, vmeta, ga, t0, t1, tm, lam, sg, tile):
    b, s, _ = q.shape
    nq = s // tile
    assert nq % 2 == 0
    tile_a = pl.BlockSpec((1, tile, V_DIM), lambda i, h, p: (i, p, h))
    tile_b = pl.BlockSpec((1, tile, V_DIM), lambda i, h, p: (i, nq - 1 - p, h))
    seq_spec = pl.BlockSpec((1, s, V_DIM), lambda i, h, p: (i, 0, h))
    meta_spec = pl.BlockSpec((LANES, V_DIM), lambda i, h, p: (0, h))
    bias_spec = pl.BlockSpec((2, LANES, LANES), lambda i, h, p: (h, 0, 0))
    row_spec = pl.BlockSpec((1, LANES), lambda i, h, p: (0, 0))
    out = jax.ShapeDtypeStruct((b, 2, s // 2, ATTN_WIDTH), BF16)
    return pl.pallas_call(
        functools.partial(_attn_kernel, tile=tile, nq=nq),
        grid=(b, ATTN_HEADS, nq // 2),
        in_specs=[
            tile_a, tile_b, seq_spec, seq_spec, meta_spec, meta_spec, tile_a, tile_b,
            bias_spec, bias_spec, bias_spec, row_spec, row_spec,
        ],
        out_specs=pl.BlockSpec((1, 2, tile, V_DIM), lambda i, h, p: (i, 0, p, h)),
        out_shape=out,
        scratch_shapes=[
            pltpu.VMEM((LANES + s, V_DIM), BF16),
            pltpu.VMEM((LANES + s, 2 * V_DIM), BF16),
            pltpu.VMEM((2, 2, tile, V_DIM), BF16),
            pltpu.VMEM((2, 2, tile, tile), F32),
            pltpu.VMEM((2, 2, tile, tile + LANES), F32),
            pltpu.VMEM((2, 2, tile, 2 * V_DIM), F32),
            pltpu.VMEM((2, 2, tile, LANES), F32),
        ],
        compiler_params=pltpu.CompilerParams(
            dimension_semantics=("arbitrary", "arbitrary", "arbitrary"), vmem_limit_bytes=VMEM_LIMIT),
        name="attention",
    )(q, q, k, v, kmeta, vmeta, ga, ga, t0, t1, tm, lam, sg)


def _merge_kernel(a_ref, c_ref, gm_ref, x_ref, wb_ref, wo_ref, o_ref):
    d = x_ref.shape[1]
    ya = jnp.dot(a_ref[0, 0], wb_ref[0], preferred_element_type=F32)
    yc = jnp.dot(c_ref[...], wb_ref[1], preferred_element_type=F32)
    merged = gm_ref[:, 0:d].astype(F32) * ya + gm_ref[:, d:2 * d].astype(F32) * yc
    o_ref[...] = x_ref[...] + jnp.dot(merged.astype(BF16), wo_ref[...], preferred_element_type=F32)


def _merge(a, c, gm, x, wb, wo, rows):
    n, d = x.shape
    row_spec = lambda width: pl.BlockSpec((rows, width), lambda i: (i, 0))
    half = a.shape[2] // rows
    nq = 2 * half

    def a_index(i):
        j = i % nq
        return i // nq, j // half, jnp.where(j < half, j, nq - 1 - j), 0

    return pl.pallas_call(
        _merge_kernel,
        grid=(n // rows,),
        in_specs=[
            pl.BlockSpec((1, 1, rows, a.shape[3]), a_index),
            row_spec(c.shape[1]), row_spec(gm.shape[1]), row_spec(d),
            pl.BlockSpec(wb.shape, lambda i: (0, 0, 0)),
            pl.BlockSpec(wo.shape, lambda i: (0, 0)),
        ],
        out_specs=row_spec(d),
        out_shape=jax.ShapeDtypeStruct((n, d), F32),
        compiler_params=pltpu.CompilerParams(
            dimension_semantics=("arbitrary",), vmem_limit_bytes=VMEM_LIMIT),
        name="merge",
    )(a, c, gm, x, wb, wo)


def kernel(x, meta_tokens, rel_bias, norm_g, w_in, q_norm_g, k_norm_g, lambda_q1, lambda_k1,
           lambda_q2, lambda_k2, subln_g, conv_w, w_branch, w_out):
    b, s, d = x.shape
    assert norm_g.shape[0] == 1, "single layer only"
    assert meta_tokens.shape[0] == N_META
    tile = min(ATTN_TILE, s // 2)
    rows = min(ROW_TILE, tile)
    assert s % (2 * tile) == 0 and rows == tile and tile % MXU_DIM == 0

    w_bf = w_in[0].astype(BF16)
    groups = SEG // HEAD_DIM
    qg = jnp.tile(q_norm_g[0].astype(F32) * (HEAD_DIM ** -0.5 * LOG2E), groups)[None]
    kg = jnp.tile(k_norm_g[0].astype(F32), groups)[None]
    gidx = np.arange(MXU_DIM) // HEAD_DIM
    gmean = jnp.asarray((gidx[:, None] == gidx[None, :]).astype(np.float32) / HEAD_DIM, BF16)
    lamv = jnp.stack([lambda_q1[0], lambda_k1[0], lambda_q2[0], lambda_k2[0]]).astype(F32)
    rb = rel_bias.astype(F32).reshape(N_BUCKETS, 2 * ATTN_HEADS)
    ng = norm_g.astype(F32)

    kmeta, vmeta, umeta, lam, t0, t1, tm = _prologue(
        rb, meta_tokens.astype(F32), ng, w_bf, kg, gmean, lamv)
    q, k, v, ga, c, gm = _in_proj(x, ng, w_bf, qg, kg, gmean, conv_w[0].astype(F32), umeta, rows)
    sg = subln_g.astype(F32) * (1.0 - LAM_INIT)
    a = _attention(q, k, v, kmeta, vmeta, ga, t0, t1, tm, lam, sg, tile)
    out = _merge(a, c.reshape(b * s, -1), gm.reshape(b * s, -1),
                 x.reshape(b * s, d), w_branch[0].astype(BF16), w_out[0].astype(BF16), rows)
    return out.reshape(b, s, d)
```

```python
import functools
import math

import numpy as np
import jax
import jax.numpy as jnp
from jax import lax
from jax.experimental import pallas as pl
from jax.experimental.pallas import tpu as pltpu

N_META = 16
ATTN_HEADS = 4
HEAD_DIM = 64
V_DIM = 2 * HEAD_DIM
ATTN_WIDTH = ATTN_HEADS * V_DIM
CONV_WIDTH = 512
CONV_K = 3
N_BRANCH = 2
N_BUCKETS = 32
MAX_DISTANCE = 128
EPS = 1e-6
NEG_INF = -1e30
LAM_INIT = 0.8 - 0.6 * math.exp(-0.3 * 0)
LOG2E = math.log2(math.e)

COL_Q, COL_K, COL_V, COL_GA, COL_CB, COL_CC, COL_CH, COL_GC, COL_GM = (
    0, 512, 1024, 1536, 2048, 2560, 3072, 3584, 4096)
SEG = 512

LANES = 128
MXU_DIM = 256
BF16_SUBLANES = 16
V_AUG_ROWS = V_DIM + BF16_SUBLANES
VMEM_LIMIT = 56 * 1024 * 1024

ATTN_TILE = 512
ROW_TILE = 512

F32 = jnp.float32
BF16 = jnp.bfloat16


def _bucket_thresholds():
    max_exact = N_BUCKETS // 2
    n = np.arange(0, 2 * MAX_DISTANCE)
    nf = np.maximum(n, max_exact).astype(np.float32)
    large = max_exact + (np.log(nf / np.float32(max_exact)) / np.float32(math.log(MAX_DISTANCE / max_exact))
                         * np.float32(N_BUCKETS - max_exact)).astype(np.int32)
    bucket = np.where(n < max_exact, n, np.minimum(large, N_BUCKETS - 1))
    return [int(np.argmax(bucket >= b)) for b in range(N_BUCKETS)]


_THR = _bucket_thresholds()
assert _THR[N_BUCKETS - 1] <= LANES


def _sigmoid(x):
    return 1.0 / (1.0 + jnp.exp(-x))


def _rms_rows(x, g):
    return x * lax.rsqrt(jnp.mean(x * x, axis=-1, keepdims=True) + EPS) * g


def _group_rms(x, gmean, g):
    sq = (x * x).astype(BF16)
    ms = jnp.concatenate([
        jnp.dot(sq[:, lo:lo + MXU_DIM], gmean, preferred_element_type=F32)
        for lo in range(0, x.shape[1], MXU_DIM)], axis=1)
    return x * lax.rsqrt(ms + EPS) * g


def _prologue_kernel(rb_ref, meta_ref, ng_ref, wk_ref, wv_ref, wcc_ref, wch_ref, kg_ref, gmean_ref,
                     lamv_ref, kmeta_ref, vmeta_ref, umeta_ref, lam_ref, t0_ref, t1_ref, tm_ref):
    xn = _rms_rows(meta_ref[...], ng_ref[...]).astype(BF16)
    k = jnp.dot(xn, wk_ref[...], preferred_element_type=F32)
    k = _group_rms(k, gmean_ref[...], kg_ref[...])
    v = jnp.dot(xn, wv_ref[...], preferred_element_type=F32)
    cc = jnp.dot(xn, wcc_ref[...], preferred_element_type=F32)
    ch = jnp.dot(xn, wch_ref[...], preferred_element_type=F32)
    kmeta_ref[...] = jnp.zeros(kmeta_ref.shape, BF16)
    kmeta_ref[0:N_META, :] = k.astype(BF16)
    v_rows = jnp.concatenate([v, jnp.zeros((LANES - N_META, v.shape[1]), F32)], axis=0)
    vmeta_ref[...] = v_rows.T.astype(BF16)
    umeta_ref[...] = cc * ch

    lv = lamv_ref[...]
    s1 = jnp.sum(lv[0:1] * lv[1:2], axis=-1, keepdims=True)
    s2 = jnp.sum(lv[2:3] * lv[3:4], axis=-1, keepdims=True)
    lam_ref[...] = jnp.broadcast_to(jnp.exp(s1) - jnp.exp(s2) + LAM_INIT, lam_ref.shape)

    row = lax.broadcasted_iota(jnp.int32, (LANES, LANES), 0)
    col = lax.broadcasted_iota(jnp.int32, (LANES, LANES), 1)
    d_diag = col - row
    d_sub = d_diag + LANES
    d_meta = d_diag + N_META

    def toeplitz(dist, hc):
        far = rb_ref[N_BUCKETS - 1, hc]
        t = jnp.full((LANES, LANES), (rb_ref[0, hc] - far) * LOG2E, F32)
        for b in range(1, N_BUCKETS):
            t = jnp.where(dist >= _THR[b], (rb_ref[b, hc] - far) * LOG2E, t)
        return t

    for hc in range(2 * ATTN_HEADS):
        t0_ref[hc] = jnp.where(d_diag >= 0, toeplitz(d_diag, hc), NEG_INF)
        t1_ref[hc] = toeplitz(d_sub, hc)
        tm_ref[hc] = jnp.where(row < N_META, toeplitz(d_meta, hc), NEG_INF)


def _prologue(rb, meta, ng, w_bf, kg, gmean, lamv):
    d = meta.shape[1]
    wspec = lambda j: pl.BlockSpec((d, SEG), lambda i, j=j: (0, j))
    full = lambda shape: pl.BlockSpec(shape, lambda i: (0,) * len(shape))
    blocks = jax.ShapeDtypeStruct((2 * ATTN_HEADS, LANES, LANES), F32)
    out_shape = (
        jax.ShapeDtypeStruct((LANES, SEG), BF16),
        jax.ShapeDtypeStruct((SEG, LANES), BF16),
        jax.ShapeDtypeStruct((N_META, SEG), F32),
        jax.ShapeDtypeStruct((1, LANES), F32),
        blocks,
        blocks,
        blocks,
    )
    return pl.pallas_call(
        _prologue_kernel,
        grid=(1,),
        in_specs=[
            pl.BlockSpec(memory_space=pltpu.SMEM),
            full(meta.shape), full(ng.shape),
            wspec(COL_K // SEG), wspec(COL_V // SEG), wspec(COL_CC // SEG), wspec(COL_CH // SEG),
            full(kg.shape), full(gmean.shape), full(lamv.shape),
        ],
        out_specs=tuple(full(s.shape) for s in out_shape),
        out_shape=out_shape,
        compiler_params=pltpu.CompilerParams(vmem_limit_bytes=VMEM_LIMIT),
        name="prologue",
    )(rb, meta, ng, w_bf, w_bf, w_bf, w_bf, kg, gmean, lamv)


def _in_proj_kernel(x_ref, ng_ref, w_ref, qg_ref, kg_ref, gmean_ref, cw_ref, umeta_ref,
                    q_ref, k_ref, v_ref, ga_ref, c_ref, gm_ref, ubuf, *, rows):
    t = pl.program_id(1)
    xn = _rms_rows(x_ref[0], ng_ref[...]).astype(BF16)

    def proj(lo, n=SEG):
        return jnp.dot(xn, w_ref[:, lo:lo + n], preferred_element_type=F32)

    gmean = gmean_ref[...]
    q_ref[0] = _group_rms(proj(COL_Q), gmean, qg_ref[...]).T.astype(BF16)
    k_ref[0] = _group_rms(proj(COL_K), gmean, kg_ref[...]).astype(BF16)
    v_ref[0] = proj(COL_V).astype(BF16).T
    ga = proj(COL_GA)
    ga_ref[0] = (ga * _sigmoid(ga)).astype(BF16)

    @pl.when(t == 0)
    def _():
        ubuf[0:8, :] = umeta_ref[N_META - 8:N_META, :]

    @pl.when(t > 0)
    def _():
        ubuf[0:8, :] = ubuf[rows:rows + 8, :]

    u = proj(COL_CC) * proj(COL_CH)
    ubuf[8:8 + rows, :] = u
    cw = cw_ref[...]
    conv = cw[0:1] * ubuf[6:6 + rows, :] + cw[1:2] * ubuf[7:7 + rows, :] + cw[2:3] * u
    gc = proj(COL_GC)
    c_ref[0] = (proj(COL_CB) * conv * (gc * _sigmoid(gc))).astype(BF16)

    for j in range(gm_ref.shape[2] // SEG):
        gm_ref[0, :, j * SEG:(j + 1) * SEG] = _sigmoid(proj(COL_GM + j * SEG)).astype(BF16)


def _in_proj(x, ng, w_bf, qg, kg, gmean, cw, umeta, rows):
    b, s, d = x.shape
    ncols = w_bf.shape[1]
    gm_cols = ncols - COL_GM
    const = lambda shape: pl.BlockSpec(shape, lambda i, j: (0,) * len(shape))
    seg_out = pl.BlockSpec((1, rows, SEG), lambda i, j: (i, j, 0))
    seg_out_t = pl.BlockSpec((1, SEG, rows), lambda i, j: (i, 0, j))
    seg, seg_t = jax.ShapeDtypeStruct((b, s, SEG), BF16), jax.ShapeDtypeStruct((b, SEG, s), BF16)
    out_shape = (seg_t, seg, seg_t, seg, seg, jax.ShapeDtypeStruct((b, s, gm_cols), BF16))
    return pl.pallas_call(
        functools.partial(_in_proj_kernel, rows=rows),
        grid=(b, s // rows),
        in_specs=[
            pl.BlockSpec((1, rows, d), lambda i, j: (i, j, 0)),
            const(ng.shape),
            pl.BlockSpec(w_bf.shape, lambda i, j: (0, 0), pipeline_mode=pl.Buffered(1)),
            const(qg.shape), const(kg.shape), const(gmean.shape), const(cw.shape), const(umeta.shape),
        ],
        out_specs=(seg_out_t, seg_out, seg_out_t, seg_out, seg_out,
                   pl.BlockSpec((1, rows, gm_cols), lambda i, j: (i, j, 0))),
        out_shape=out_shape,
        scratch_shapes=[pltpu.VMEM((rows + 8, SEG), F32)],
        compiler_params=pltpu.CompilerParams(
            dimension_semantics=("arbitrary", "arbitrary"), vmem_limit_bytes=VMEM_LIMIT),
        name="in_proj",
    )(x, ng, w_bf, qg, kg, gmean, cw, umeta)


def _attn_kernel(qa_ref, qb_ref, k_ref, v_ref, kmeta_ref, vmeta_ref, gaa_ref, gab_ref,
                 t0_ref, t1_ref, tm_ref, lam_ref, sg_ref, o_ref,
                 kbuf, vbuf, qm_buf, s_buf, own_buf, acc_buf, m_buf, *, tile, nq):
    pr = pl.program_id(2)
    nblk = tile // LANES
    n_far = nq - 1
    qts = (pr, nq - 1 - pr)

    @pl.when(pr == 0)
    def _():
        kbuf[0:LANES, :] = kmeta_ref[...]
        kbuf[LANES:, :] = k_ref[0]
        vbuf[0:V_DIM, 0:LANES] = vmeta_ref[...]
        vbuf[0:V_DIM, LANES:] = v_ref[0]
        vbuf[V_DIM:, :] = jnp.ones((V_AUG_ROWS - V_DIM, vbuf.shape[1]), BF16)

    chan = lax.broadcasted_iota(jnp.int32, (V_DIM, tile), 0)
    for z, q_ref in enumerate((qa_ref, qb_ref)):
        q = q_ref[0]
        zero = jnp.zeros_like(q)
        qm_buf[z, 0] = jnp.where(chan < HEAD_DIM, q, zero)
        qm_buf[z, 1] = jnp.where(chan >= HEAD_DIM, q, zero)
    key_m = lax.broadcasted_iota(jnp.int32, (LANES, LANES), 0)
    mcol = jnp.where(key_m < N_META, 0.0, NEG_INF)

    def scores(kk, qm):
        return jnp.dot(kk, qm, preferred_element_type=F32)

    def colmax(s):
        return jnp.max(s, axis=0, keepdims=True)

    def lanes(x, n):
        return jnp.concatenate([x] * n, axis=1)


    def far_item(i):
        z = jnp.where(i >= pr, 1, 0)
        return z, i - z * pr

    def far_keys(kt):
        return pl.ds(pl.multiple_of(kt * tile, tile), tile)

    def store_scores(slot, i):
        z, kt = far_item(i)
        kk = kbuf[far_keys(kt), :]
        pad_off = jnp.where(kt == 0, NEG_INF, 0.0)
        for c in range(2):
            s = scores(kk, qm_buf[z, c])
            s_buf[slot, c] = jnp.concatenate(
                [s[:N_META], s[N_META:LANES] + pad_off, s[LANES:]], axis=0)

    def far_update(slot, i, last=False):
        z, kt = (1, i - pr) if last else far_item(i)
        vv = vbuf[:, far_keys(kt)]
        accs = []
        for c in range(2):
            m_old = m_buf[z, c, 0:1, :]
            m_new = jnp.maximum(m_old, colmax(s_buf[slot, c]))
            p = jnp.exp2((s_buf[slot, c] - m_new).astype(BF16))
            acc = jnp.exp2(m_old - m_new) * acc_buf[z, c] + jnp.dot(vv, p, preferred_element_type=F32)
            if last:
                accs.append(acc)
            else:
                m_buf[z, c] = jnp.broadcast_to(m_new, (8, tile))
                acc_buf[z, c] = acc
        return accs


    chunk_blocks = MXU_DIM // LANES
    chunks = [(lo, min(lo + chunk_blocks, nblk + 1), max(lo - 1, 0))
              for lo in range(0, nblk + 1, chunk_blocks)]

    def own_keys(z, lo, hi):
        start = qts[z] * tile + lo * LANES
        return pl.ds(pl.multiple_of(start, LANES), (hi - lo) * LANES)

    def store_own(z):
        for lo, hi, i_min in chunks:
            kk = kbuf[own_keys(z, lo, hi), :]
            for c in range(2):
                own_buf[z, c, lo * LANES:hi * LANES, i_min * LANES:] = scores(
                    kk, qm_buf[z, c, :, i_min * LANES:])

    def init_own(z):
        first = (qts[z] == 0) if z == 0 else None
        for c in range(2):
            t0, t1, tm = t0_ref[c], t1_ref[c], tm_ref[c]

            def biased(j, i):
                blk = own_buf[z, c, j * LANES:(j + 1) * LANES, i * LANES:(i + 1) * LANES]
                if j == i + 1:
                    return blk + t0
                if j == 0 and first is not None:
                    return blk + (jnp.where(first, tm, t1) if i == 0 else jnp.where(first, mcol, 0.0))
                return blk + t1 if j == i else blk

            pblk, m_cols = {}, []
            for i in range(nblk):
                col = [biased(j, i) for j in range(i + 2)]
                m_i = colmax(jnp.concatenate(col, axis=0))
                m_cols.append(m_i)
                for j, blk in enumerate(col):
                    pblk[j, i] = jnp.exp2((blk - m_i).astype(BF16))
            m_buf[z, c] = jnp.broadcast_to(jnp.concatenate(m_cols, axis=1), (8, tile))

            zeros = jnp.zeros((LANES, LANES), BF16)
            outs = [None] * nblk
            for lo, hi, i_min in chunks:
                p = jnp.concatenate([
                    jnp.concatenate([pblk.get((j, i), zeros) for i in range(i_min, nblk)], axis=1)
                    for j in range(lo, hi)], axis=0)
                o = jnp.dot(vbuf[:, own_keys(z, lo, hi)], p, preferred_element_type=F32)
                for i in range(i_min, nblk):
                    piece = o[:, (i - i_min) * LANES:(i - i_min + 1) * LANES]
                    outs[i] = piece if outs[i] is None else outs[i] + piece
            acc_buf[z, c] = jnp.concatenate(outs, axis=1)

    def finalize(z, accs, ga_ref):
        o = [acc[:V_DIM] / acc[V_DIM:V_DIM + 1] for acc in accs]
        a = o[0] - lanes(lam_ref[...], nblk) * o[1]
        ms = jnp.mean(a * a, axis=0, keepdims=True)
        a = a * lax.rsqrt(ms + EPS) * lanes(sg_ref[...], nblk)
        o_ref[0, z] = (a.T * ga_ref[0].astype(F32)).astype(BF16)

    store_own(0)
    store_own(1)
    store_scores(0, 0)
    init_own(0)
    init_own(1)

    def far_pair(j, carry):
        i = 2 * j
        store_scores(1, i + 1)
        far_update(0, i)
        store_scores(0, i + 2)
        far_update(1, i + 1)
        return carry

    lax.fori_loop(0, (n_far - 1) // 2, far_pair, 0)
    last = far_update(0, n_far - 1, last=True)
    finalize(0, [acc_buf[0, 0], acc_buf[0, 1]], gaa_ref)
    finalize(1, last, gab_ref)


def _attention(q, k, v, kmeta, vmeta, ga, t0, t1, tm, lam, sg, tile):
    b, s, _ = k.shape
    nq = s // tile
    assert nq % 2 == 0
    tile_a = pl.BlockSpec((1, tile, V_DIM), lambda i, h, p: (i, p, h))
    tile_b = pl.BlockSpec((1, tile, V_DIM), lambda i, h, p: (i, nq - 1 - p, h))
    tile_at = pl.BlockSpec((1, V_DIM, tile), lambda i, h, p: (i, h, p))
    tile_bt = pl.BlockSpec((1, V_DIM, tile), lambda i, h, p: (i, h, nq - 1 - p))
    seq_spec = pl.BlockSpec((1, s, V_DIM), lambda i, h, p: (i, 0, h))
    seq_spec_t = pl.BlockSpec((1, V_DIM, s), lambda i, h, p: (i, h, 0))
    meta_spec = pl.BlockSpec((LANES, V_DIM), lambda i, h, p: (0, h))
    meta_spec_t = pl.BlockSpec((V_DIM, LANES), lambda i, h, p: (h, 0))
    bias_spec = pl.BlockSpec((2, LANES, LANES), lambda i, h, p: (h, 0, 0))
    row_spec = pl.BlockSpec((1, LANES), lambda i, h, p: (0, 0))
    gain_spec = pl.BlockSpec((V_DIM, LANES), lambda i, h, p: (0, 0))
    out = jax.ShapeDtypeStruct((b, 2, s // 2, ATTN_WIDTH), BF16)
    return pl.pallas_call(
        functools.partial(_attn_kernel, tile=tile, nq=nq),
        grid=(b, ATTN_HEADS, nq // 2),
        in_specs=[
            tile_at, tile_bt, seq_spec, seq_spec_t, meta_spec, meta_spec_t, tile_a, tile_b,
            bias_spec, bias_spec, bias_spec, row_spec, gain_spec,
        ],
        out_specs=pl.BlockSpec((1, 2, tile, V_DIM), lambda i, h, p: (i, 0, p, h)),
        out_shape=out,
        scratch_shapes=[
            pltpu.VMEM((LANES + s, V_DIM), BF16),
            pltpu.VMEM((V_AUG_ROWS, LANES + s), BF16),
            pltpu.VMEM((2, 2, V_DIM, tile), BF16),
            pltpu.VMEM((2, 2, tile, tile), F32),
            pltpu.VMEM((2, 2, tile + LANES, tile), F32),
            pltpu.VMEM((2, 2, V_AUG_ROWS, tile), F32),
            pltpu.VMEM((2, 2, 8, tile), F32),
        ],
        compiler_params=pltpu.CompilerParams(
            dimension_semantics=("arbitrary", "arbitrary", "arbitrary"), vmem_limit_bytes=VMEM_LIMIT),
        name="attention",
    )(q, q, k, v, kmeta, vmeta, ga, ga, t0, t1, tm, lam, sg)


def _merge_kernel(a_ref, c_ref, gm_ref, x_ref, wb_ref, wo_ref, o_ref):
    d = x_ref.shape[1]
    ya = jnp.dot(a_ref[0, 0], wb_ref[0], preferred_element_type=F32)
    yc = jnp.dot(c_ref[...], wb_ref[1], preferred_element_type=F32)
    merged = gm_ref[:, 0:d].astype(F32) * ya + gm_ref[:, d:2 * d].astype(F32) * yc
    o_ref[...] = x_ref[...] + jnp.dot(merged.astype(BF16), wo_ref[...], preferred_element_type=F32)


def _merge(a, c, gm, x, wb, wo, rows):
    n, d = x.shape
    row_spec = lambda width: pl.BlockSpec((rows, width), lambda i: (i, 0))
    half = a.shape[2] // rows
    nq = 2 * half

    def a_index(i):
        j = i % nq
        return i // nq, j // half, jnp.where(j < half, j, nq - 1 - j), 0

    return pl.pallas_call(
        _merge_kernel,
        grid=(n // rows,),
        in_specs=[
            pl.BlockSpec((1, 1, rows, a.shape[3]), a_index),
            row_spec(c.shape[1]), row_spec(gm.shape[1]), row_spec(d),
            pl.BlockSpec(wb.shape, lambda i: (0, 0, 0)),
            pl.BlockSpec(wo.shape, lambda i: (0, 0)),
        ],
        out_specs=row_spec(d),
        out_shape=jax.ShapeDtypeStruct((n, d), F32),
        compiler_params=pltpu.CompilerParams(
            dimension_semantics=("arbitrary",), vmem_limit_bytes=VMEM_LIMIT),
        name="merge",
    )(a, c, gm, x, wb, wo)


def kernel(x, meta_tokens, rel_bias, norm_g, w_in, q_norm_g, k_norm_g, lambda_q1, lambda_k1,
           lambda_q2, lambda_k2, subln_g, conv_w, w_branch, w_out):
    b, s, d = x.shape
    assert norm_g.shape[0] == 1, "single layer only"
    assert meta_tokens.shape[0] == N_META
    tile = min(ATTN_TILE, s // 2)
    rows = min(ROW_TILE, tile)
    assert s % (2 * tile) == 0 and rows == tile and tile % MXU_DIM == 0

    w_bf = w_in[0].astype(BF16)
    groups = SEG // HEAD_DIM
    qg = jnp.tile(q_norm_g[0].astype(F32) * (HEAD_DIM ** -0.5 * LOG2E), groups)[None]
    kg = jnp.tile(k_norm_g[0].astype(F32), groups)[None]
    gidx = np.arange(MXU_DIM) // HEAD_DIM
    gmean = jnp.asarray((gidx[:, None] == gidx[None, :]).astype(np.float32) / HEAD_DIM, BF16)
    lamv = jnp.stack([lambda_q1[0], lambda_k1[0], lambda_q2[0], lambda_k2[0]]).astype(F32)
    rb = rel_bias.astype(F32).reshape(N_BUCKETS, 2 * ATTN_HEADS)
    ng = norm_g.astype(F32)

    kmeta, vmeta, umeta, lam, t0, t1, tm = _prologue(
        rb, meta_tokens.astype(F32), ng, w_bf, kg, gmean, lamv)
    q, k, v, ga, c, gm = _in_proj(x, ng, w_bf, qg, kg, gmean, conv_w[0].astype(F32), umeta, rows)
    sg = jnp.broadcast_to((subln_g[0].astype(F32) * (1.0 - LAM_INIT))[:, None], (V_DIM, LANES))
    a = _attention(q, k, v, kmeta, vmeta, ga, t0, t1, tm, lam, sg, tile)
    out = _merge(a, c.reshape(b * s, -1), gm.reshape(b * s, -1),
                 x.reshape(b * s, d), w_branch[0].astype(BF16), w_out[0].astype(BF16), rows)
    return out.reshape(b, s, d)
```

```python
import functools
import math

import numpy as np
import jax
import jax.numpy as jnp
from jax import lax
from jax.experimental import pallas as pl
from jax.experimental.pallas import tpu as pltpu

N_META = 16
ATTN_HEADS = 4
HEAD_DIM = 64
V_DIM = 2 * HEAD_DIM
ATTN_WIDTH = ATTN_HEADS * V_DIM
CONV_WIDTH = 512
CONV_K = 3
N_BRANCH = 2
N_BUCKETS = 32
MAX_DISTANCE = 128
EPS = 1e-6
NEG_INF = -1e30
LAM_INIT = 0.8 - 0.6 * math.exp(-0.3 * 0)
LOG2E = math.log2(math.e)

COL_Q, COL_K, COL_V, COL_GA, COL_CB, COL_CC, COL_CH, COL_GC, COL_GM = (
    0, 512, 1024, 1536, 2048, 2560, 3072, 3584, 4096)
SEG = 512

LANES = 128
MXU_DIM = 256
VMEM_LIMIT = 56 * 1024 * 1024

ATTN_TILE = 512
ROW_TILE = 512

F32 = jnp.float32
BF16 = jnp.bfloat16


def _bucket_thresholds():
    max_exact = N_BUCKETS // 2
    n = np.arange(0, 2 * MAX_DISTANCE)
    nf = np.maximum(n, max_exact).astype(np.float32)
    large = max_exact + (np.log(nf / np.float32(max_exact)) / np.float32(math.log(MAX_DISTANCE / max_exact))
                         * np.float32(N_BUCKETS - max_exact)).astype(np.int32)
    bucket = np.where(n < max_exact, n, np.minimum(large, N_BUCKETS - 1))
    return [int(np.argmax(bucket >= b)) for b in range(N_BUCKETS)]


_THR = _bucket_thresholds()
assert _THR[N_BUCKETS - 1] <= LANES


def _sigmoid(x):
    return 0.5 * jnp.tanh(0.5 * x) + 0.5


def _rms_rows(x, g):
    return x * lax.rsqrt(jnp.mean(x * x, axis=-1, keepdims=True) + EPS) * g


def _group_rms(x, gmean, g):
    sq = (x * x).astype(BF16)
    ms = jnp.concatenate([
        jnp.dot(sq[:, lo:lo + MXU_DIM], gmean, preferred_element_type=F32)
        for lo in range(0, x.shape[1], MXU_DIM)], axis=1)
    return x * lax.rsqrt(ms + EPS) * g


def _prologue_kernel(rb_ref, meta_ref, ng_ref, wk_ref, wv_ref, wcc_ref, wch_ref, kg_ref, gmean_ref,
                     lamv_ref, kmeta_ref, vmeta_ref, umeta_ref, lam_ref, t0_ref, t1_ref, tm_ref):
    xn = _rms_rows(meta_ref[...], ng_ref[...]).astype(BF16)
    k = jnp.dot(xn, wk_ref[...], preferred_element_type=F32)
    k = _group_rms(k, gmean_ref[...], kg_ref[...])
    v = jnp.dot(xn, wv_ref[...], preferred_element_type=F32)
    cc = jnp.dot(xn, wcc_ref[...], preferred_element_type=F32)
    ch = jnp.dot(xn, wch_ref[...], preferred_element_type=F32)
    kmeta_ref[...] = jnp.zeros(kmeta_ref.shape, BF16)
    vmeta_ref[...] = jnp.zeros(vmeta_ref.shape, BF16)
    kmeta_ref[0:N_META, :] = k.astype(BF16)
    vmeta_ref[0:N_META, :] = v.astype(BF16)
    umeta_ref[...] = cc * ch

    lv = lamv_ref[...]
    s1 = jnp.sum(lv[0:1] * lv[1:2], axis=-1, keepdims=True)
    s2 = jnp.sum(lv[2:3] * lv[3:4], axis=-1, keepdims=True)
    lam_ref[...] = jnp.broadcast_to(jnp.exp(s1) - jnp.exp(s2) + LAM_INIT, lam_ref.shape)

    row = lax.broadcasted_iota(jnp.int32, (LANES, LANES), 0)
    col = lax.broadcasted_iota(jnp.int32, (LANES, LANES), 1)
    d_diag = row - col
    d_sub = d_diag + LANES
    d_meta = d_diag + N_META

    def toeplitz(dist, hc):
        far = rb_ref[N_BUCKETS - 1, hc]
        t = jnp.full((LANES, LANES), (rb_ref[0, hc] - far) * LOG2E, F32)
        for b in range(1, N_BUCKETS):
            t = jnp.where(dist >= _THR[b], (rb_ref[b, hc] - far) * LOG2E, t)
        return t

    for hc in range(2 * ATTN_HEADS):
        t0_ref[hc] = jnp.where(d_diag >= 0, toeplitz(d_diag, hc), NEG_INF)
        t1_ref[hc] = toeplitz(d_sub, hc)
        tm_ref[hc] = jnp.where(col < N_META, toeplitz(d_meta, hc), NEG_INF)


def _prologue(rb, meta, ng, w_bf, kg, gmean, lamv):
    d = meta.shape[1]
    wspec = lambda j: pl.BlockSpec((d, SEG), lambda i, j=j: (0, j))
    full = lambda shape: pl.BlockSpec(shape, lambda i: (0,) * len(shape))
    blocks = jax.ShapeDtypeStruct((2 * ATTN_HEADS, LANES, LANES), F32)
    out_shape = (
        jax.ShapeDtypeStruct((LANES, SEG), BF16),
        jax.ShapeDtypeStruct((LANES, SEG), BF16),
        jax.ShapeDtypeStruct((N_META, SEG), F32),
        jax.ShapeDtypeStruct((1, LANES), F32),
        blocks,
        blocks,
        blocks,
    )
    return pl.pallas_call(
        _prologue_kernel,
        grid=(1,),
        in_specs=[
            pl.BlockSpec(memory_space=pltpu.SMEM),
            full(meta.shape), full(ng.shape),
            wspec(COL_K // SEG), wspec(COL_V // SEG), wspec(COL_CC // SEG), wspec(COL_CH // SEG),
            full(kg.shape), full(gmean.shape), full(lamv.shape),
        ],
        out_specs=tuple(full(s.shape) for s in out_shape),
        out_shape=out_shape,
        compiler_params=pltpu.CompilerParams(vmem_limit_bytes=VMEM_LIMIT),
        name="prologue",
    )(rb, meta, ng, w_bf, w_bf, w_bf, w_bf, kg, gmean, lamv)


def _in_proj_kernel(x_ref, ng_ref, w_ref, qg_ref, kg_ref, gmean_ref, cw_ref, umeta_ref,
                    q_ref, k_ref, v_ref, ga_ref, c_ref, gm_ref, ubuf, *, rows):
    t = pl.program_id(1)

    @pl.when(t == 0)
    def _():
        ubuf[0:8, :] = umeta_ref[N_META - 8:N_META, :]

    @pl.when(t > 0)
    def _():
        ubuf[0:8, :] = ubuf[rows:rows + 8, :]

    xn = _rms_rows(x_ref[0], ng_ref[...]).astype(BF16)

    def proj(lo, n=SEG):
        return jnp.dot(xn, w_ref[:, lo:lo + n], preferred_element_type=F32)

    for j in range(gm_ref.shape[2] // SEG):
        gm_ref[0, :, j * SEG:(j + 1) * SEG] = _sigmoid(proj(COL_GM + j * SEG)).astype(BF16)
    ga = proj(COL_GA)
    ga_ref[0] = (ga * _sigmoid(ga)).astype(BF16)

    u = proj(COL_CC) * proj(COL_CH)
    ubuf[8:8 + rows, :] = u
    cw = cw_ref[...]
    conv = cw[0:1] * ubuf[6:6 + rows, :] + cw[1:2] * ubuf[7:7 + rows, :] + cw[2:3] * u
    gc = proj(COL_GC)
    c_ref[0] = (proj(COL_CB) * conv * (gc * _sigmoid(gc))).astype(BF16)

    gmean = gmean_ref[...]
    q_ref[0] = _group_rms(proj(COL_Q), gmean, qg_ref[...]).astype(BF16)
    k_ref[0] = _group_rms(proj(COL_K), gmean, kg_ref[...]).astype(BF16)
    v_ref[0] = proj(COL_V).astype(BF16)


def _in_proj(x, ng, w_bf, qg, kg, gmean, cw, umeta, rows):
    b, s, d = x.shape
    ncols = w_bf.shape[1]
    gm_cols = ncols - COL_GM
    const = lambda shape: pl.BlockSpec(shape, lambda i, j: (0,) * len(shape))
    seg_out = pl.BlockSpec((1, rows, SEG), lambda i, j: (i, j, 0))
    out_shape = tuple(jax.ShapeDtypeStruct((b, s, SEG), BF16) for _ in range(5)) + (
        jax.ShapeDtypeStruct((b, s, gm_cols), BF16),)
    return pl.pallas_call(
        functools.partial(_in_proj_kernel, rows=rows),
        grid=(b, s // rows),
        in_specs=[
            pl.BlockSpec((1, rows, d), lambda i, j: (i, j, 0)),
            const(ng.shape),
            pl.BlockSpec(w_bf.shape, lambda i, j: (0, 0), pipeline_mode=pl.Buffered(1)),
            const(qg.shape), const(kg.shape), const(gmean.shape), const(cw.shape), const(umeta.shape),
        ],
        out_specs=(seg_out,) * 5 + (pl.BlockSpec((1, rows, gm_cols), lambda i, j: (i, j, 0)),),
        out_shape=out_shape,
        scratch_shapes=[pltpu.VMEM((rows + 8, SEG), F32)],
        compiler_params=pltpu.CompilerParams(
            dimension_semantics=("arbitrary", "arbitrary"), vmem_limit_bytes=VMEM_LIMIT),
        name="in_proj",
    )(x, ng, w_bf, qg, kg, gmean, cw, umeta)


def _attn_kernel(qa_ref, qb_ref, k_ref, v_ref, kmeta_ref, vmeta_ref, gaa_ref, gab_ref,
                 t0_ref, t1_ref, tm_ref, lam_ref, sg_ref, o_ref,
                 kbuf, vbuf, qm_buf, s_buf, own_buf, acc_buf, m_buf, *, tile, nq):
    pr = pl.program_id(2)
    nblk = tile // LANES
    n_far = nq - 1
    qts = (pr, nq - 1 - pr)
    slot0_items = nq // 2 - 1

    @pl.when(pr == 0)
    def _():
        kbuf[0:LANES, :] = kmeta_ref[...]
        kbuf[LANES:, :] = k_ref[0]
        vbuf[0:LANES, 0:V_DIM] = vmeta_ref[...]
        vbuf[LANES:, 0:V_DIM] = v_ref[0]
        vbuf[:, V_DIM:] = jnp.ones((vbuf.shape[0], V_DIM), BF16)

    lane = lax.broadcasted_iota(jnp.int32, (tile, V_DIM), 1)
    for z, q_ref in enumerate((qa_ref, qb_ref)):
        q = q_ref[0]
        zero = jnp.zeros_like(q)
        qm_buf[z, 0] = jnp.where(lane < HEAD_DIM, q, zero)
        qm_buf[z, 1] = jnp.where(lane >= HEAD_DIM, q, zero)
    lane_m = lax.broadcasted_iota(jnp.int32, (1, LANES), 1)
    mrow = jnp.where(lane_m < N_META, 0.0, NEG_INF)

    def scores(qm, kk):
        return lax.dot_general(qm, kk, (((1,), (1,)), ((), ())), preferred_element_type=F32)

    def rowmax(s):
        return jnp.max(s, axis=-1, keepdims=True)

    def lanes(x, n):
        return jnp.concatenate([x] * n, axis=1)


    def far_item(i):
        if isinstance(i, int) and i >= slot0_items:
            return 1, i - pr
        z = jnp.where(i >= pr, 1, 0)
        return z, i - z * pr

    def far_rows(kt):
        return pl.ds(pl.multiple_of(kt * tile, tile), tile)

    def store_scores(slot, i):
        z, kt = far_item(i)
        kk = kbuf[far_rows(kt), :]
        meta_mask = jnp.where(kt == 0, mrow, 0.0)
        for c in range(2):
            s = scores(qm_buf[z, c], kk)
            s_buf[slot, c] = jnp.concatenate([s[:, :LANES] + meta_mask, s[:, LANES:]], axis=1)

    def far_update(slot, i, last=False):
        z, kt = far_item(i)
        vv = vbuf[far_rows(kt), :]
        accs = []
        for c in range(2):
            m_old = m_buf[z, c]
            m_new = jnp.maximum(m_old, rowmax(s_buf[slot, c]))
            p = jnp.exp2((s_buf[slot, c] - lanes(m_new, nblk)).astype(BF16))
            acc = (lanes(jnp.exp2(m_old - m_new), 2) * acc_buf[z, c]
                   + jnp.dot(p, vv, preferred_element_type=F32))
            if last:
                accs.append(acc)
            else:
                m_buf[z, c] = m_new
                acc_buf[z, c] = acc
        return accs


    chunk_blocks = MXU_DIM // LANES
    chunks = [(lo, min(lo + chunk_blocks, nblk + 1), max(lo - 1, 0))
              for lo in range(0, nblk + 1, chunk_blocks)]

    def own_rows(z, lo, hi):
        start = qts[z] * tile + lo * LANES
        return pl.ds(pl.multiple_of(start, LANES), (hi - lo) * LANES)

    def store_own(z):
        for lo, hi, i_min in chunks:
            kk = kbuf[own_rows(z, lo, hi), :]
            for c in range(2):
                own_buf[z, c, i_min * LANES:, lo * LANES:hi * LANES] = scores(
                    qm_buf[z, c, i_min * LANES:, :], kk)

    def init_own(z):
        first = (qts[z] == 0) if z == 0 else None
        for c in range(2):
            t0, t1, tm = t0_ref[c], t1_ref[c], tm_ref[c]

            def biased(i, j):
                blk = own_buf[z, c, i * LANES:(i + 1) * LANES, j * LANES:(j + 1) * LANES]
                if j == i + 1:
                    return blk + t0
                if j == 0 and first is not None:
                    return blk + (jnp.where(first, tm, t1) if i == 0 else jnp.where(first, mrow, 0.0))
                return blk + t1 if j == i else blk

            pblk, m_rows = {}, []
            for i in range(nblk):
                row = [biased(i, j) for j in range(i + 2)]
                m_i = jnp.broadcast_to(rowmax(jnp.concatenate(row, axis=1)), (LANES, LANES))
                m_rows.append(m_i)
                for j, blk in enumerate(row):
                    pblk[i, j] = jnp.exp2((blk - m_i).astype(BF16))
            m_buf[z, c] = jnp.concatenate(m_rows, axis=0)

            zeros = jnp.zeros((LANES, LANES), BF16)
            outs = [None] * nblk
            for lo, hi, i_min in chunks:
                p = jnp.concatenate([
                    jnp.concatenate([pblk.get((i, j), zeros) for j in range(lo, hi)], axis=1)
                    for i in range(i_min, nblk)], axis=0)
                o = jnp.dot(p, vbuf[own_rows(z, lo, hi), :], preferred_element_type=F32)
                for i in range(i_min, nblk):
                    piece = o[(i - i_min) * LANES:(i - i_min + 1) * LANES]
                    outs[i] = piece if outs[i] is None else outs[i] + piece
            acc_buf[z, c] = jnp.concatenate(outs, axis=0)

    def finalize(z, accs, ga_ref):
        o = [acc[:, :V_DIM] / acc[:, V_DIM:] for acc in accs]
        a = _rms_rows(o[0] - lam_ref[...] * o[1], sg_ref[...])
        o_ref[0, z] = (a * ga_ref[0].astype(F32)).astype(BF16)

    store_own(0)
    store_own(1)
    store_scores(0, 0)
    init_own(0)
    init_own(1)

    def far_pair(j, carry):
        i = 2 * j
        store_scores(1, i + 1)
        far_update(0, i)
        store_scores(0, i + 2)
        far_update(1, i + 1)
        return carry

    n_pairs = max((n_far - 3) // 2, 0)
    lax.fori_loop(0, n_pairs, far_pair, 0)
    slot0_done = False
    for i in range(2 * n_pairs, n_far):
        if i >= slot0_items and not slot0_done:
            finalize(0, [acc_buf[0, 0], acc_buf[0, 1]], gaa_ref)
            slot0_done = True
        if i + 1 < n_far:
            store_scores((i + 1) % 2, i + 1)
        last = far_update(i % 2, i, last=i == n_far - 1)
    finalize(1, last, gab_ref)


def _attention(q, k, v, kmeta, vmeta, ga, t0, t1, tm, lam, sg, tile):
    b, s, _ = q.shape
    nq = s // tile
    assert nq % 2 == 0
    tile_a = pl.BlockSpec((1, tile, V_DIM), lambda i, h, p: (i, p, h))
    tile_b = pl.BlockSpec((1, tile, V_DIM), lambda i, h, p: (i, nq - 1 - p, h))
    seq_spec = pl.BlockSpec((1, s, V_DIM), lambda i, h, p: (i, 0, h))
    meta_spec = pl.BlockSpec((LANES, V_DIM), lambda i, h, p: (0, h))
    bias_spec = pl.BlockSpec((2, LANES, LANES), lambda i, h, p: (h, 0, 0))
    row_spec = pl.BlockSpec((1, LANES), lambda i, h, p: (0, 0))
    out = jax.ShapeDtypeStruct((b, 2, s // 2, ATTN_WIDTH), BF16)
    return pl.pallas_call(
        functools.partial(_attn_kernel, tile=tile, nq=nq),
        grid=(b, ATTN_HEADS, nq // 2),
        in_specs=[
            tile_a, tile_b, seq_spec, seq_spec, meta_spec, meta_spec, tile_a, tile_b,
            bias_spec, bias_spec, bias_spec, row_spec, row_spec,
        ],
        out_specs=pl.BlockSpec((1, 2, tile, V_DIM), lambda i, h, p: (i, 0, p, h)),
        out_shape=out,
        scratch_shapes=[
            pltpu.VMEM((LANES + s, V_DIM), BF16),
            pltpu.VMEM((LANES + s, 2 * V_DIM), BF16),
            pltpu.VMEM((2, 2, tile, V_DIM), BF16),
            pltpu.VMEM((2, 2, tile, tile), F32),
            pltpu.VMEM((2, 2, tile, tile + LANES), F32),
            pltpu.VMEM((2, 2, tile, 2 * V_DIM), F32),
            pltpu.VMEM((2, 2, tile, LANES), F32),
        ],
        compiler_params=pltpu.CompilerParams(
            dimension_semantics=("arbitrary", "arbitrary", "arbitrary"), vmem_limit_bytes=VMEM_LIMIT),
        name="attention",
    )(q, q, k, v, kmeta, vmeta, ga, ga, t0, t1, tm, lam, sg)


def _merge_kernel(a_ref, c_ref, gm_ref, x_ref, wb_ref, wo_ref, o_ref):
    d = x_ref.shape[1]
    ya = jnp.dot(a_ref[0, 0], wb_ref[0], preferred_element_type=F32)
    yc = jnp.dot(c_ref[...], wb_ref[1], preferred_element_type=F32)
    merged = gm_ref[:, 0:d].astype(F32) * ya + gm_ref[:, d:2 * d].astype(F32) * yc
    o_ref[...] = x_ref[...] + jnp.dot(merged.astype(BF16), wo_ref[...], preferred_element_type=F32)


def _merge(a, c, gm, x, wb, wo, rows):
    n, d = x.shape
    row_spec = lambda width: pl.BlockSpec((rows, width), lambda i: (i, 0))
    half = a.shape[2] // rows
    nq = 2 * half

    def a_index(i):
        j = i % nq
        return i // nq, j // half, jnp.where(j < half, j, nq - 1 - j), 0

    return pl.pallas_call(
        _merge_kernel,
        grid=(n // rows,),
        in_specs=[
            pl.BlockSpec((1, 1, rows, a.shape[3]), a_index),
            row_spec(c.shape[1]), row_spec(gm.shape[1]), row_spec(d),
            pl.BlockSpec(wb.shape, lambda i: (0, 0, 0)),
            pl.BlockSpec(wo.shape, lambda i: (0, 0)),
        ],
        out_specs=row_spec(d),
        out_shape=jax.ShapeDtypeStruct((n, d), F32),
        compiler_params=pltpu.CompilerParams(
            dimension_semantics=("arbitrary",), vmem_limit_bytes=VMEM_LIMIT),
        name="merge",
    )(a, c, gm, x, wb, wo)


def kernel(x, meta_tokens, rel_bias, norm_g, w_in, q_norm_g, k_norm_g, lambda_q1, lambda_k1,
           lambda_q2, lambda_k2, subln_g, conv_w, w_branch, w_out):
    b, s, d = x.shape
    assert norm_g.shape[0] == 1, "single layer only"
    assert meta_tokens.shape[0] == N_META
    tile = min(ATTN_TILE, s // 2)
    rows = min(ROW_TILE, tile)
    assert s % (2 * tile) == 0 and rows == tile and tile % MXU_DIM == 0

    w_bf = w_in[0].astype(BF16)
    groups = SEG // HEAD_DIM
    qg = jnp.tile(q_norm_g[0].astype(F32) * (HEAD_DIM ** -0.5 * LOG2E), groups)[None]
    kg = jnp.tile(k_norm_g[0].astype(F32), groups)[None]
    gidx = np.arange(MXU_DIM) // HEAD_DIM
    gmean = jnp.asarray((gidx[:, None] == gidx[None, :]).astype(np.float32) / HEAD_DIM, BF16)
    lamv = jnp.stack([lambda_q1[0], lambda_k1[0], lambda_q2[0], lambda_k2[0]]).astype(F32)
    rb = rel_bias.astype(F32).reshape(N_BUCKETS, 2 * ATTN_HEADS)
    ng = norm_g.astype(F32)

    kmeta, vmeta, umeta, lam, t0, t1, tm = _prologue(
        rb, meta_tokens.astype(F32), ng, w_bf, kg, gmean, lamv)
    q, k, v, ga, c, gm = _in_proj(x, ng, w_bf, qg, kg, gmean, conv_w[0].astype(F32), umeta, rows)
    sg = subln_g.astype(F32) * (1.0 - LAM_INIT)
    a = _attention(q, k, v, kmeta, vmeta, ga, t0, t1, tm, lam, sg, tile)
    out = _merge(a, c.reshape(b * s, -1), gm.reshape(b * s, -1),
                 x.reshape(b * s, d), w_branch[0].astype(BF16), w_out[0].astype(BF16), rows)
    return out.reshape(b, s, d)
```

```python
import functools
import math

import numpy as np
import jax
import jax.numpy as jnp
from jax import lax
from jax.experimental import pallas as pl
from jax.experimental.pallas import tpu as pltpu

N_META = 16
ATTN_HEADS = 4
HEAD_DIM = 64
V_DIM = 2 * HEAD_DIM
ATTN_WIDTH = ATTN_HEADS * V_DIM
CONV_WIDTH = 512
CONV_K = 3
N_BRANCH = 2
N_BUCKETS = 32
MAX_DISTANCE = 128
EPS = 1e-6
NEG_INF = -1e30
LAM_INIT = 0.8 - 0.6 * math.exp(-0.3 * 0)
LOG2E = math.log2(math.e)

COL_Q, COL_K, COL_V, COL_GA, COL_CB, COL_CC, COL_CH, COL_GC, COL_GM = (
    0, 512, 1024, 1536, 2048, 2560, 3072, 3584, 4096)
SEG = 512

LANES = 128
MXU_DIM = 256
VMEM_LIMIT = 56 * 1024 * 1024

ATTN_TILE = 1024
ROW_TILE = 512

F32 = jnp.float32
BF16 = jnp.bfloat16


def _bucket_thresholds():
    max_exact = N_BUCKETS // 2
    n = np.arange(0, 2 * MAX_DISTANCE)
    nf = np.maximum(n, max_exact).astype(np.float32)
    large = max_exact + (np.log(nf / np.float32(max_exact)) / np.float32(math.log(MAX_DISTANCE / max_exact))
                         * np.float32(N_BUCKETS - max_exact)).astype(np.int32)
    bucket = np.where(n < max_exact, n, np.minimum(large, N_BUCKETS - 1))
    return [int(np.argmax(bucket >= b)) for b in range(N_BUCKETS)]


_THR = _bucket_thresholds()
assert _THR[N_BUCKETS - 1] <= LANES


def _sigmoid(x):
    return 0.5 * jnp.tanh(0.5 * x) + 0.5


def _rms_rows(x, g):
    return x * lax.rsqrt(jnp.mean(x * x, axis=-1, keepdims=True) + EPS) * g


def _group_rms(x, gmean, g):
    sq = (x * x).astype(BF16)
    ms = jnp.concatenate([
        jnp.dot(sq[:, lo:lo + MXU_DIM], gmean, preferred_element_type=F32)
        for lo in range(0, x.shape[1], MXU_DIM)], axis=1)
    return x * lax.rsqrt(ms + EPS) * g


def _prologue_kernel(rb_ref, meta_ref, ng_ref, wk_ref, wv_ref, wcc_ref, wch_ref, kg_ref, gmean_ref,
                     lamv_ref, kmeta_ref, vmeta_ref, umeta_ref, lam_ref, t0_ref, t1_ref, tm_ref):
    xn = _rms_rows(meta_ref[...], ng_ref[...]).astype(BF16)
    k = jnp.dot(xn, wk_ref[...], preferred_element_type=F32)
    k = _group_rms(k, gmean_ref[...], kg_ref[...])
    v = jnp.dot(xn, wv_ref[...], preferred_element_type=F32)
    cc = jnp.dot(xn, wcc_ref[...], preferred_element_type=F32)
    ch = jnp.dot(xn, wch_ref[...], preferred_element_type=F32)
    kmeta_ref[...] = jnp.zeros(kmeta_ref.shape, BF16)
    vmeta_ref[...] = jnp.zeros(vmeta_ref.shape, BF16)
    kmeta_ref[0:N_META, :] = k.astype(BF16)
    vmeta_ref[0:N_META, :] = v.astype(BF16)
    umeta_ref[...] = cc * ch

    lv = lamv_ref[...]
    s1 = jnp.sum(lv[0:1] * lv[1:2], axis=-1, keepdims=True)
    s2 = jnp.sum(lv[2:3] * lv[3:4], axis=-1, keepdims=True)
    lam_ref[...] = jnp.broadcast_to(jnp.exp(s1) - jnp.exp(s2) + LAM_INIT, lam_ref.shape)

    row = lax.broadcasted_iota(jnp.int32, (LANES, LANES), 0)
    col = lax.broadcasted_iota(jnp.int32, (LANES, LANES), 1)
    d_diag = row - col
    d_sub = d_diag + LANES
    d_meta = d_diag + N_META

    def toeplitz(dist, hc):
        far = rb_ref[N_BUCKETS - 1, hc]
        t = jnp.full((LANES, LANES), (rb_ref[0, hc] - far) * LOG2E, F32)
        for b in range(1, N_BUCKETS):
            t = jnp.where(dist >= _THR[b], (rb_ref[b, hc] - far) * LOG2E, t)
        return t

    for hc in range(2 * ATTN_HEADS):
        t0_ref[hc] = jnp.where(d_diag >= 0, toeplitz(d_diag, hc), NEG_INF)
        t1_ref[hc] = toeplitz(d_sub, hc)
        tm_ref[hc] = jnp.where(col < N_META, toeplitz(d_meta, hc), NEG_INF)


def _prologue(rb, meta, ng, w_bf, kg, gmean, lamv):
    d = meta.shape[1]
    wspec = lambda j: pl.BlockSpec((d, SEG), lambda i, j=j: (0, j))
    full = lambda shape: pl.BlockSpec(shape, lambda i: (0,) * len(shape))
    blocks = jax.ShapeDtypeStruct((2 * ATTN_HEADS, LANES, LANES), F32)
    out_shape = (
        jax.ShapeDtypeStruct((LANES, SEG), BF16),
        jax.ShapeDtypeStruct((LANES, SEG), BF16),
        jax.ShapeDtypeStruct((N_META, SEG), F32),
        jax.ShapeDtypeStruct((1, LANES), F32),
        blocks,
        blocks,
        blocks,
    )
    return pl.pallas_call(
        _prologue_kernel,
        grid=(1,),
        in_specs=[
            pl.BlockSpec(memory_space=pltpu.SMEM),
            full(meta.shape), full(ng.shape),
            wspec(COL_K // SEG), wspec(COL_V // SEG), wspec(COL_CC // SEG), wspec(COL_CH // SEG),
            full(kg.shape), full(gmean.shape), full(lamv.shape),
        ],
        out_specs=tuple(full(s.shape) for s in out_shape),
        out_shape=out_shape,
        compiler_params=pltpu.CompilerParams(vmem_limit_bytes=VMEM_LIMIT),
        name="prologue",
    )(rb, meta, ng, w_bf, w_bf, w_bf, w_bf, kg, gmean, lamv)


def _in_proj_kernel(x_ref, ng_ref, w_ref, qg_ref, kg_ref, gmean_ref, cw_ref, umeta_ref,
                    q_ref, k_ref, v_ref, ga_ref, c_ref, gm_ref, ubuf, *, rows):
    t = pl.program_id(1)

    @pl.when(t == 0)
    def _():
        ubuf[0:8, :] = umeta_ref[N_META - 8:N_META, :]

    @pl.when(t > 0)
    def _():
        ubuf[0:8, :] = ubuf[rows:rows + 8, :]

    xn = _rms_rows(x_ref[0], ng_ref[...]).astype(BF16)

    def proj(lo, n=SEG):
        return jnp.dot(xn, w_ref[:, lo:lo + n], preferred_element_type=F32)

    for j in range(gm_ref.shape[2] // SEG):
        gm_ref[0, :, j * SEG:(j + 1) * SEG] = _sigmoid(proj(COL_GM + j * SEG)).astype(BF16)
    ga = proj(COL_GA)
    ga_ref[0] = (ga * _sigmoid(ga)).astype(BF16)

    u = proj(COL_CC) * proj(COL_CH)
    ubuf[8:8 + rows, :] = u
    cw = cw_ref[...]
    conv = cw[0:1] * ubuf[6:6 + rows, :] + cw[1:2] * ubuf[7:7 + rows, :] + cw[2:3] * u
    gc = proj(COL_GC)
    c_ref[0] = (proj(COL_CB) * conv * (gc * _sigmoid(gc))).astype(BF16)

    gmean = gmean_ref[...]
    q_ref[0] = _group_rms(proj(COL_Q), gmean, qg_ref[...]).astype(BF16)
    k_ref[0] = _group_rms(proj(COL_K), gmean, kg_ref[...]).astype(BF16)
    v_ref[0] = proj(COL_V).astype(BF16)


def _in_proj(x, ng, w_bf, qg, kg, gmean, cw, umeta, rows):
    b, s, d = x.shape
    ncols = w_bf.shape[1]
    gm_cols = ncols - COL_GM
    const = lambda shape: pl.BlockSpec(shape, lambda i, j: (0,) * len(shape))
    seg_out = pl.BlockSpec((1, rows, SEG), lambda i, j: (i, j, 0))
    out_shape = tuple(jax.ShapeDtypeStruct((b, s, SEG), BF16) for _ in range(5)) + (
        jax.ShapeDtypeStruct((b, s, gm_cols), BF16),)
    return pl.pallas_call(
        functools.partial(_in_proj_kernel, rows=rows),
        grid=(b, s // rows),
        in_specs=[
            pl.BlockSpec((1, rows, d), lambda i, j: (i, j, 0)),
            const(ng.shape),
            pl.BlockSpec(w_bf.shape, lambda i, j: (0, 0), pipeline_mode=pl.Buffered(1)),
            const(qg.shape), const(kg.shape), const(gmean.shape), const(cw.shape), const(umeta.shape),
        ],
        out_specs=(seg_out,) * 5 + (pl.BlockSpec((1, rows, gm_cols), lambda i, j: (i, j, 0)),),
        out_shape=out_shape,
        scratch_shapes=[pltpu.VMEM((rows + 8, SEG), F32)],
        compiler_params=pltpu.CompilerParams(
            dimension_semantics=("arbitrary", "arbitrary"), vmem_limit_bytes=VMEM_LIMIT),
        name="in_proj",
    )(x, ng, w_bf, qg, kg, gmean, cw, umeta)


def _attn_kernel(qa_ref, qb_ref, k_ref, v_ref, kmeta_ref, vmeta_ref, gaa_ref, gab_ref,
                 t0_ref, t1_ref, tm_ref, lam_ref, sg_ref, o_ref,
                 kbuf, vbuf, qm_buf, s_buf, own_buf, acc_buf, m_buf, *, tile, nq):
    pr = pl.program_id(2)
    nblk = tile // LANES
    n_far = nq - 1
    qts = (pr, nq - 1 - pr)
    slot0_items = nq // 2 - 1

    @pl.when(pr == 0)
    def _():
        kbuf[0:LANES, :] = kmeta_ref[...]
        kbuf[LANES:, :] = k_ref[0]
        vbuf[0:LANES, 0:V_DIM] = vmeta_ref[...]
        vbuf[LANES:, 0:V_DIM] = v_ref[0]
        vbuf[:, V_DIM:] = jnp.ones((vbuf.shape[0], V_DIM), BF16)

    lane = lax.broadcasted_iota(jnp.int32, (tile, V_DIM), 1)
    for z, q_ref in enumerate((qa_ref, qb_ref)):
        q = q_ref[0]
        zero = jnp.zeros_like(q)
        qm_buf[z, 0] = jnp.where(lane < HEAD_DIM, q, zero)
        qm_buf[z, 1] = jnp.where(lane >= HEAD_DIM, q, zero)
    lane_m = lax.broadcasted_iota(jnp.int32, (1, LANES), 1)
    mrow = jnp.where(lane_m < N_META, 0.0, NEG_INF)

    def scores(qm, kk):
        return lax.dot_general(qm, kk, (((1,), (1,)), ((), ())), preferred_element_type=F32)

    def rowmax(s):
        return jnp.max(s, axis=-1, keepdims=True)

    def lanes(x, n):
        return jnp.concatenate([x] * n, axis=1)


    def far_item(i):
        if isinstance(i, int) and i >= slot0_items:
            return 1, i - pr
        z = jnp.where(i >= pr, 1, 0)
        return z, i - z * pr

    def far_rows(kt):
        return pl.ds(pl.multiple_of(kt * tile, tile), tile)

    def store_scores(slot, i):
        z, kt = far_item(i)
        kk = kbuf[far_rows(kt), :]
        meta_mask = jnp.where(kt == 0, mrow, 0.0)
        for c in range(2):
            s = scores(qm_buf[z, c], kk)
            s_buf[slot, c] = jnp.concatenate([s[:, :LANES] + meta_mask, s[:, LANES:]], axis=1)

    def far_update(slot, i, last=False):
        z, kt = far_item(i)
        vv = vbuf[far_rows(kt), :]
        accs = []
        for c in range(2):
            m_old = m_buf[z, c]
            m_new = jnp.maximum(m_old, rowmax(s_buf[slot, c]))
            p = jnp.exp2((s_buf[slot, c] - lanes(m_new, nblk)).astype(BF16))
            acc = (lanes(jnp.exp2(m_old - m_new), 2) * acc_buf[z, c]
                   + jnp.dot(p, vv, preferred_element_type=F32))
            if last:
                accs.append(acc)
            else:
                m_buf[z, c] = m_new
                acc_buf[z, c] = acc
        return accs


    chunk_blocks = MXU_DIM // LANES
    chunks = [(lo, min(lo + chunk_blocks, nblk + 1), max(lo - 1, 0))
              for lo in range(0, nblk + 1, chunk_blocks)]

    def own_rows(z, lo, hi):
        start = qts[z] * tile + lo * LANES
        return pl.ds(pl.multiple_of(start, LANES), (hi - lo) * LANES)

    def store_own(z):
        for lo, hi, i_min in chunks:
            kk = kbuf[own_rows(z, lo, hi), :]
            for c in range(2):
                own_buf[c, i_min * LANES:, lo * LANES:hi * LANES] = scores(
                    qm_buf[z, c, i_min * LANES:, :], kk)

    def init_own(z):
        first = (qts[z] == 0) if z == 0 else None
        for c in range(2):
            t0, t1, tm = t0_ref[c], t1_ref[c], tm_ref[c]

            def biased(i, j):
                blk = own_buf[c, i * LANES:(i + 1) * LANES, j * LANES:(j + 1) * LANES]
                if j == i + 1:
                    return blk + t0
                if j == 0 and first is not None:
                    return blk + (jnp.where(first, tm, t1) if i == 0 else jnp.where(first, mrow, 0.0))
                return blk + t1 if j == i else blk

            pblk, m_rows = {}, []
            for i in range(nblk):
                row = [biased(i, j) for j in range(i + 2)]
                m_i = jnp.broadcast_to(rowmax(jnp.concatenate(row, axis=1)), (LANES, LANES))
                m_rows.append(m_i)
                for j, blk in enumerate(row):
                    pblk[i, j] = jnp.exp2((blk - m_i).astype(BF16))
            m_buf[z, c] = jnp.concatenate(m_rows, axis=0)

            zeros = jnp.zeros((LANES, LANES), BF16)
            outs = [None] * nblk
            for lo, hi, i_min in chunks:
                p = jnp.concatenate([
                    jnp.concatenate([pblk.get((i, j), zeros) for j in range(lo, hi)], axis=1)
                    for i in range(i_min, nblk)], axis=0)
                o = jnp.dot(p, vbuf[own_rows(z, lo, hi), :], preferred_element_type=F32)
                for i in range(i_min, nblk):
                    piece = o[(i - i_min) * LANES:(i - i_min + 1) * LANES]
                    outs[i] = piece if outs[i] is None else outs[i] + piece
            acc_buf[z, c] = jnp.concatenate(outs, axis=0)

    def finalize(z, accs, ga_ref):
        o = [acc[:, :V_DIM] / acc[:, V_DIM:] for acc in accs]
        a = _rms_rows(o[0] - lam_ref[...] * o[1], sg_ref[...])
        o_ref[0, z] = (a * ga_ref[0].astype(F32)).astype(BF16)

    store_own(0)
    store_scores(0, 0)
    init_own(0)
    store_own(1)
    init_own(1)

    def far_pair(j, carry):
        i = 2 * j
        store_scores(1, i + 1)
        far_update(0, i)
        store_scores(0, i + 2)
        far_update(1, i + 1)
        return carry

    n_pairs = max((n_far - 3) // 2, 0)
    lax.fori_loop(0, n_pairs, far_pair, 0)
    slot0_done = False
    for i in range(2 * n_pairs, n_far):
        if i >= slot0_items and not slot0_done:
            finalize(0, [acc_buf[0, 0], acc_buf[0, 1]], gaa_ref)
            slot0_done = True
        if i + 1 < n_far:
            store_scores((i + 1) % 2, i + 1)
        last = far_update(i % 2, i, last=i == n_far - 1)
    finalize(1, last, gab_ref)


def _attention(q, k, v, kmeta, vmeta, ga, t0, t1, tm, lam, sg, tile):
    b, s, _ = q.shape
    nq = s // tile
    assert nq % 2 == 0
    tile_a = pl.BlockSpec((1, tile, V_DIM), lambda i, h, p: (i, p, h))
    tile_b = pl.BlockSpec((1, tile, V_DIM), lambda i, h, p: (i, nq - 1 - p, h))
    seq_spec = pl.BlockSpec((1, s, V_DIM), lambda i, h, p: (i, 0, h))
    meta_spec = pl.BlockSpec((LANES, V_DIM), lambda i, h, p: (0, h))
    bias_spec = pl.BlockSpec((2, LANES, LANES), lambda i, h, p: (h, 0, 0))
    row_spec = pl.BlockSpec((1, LANES), lambda i, h, p: (0, 0))
    out = jax.ShapeDtypeStruct((b, 2, s // 2, ATTN_WIDTH), BF16)
    return pl.pallas_call(
        functools.partial(_attn_kernel, tile=tile, nq=nq),
        grid=(b, ATTN_HEADS, nq // 2),
        in_specs=[
            tile_a, tile_b, seq_spec, seq_spec, meta_spec, meta_spec, tile_a, tile_b,
            bias_spec, bias_spec, bias_spec, row_spec, row_spec,
        ],
        out_specs=pl.BlockSpec((1, 2, tile, V_DIM), lambda i, h, p: (i, 0, p, h)),
        out_shape=out,
        scratch_shapes=[
            pltpu.VMEM((LANES + s, V_DIM), BF16),
            pltpu.VMEM((LANES + s, 2 * V_DIM), BF16),
            pltpu.VMEM((2, 2, tile, V_DIM), BF16),
            pltpu.VMEM((2, 2, tile, tile), F32),
            pltpu.VMEM((2, tile, tile + LANES), F32),
            pltpu.VMEM((2, 2, tile, 2 * V_DIM), F32),
            pltpu.VMEM((2, 2, tile, LANES), F32),
        ],
        compiler_params=pltpu.CompilerParams(
            dimension_semantics=("arbitrary", "arbitrary", "arbitrary"), vmem_limit_bytes=VMEM_LIMIT),
        name="attention",
    )(q, q, k, v, kmeta, vmeta, ga, ga, t0, t1, tm, lam, sg)


def _merge_kernel(a_ref, c_ref, gm_ref, x_ref, wb_ref, wo_ref, o_ref):
    d = x_ref.shape[1]
    ya = jnp.dot(a_ref[0, 0], wb_ref[0], preferred_element_type=F32)
    yc = jnp.dot(c_ref[...], wb_ref[1], preferred_element_type=F32)
    merged = gm_ref[:, 0:d].astype(F32) * ya + gm_ref[:, d:2 * d].astype(F32) * yc
    o_ref[...] = x_ref[...] + jnp.dot(merged.astype(BF16), wo_ref[...], preferred_element_type=F32)


def _merge(a, c, gm, x, wb, wo, rows):
    n, d = x.shape
    row_spec = lambda width: pl.BlockSpec((rows, width), lambda i: (i, 0))
    half = a.shape[2] // rows
    nq = 2 * half

    def a_index(i):
        j = i % nq
        return i // nq, j // half, jnp.where(j < half, j, nq - 1 - j), 0

    return pl.pallas_call(
        _merge_kernel,
        grid=(n // rows,),
        in_specs=[
            pl.BlockSpec((1, 1, rows, a.shape[3]), a_index),
            row_spec(c.shape[1]), row_spec(gm.shape[1]), row_spec(d),
            pl.BlockSpec(wb.shape, lambda i: (0, 0, 0)),
            pl.BlockSpec(wo.shape, lambda i: (0, 0)),
        ],
        out_specs=row_spec(d),
        out_shape=jax.ShapeDtypeStruct((n, d), F32),
        compiler_params=pltpu.CompilerParams(
            dimension_semantics=("arbitrary",), vmem_limit_bytes=VMEM_LIMIT),
        name="merge",
    )(a, c, gm, x, wb, wo)


def kernel(x, meta_tokens, rel_bias, norm_g, w_in, q_norm_g, k_norm_g, lambda_q1, lambda_k1,
           lambda_q2, lambda_k2, subln_g, conv_w, w_branch, w_out):
    b, s, d = x.shape
    assert norm_g.shape[0] == 1, "single layer only"
    assert meta_tokens.shape[0] == N_META
    tile = min(ATTN_TILE, s // 2)
    rows = min(ROW_TILE, tile)
    assert s % (2 * tile) == 0 and s % rows == 0 and tile % MXU_DIM == 0

    w_bf = w_in[0].astype(BF16)
    groups = SEG // HEAD_DIM
    qg = jnp.tile(q_norm_g[0].astype(F32) * (HEAD_DIM ** -0.5 * LOG2E), groups)[None]
    kg = jnp.tile(k_norm_g[0].astype(F32), groups)[None]
    gidx = np.arange(MXU_DIM) // HEAD_DIM
    gmean = jnp.asarray((gidx[:, None] == gidx[None, :]).astype(np.float32) / HEAD_DIM, BF16)
    lamv = jnp.stack([lambda_q1[0], lambda_k1[0], lambda_q2[0], lambda_k2[0]]).astype(F32)
    rb = rel_bias.astype(F32).reshape(N_BUCKETS, 2 * ATTN_HEADS)
    ng = norm_g.astype(F32)

    kmeta, vmeta, umeta, lam, t0, t1, tm = _prologue(
        rb, meta_tokens.astype(F32), ng, w_bf, kg, gmean, lamv)
    q, k, v, ga, c, gm = _in_proj(x, ng, w_bf, qg, kg, gmean, conv_w[0].astype(F32), umeta, rows)
    sg = subln_g.astype(F32) * (1.0 - LAM_INIT)
    a = _attention(q, k, v, kmeta, vmeta, ga, t0, t1, tm, lam, sg, tile)
    out = _merge(a, c.reshape(b * s, -1), gm.reshape(b * s, -1),
                 x.reshape(b * s, d), w_branch[0].astype(BF16), w_out[0].astype(BF16), tile)
    return out.reshape(b, s, d)
```

```python
import functools
import math

import numpy as np
import jax
import jax.numpy as jnp
from jax import lax
from jax.experimental import pallas as pl
from jax.experimental.pallas import tpu as pltpu

N_META = 16
ATTN_HEADS = 4
HEAD_DIM = 64
V_DIM = 2 * HEAD_DIM
ATTN_WIDTH = ATTN_HEADS * V_DIM
CONV_WIDTH = 512
CONV_K = 3
N_BRANCH = 2
N_BUCKETS = 32
MAX_DISTANCE = 128
EPS = 1e-6
NEG_INF = -1e30
LAM_INIT = 0.8 - 0.6 * math.exp(-0.3 * 0)
LOG2E = math.log2(math.e)

COL_Q, COL_K, COL_V, COL_GA, COL_CB, COL_CC, COL_CH, COL_GC, COL_GM = (
    0, 512, 1024, 1536, 2048, 2560, 3072, 3584, 4096)
SEG = 512

LANES = 128
MXU_DIM = 256
VMEM_LIMIT = 56 * 1024 * 1024

ATTN_TILE = 1024
ROW_TILE = 1024

F32 = jnp.float32
BF16 = jnp.bfloat16


def _bucket_thresholds():
    max_exact = N_BUCKETS // 2
    n = np.arange(0, 2 * MAX_DISTANCE)
    nf = np.maximum(n, max_exact).astype(np.float32)
    large = max_exact + (np.log(nf / np.float32(max_exact)) / np.float32(math.log(MAX_DISTANCE / max_exact))
                         * np.float32(N_BUCKETS - max_exact)).astype(np.int32)
    bucket = np.where(n < max_exact, n, np.minimum(large, N_BUCKETS - 1))
    return [int(np.argmax(bucket >= b)) for b in range(N_BUCKETS)]


_THR = _bucket_thresholds()
assert _THR[N_BUCKETS - 1] <= LANES


def _sigmoid(x):
    return 0.5 * jnp.tanh(0.5 * x) + 0.5


def _rms_rows(x, g):
    return x * lax.rsqrt(jnp.mean(x * x, axis=-1, keepdims=True) + EPS) * g


def _group_rms(x, gmean, g):
    sq = (x * x).astype(BF16)
    ms = jnp.concatenate([
        jnp.dot(sq[:, lo:lo + MXU_DIM], gmean, preferred_element_type=F32)
        for lo in range(0, x.shape[1], MXU_DIM)], axis=1)
    return x * lax.rsqrt(ms + EPS) * g


def _prologue_kernel(rb_ref, meta_ref, ng_ref, wk_ref, wv_ref, wcc_ref, wch_ref, kg_ref, gmean_ref,
                     lamv_ref, kmeta_ref, vmeta_ref, umeta_ref, lam_ref, t0_ref, t1_ref, tm_ref):
    xn = _rms_rows(meta_ref[...], ng_ref[...]).astype(BF16)
    k = jnp.dot(xn, wk_ref[...], preferred_element_type=F32)
    k = _group_rms(k, gmean_ref[...], kg_ref[...])
    v = jnp.dot(xn, wv_ref[...], preferred_element_type=F32)
    cc = jnp.dot(xn, wcc_ref[...], preferred_element_type=F32)
    ch = jnp.dot(xn, wch_ref[...], preferred_element_type=F32)
    kmeta_ref[...] = jnp.zeros(kmeta_ref.shape, BF16)
    vmeta_ref[...] = jnp.zeros(vmeta_ref.shape, BF16)
    kmeta_ref[0:N_META, :] = k.astype(BF16)
    vmeta_ref[0:N_META, :] = v.astype(BF16)
    umeta_ref[...] = cc * ch

    lv = lamv_ref[...]
    s1 = jnp.sum(lv[0:1] * lv[1:2], axis=-1, keepdims=True)
    s2 = jnp.sum(lv[2:3] * lv[3:4], axis=-1, keepdims=True)
    lam_ref[...] = jnp.broadcast_to(jnp.exp(s1) - jnp.exp(s2) + LAM_INIT, lam_ref.shape)

    row = lax.broadcasted_iota(jnp.int32, (LANES, LANES), 0)
    col = lax.broadcasted_iota(jnp.int32, (LANES, LANES), 1)
    d_diag = row - col
    d_sub = d_diag + LANES
    d_meta = d_diag + N_META

    def toeplitz(dist, hc):
        far = rb_ref[N_BUCKETS - 1, hc]
        t = jnp.full((LANES, LANES), (rb_ref[0, hc] - far) * LOG2E, F32)
        for b in range(1, N_BUCKETS):
            t = jnp.where(dist >= _THR[b], (rb_ref[b, hc] - far) * LOG2E, t)
        return t

    for hc in range(2 * ATTN_HEADS):
        t0_ref[hc] = jnp.where(d_diag >= 0, toeplitz(d_diag, hc), NEG_INF)
        t1_ref[hc] = toeplitz(d_sub, hc)
        tm_ref[hc] = jnp.where(col < N_META, toeplitz(d_meta, hc), NEG_INF)


def _prologue(rb, meta, ng, w_bf, kg, gmean, lamv):
    d = meta.shape[1]
    wspec = lambda j: pl.BlockSpec((d, SEG), lambda i, j=j: (0, j))
    full = lambda shape: pl.BlockSpec(shape, lambda i: (0,) * len(shape))
    blocks = jax.ShapeDtypeStruct((2 * ATTN_HEADS, LANES, LANES), F32)
    out_shape = (
        jax.ShapeDtypeStruct((LANES, SEG), BF16),
        jax.ShapeDtypeStruct((LANES, SEG), BF16),
        jax.ShapeDtypeStruct((N_META, SEG), F32),
        jax.ShapeDtypeStruct((1, LANES), F32),
        blocks,
        blocks,
        blocks,
    )
    return pl.pallas_call(
        _prologue_kernel,
        grid=(1,),
        in_specs=[
            pl.BlockSpec(memory_space=pltpu.SMEM),
            full(meta.shape), full(ng.shape),
            wspec(COL_K // SEG), wspec(COL_V // SEG), wspec(COL_CC // SEG), wspec(COL_CH // SEG),
            full(kg.shape), full(gmean.shape), full(lamv.shape),
        ],
        out_specs=tuple(full(s.shape) for s in out_shape),
        out_shape=out_shape,
        compiler_params=pltpu.CompilerParams(vmem_limit_bytes=VMEM_LIMIT),
        name="prologue",
    )(rb, meta, ng, w_bf, w_bf, w_bf, w_bf, kg, gmean, lamv)


def _in_proj_kernel(x_ref, ng_ref, w_ref, qg_ref, kg_ref, gmean_ref, cw_ref, umeta_ref,
                    q_ref, k_ref, v_ref, ga_ref, c_ref, gm_ref, ubuf, *, rows):
    t = pl.program_id(1)

    @pl.when(t == 0)
    def _():
        ubuf[0:8, :] = umeta_ref[N_META - 8:N_META, :]

    @pl.when(t > 0)
    def _():
        ubuf[0:8, :] = ubuf[rows:rows + 8, :]

    xn = _rms_rows(x_ref[0], ng_ref[...]).astype(BF16)

    def proj(lo, n=SEG):
        return jnp.dot(xn, w_ref[:, lo:lo + n], preferred_element_type=F32)

    for j in range(gm_ref.shape[2] // SEG):
        gm_ref[0, :, j * SEG:(j + 1) * SEG] = _sigmoid(proj(COL_GM + j * SEG)).astype(BF16)
    ga = proj(COL_GA)
    ga_ref[0] = (ga * _sigmoid(ga)).astype(BF16)

    u = proj(COL_CC) * proj(COL_CH)
    ubuf[8:8 + rows, :] = u
    cw = cw_ref[...]
    conv = cw[0:1] * ubuf[6:6 + rows, :] + cw[1:2] * ubuf[7:7 + rows, :] + cw[2:3] * u
    gc = proj(COL_GC)
    c_ref[0] = (proj(COL_CB) * conv * (gc * _sigmoid(gc))).astype(BF16)

    gmean = gmean_ref[...]
    q_ref[0] = _group_rms(proj(COL_Q), gmean, qg_ref[...]).astype(BF16)
    k_ref[0] = _group_rms(proj(COL_K), gmean, kg_ref[...]).astype(BF16)
    v_ref[0] = proj(COL_V).astype(BF16)


def _in_proj(x, ng, w_bf, qg, kg, gmean, cw, umeta, rows):
    b, s, d = x.shape
    ncols = w_bf.shape[1]
    gm_cols = ncols - COL_GM
    const = lambda shape: pl.BlockSpec(shape, lambda i, j: (0,) * len(shape))
    seg_out = pl.BlockSpec((1, rows, SEG), lambda i, j: (i, j, 0))
    out_shape = tuple(jax.ShapeDtypeStruct((b, s, SEG), BF16) for _ in range(5)) + (
        jax.ShapeDtypeStruct((b, s, gm_cols), BF16),)
    return pl.pallas_call(
        functools.partial(_in_proj_kernel, rows=rows),
        grid=(b, s // rows),
        in_specs=[
            pl.BlockSpec((1, rows, d), lambda i, j: (i, j, 0)),
            const(ng.shape),
            pl.BlockSpec(w_bf.shape, lambda i, j: (0, 0), pipeline_mode=pl.Buffered(1)),
            const(qg.shape), const(kg.shape), const(gmean.shape), const(cw.shape), const(umeta.shape),
        ],
        out_specs=(seg_out,) * 5 + (pl.BlockSpec((1, rows, gm_cols), lambda i, j: (i, j, 0)),),
        out_shape=out_shape,
        scratch_shapes=[pltpu.VMEM((rows + 8, SEG), F32)],
        compiler_params=pltpu.CompilerParams(
            dimension_semantics=("arbitrary", "arbitrary"), vmem_limit_bytes=VMEM_LIMIT),
        name="in_proj",
    )(x, ng, w_bf, qg, kg, gmean, cw, umeta)


def _attn_kernel(qa_ref, qb_ref, k_ref, v_ref, kmeta_ref, vmeta_ref, gaa_ref, gab_ref,
                 t0_ref, t1_ref, tm_ref, lam_ref, sg_ref, o_ref,
                 kbuf, vbuf, qm_buf, s_buf, own_buf, acc_buf, m_buf, *, tile, nq):
    pr = pl.program_id(2)
    nblk = tile // LANES
    n_far = nq - 1
    qts = (pr, nq - 1 - pr)
    slot0_items = nq // 2 - 1

    @pl.when(pr == 0)
    def _():
        kbuf[0:LANES, :] = kmeta_ref[...]
        kbuf[LANES:, :] = k_ref[0]
        vbuf[0:LANES, 0:V_DIM] = vmeta_ref[...]
        vbuf[LANES:, 0:V_DIM] = v_ref[0]
        vbuf[:, V_DIM:] = jnp.ones((vbuf.shape[0], V_DIM), BF16)

    lane = lax.broadcasted_iota(jnp.int32, (tile, V_DIM), 1)
    for z, q_ref in enumerate((qa_ref, qb_ref)):
        q = q_ref[0]
        zero = jnp.zeros_like(q)
        qm_buf[z, 0] = jnp.where(lane < HEAD_DIM, q, zero)
        qm_buf[z, 1] = jnp.where(lane >= HEAD_DIM, q, zero)
    lane_m = lax.broadcasted_iota(jnp.int32, (1, LANES), 1)
    mrow = jnp.where(lane_m < N_META, 0.0, NEG_INF)

    def scores(qm, kk):
        return lax.dot_general(qm, kk, (((1,), (1,)), ((), ())), preferred_element_type=F32)

    def rowmax(s):
        return jnp.max(s, axis=-1, keepdims=True)

    def lanes(x, n):
        return jnp.concatenate([x] * n, axis=1)


    def far_item(i):
        if isinstance(i, int) and i >= slot0_items:
            return 1, i - pr
        z = jnp.where(i >= pr, 1, 0)
        return z, i - z * pr

    def far_rows(kt):
        return pl.ds(pl.multiple_of(kt * tile, tile), tile)

    def store_scores(slot, i):
        z, kt = far_item(i)
        kk = kbuf[far_rows(kt), :]
        meta_mask = jnp.where(kt == 0, mrow, 0.0)
        for c in range(2):
            s = scores(qm_buf[z, c], kk)
            s_buf[slot, c] = jnp.concatenate([s[:, :LANES] + meta_mask, s[:, LANES:]], axis=1)

    def far_update(slot, i, last=False):
        z, kt = far_item(i)
        vv = vbuf[far_rows(kt), :]
        accs = []
        for c in range(2):
            m_old = m_buf[z, c]
            m_new = jnp.maximum(m_old, rowmax(s_buf[slot, c]))
            p = jnp.exp2((s_buf[slot, c] - lanes(m_new, nblk)).astype(BF16))
            acc = (lanes(jnp.exp2(m_old - m_new), 2) * acc_buf[z, c]
                   + jnp.dot(p, vv, preferred_element_type=F32))
            if last:
                accs.append(acc)
            else:
                m_buf[z, c] = m_new
                acc_buf[z, c] = acc
        return accs


    chunk_blocks = MXU_DIM // LANES
    chunks = [(lo, min(lo + chunk_blocks, nblk + 1), max(lo - 1, 0))
              for lo in range(0, nblk + 1, chunk_blocks)]

    def own_rows(z, lo, hi):
        start = qts[z] * tile + lo * LANES
        return pl.ds(pl.multiple_of(start, LANES), (hi - lo) * LANES)

    def store_own(z):
        for lo, hi, i_min in chunks:
            kk = kbuf[own_rows(z, lo, hi), :]
            for c in range(2):
                own_buf[c, i_min * LANES:, lo * LANES:hi * LANES] = scores(
                    qm_buf[z, c, i_min * LANES:, :], kk)

    def init_own(z):
        first = (qts[z] == 0) if z == 0 else None
        for c in range(2):
            t0, t1, tm = t0_ref[c], t1_ref[c], tm_ref[c]

            def biased(i, j):
                blk = own_buf[c, i * LANES:(i + 1) * LANES, j * LANES:(j + 1) * LANES]
                if j == i + 1:
                    return blk + t0
                if j == 0 and first is not None:
                    return blk + (jnp.where(first, tm, t1) if i == 0 else jnp.where(first, mrow, 0.0))
                return blk + t1 if j == i else blk

            pblk, m_rows = {}, []
            for i in range(nblk):
                row = [biased(i, j) for j in range(i + 2)]
                m_i = jnp.broadcast_to(rowmax(jnp.concatenate(row, axis=1)), (LANES, LANES))
                m_rows.append(m_i)
                for j, blk in enumerate(row):
                    pblk[i, j] = jnp.exp2((blk - m_i).astype(BF16))
            m_buf[z, c] = jnp.concatenate(m_rows, axis=0)

            zeros = jnp.zeros((LANES, LANES), BF16)
            outs = [None] * nblk
            for lo, hi, i_min in chunks:
                p = jnp.concatenate([
                    jnp.concatenate([pblk.get((i, j), zeros) for j in range(lo, hi)], axis=1)
                    for i in range(i_min, nblk)], axis=0)
                o = jnp.dot(p, vbuf[own_rows(z, lo, hi), :], preferred_element_type=F32)
                for i in range(i_min, nblk):
                    piece = o[(i - i_min) * LANES:(i - i_min + 1) * LANES]
                    outs[i] = piece if outs[i] is None else outs[i] + piece
            acc_buf[z, c] = jnp.concatenate(outs, axis=0)

    def finalize(z, accs, ga_ref):
        o = [acc[:, :V_DIM] / acc[:, V_DIM:] for acc in accs]
        a = _rms_rows(o[0] - lam_ref[...] * o[1], sg_ref[...])
        o_ref[0, z] = (a * ga_ref[0].astype(F32)).astype(BF16)

    store_own(0)
    store_scores(0, 0)
    init_own(0)
    store_own(1)
    init_own(1)

    def far_pair(j, carry):
        i = 2 * j
        store_scores(1, i + 1)
        far_update(0, i)
        store_scores(0, i + 2)
        far_update(1, i + 1)
        return carry

    n_pairs = max((n_far - 3) // 2, 0)
    lax.fori_loop(0, n_pairs, far_pair, 0)
    slot0_done = False
    for i in range(2 * n_pairs, n_far):
        if i >= slot0_items and not slot0_done:
            finalize(0, [acc_buf[0, 0], acc_buf[0, 1]], gaa_ref)
            slot0_done = True
        if i + 1 < n_far:
            store_scores((i + 1) % 2, i + 1)
        last = far_update(i % 2, i, last=i == n_far - 1)
    finalize(1, last, gab_ref)


def _attention(q, k, v, kmeta, vmeta, ga, t0, t1, tm, lam, sg, tile):
    b, s, _ = q.shape
    nq = s // tile
    assert nq % 2 == 0
    tile_a = pl.BlockSpec((1, tile, V_DIM), lambda i, h, p: (i, p, h))
    tile_b = pl.BlockSpec((1, tile, V_DIM), lambda i, h, p: (i, nq - 1 - p, h))
    seq_spec = pl.BlockSpec((1, s, V_DIM), lambda i, h, p: (i, 0, h))
    meta_spec = pl.BlockSpec((LANES, V_DIM), lambda i, h, p: (0, h))
    bias_spec = pl.BlockSpec((2, LANES, LANES), lambda i, h, p: (h, 0, 0))
    row_spec = pl.BlockSpec((1, LANES), lambda i, h, p: (0, 0))
    out = jax.ShapeDtypeStruct((b, 2, s // 2, ATTN_WIDTH), BF16)
    return pl.pallas_call(
        functools.partial(_attn_kernel, tile=tile, nq=nq),
        grid=(b, ATTN_HEADS, nq // 2),
        in_specs=[
            tile_a, tile_b, seq_spec, seq_spec, meta_spec, meta_spec, tile_a, tile_b,
            bias_spec, bias_spec, bias_spec, row_spec, row_spec,
        ],
        out_specs=pl.BlockSpec((1, 2, tile, V_DIM), lambda i, h, p: (i, 0, p, h)),
        out_shape=out,
        scratch_shapes=[
            pltpu.VMEM((LANES + s, V_DIM), BF16),
            pltpu.VMEM((LANES + s, 2 * V_DIM), BF16),
            pltpu.VMEM((2, 2, tile, V_DIM), BF16),
            pltpu.VMEM((2, 2, tile, tile), F32),
            pltpu.VMEM((2, tile, tile + LANES), F32),
            pltpu.VMEM((2, 2, tile, 2 * V_DIM), F32),
            pltpu.VMEM((2, 2, tile, LANES), F32),
        ],
        compiler_params=pltpu.CompilerParams(
            dimension_semantics=("arbitrary", "arbitrary", "arbitrary"), vmem_limit_bytes=VMEM_LIMIT),
        name="attention",
    )(q, q, k, v, kmeta, vmeta, ga, ga, t0, t1, tm, lam, sg)


def _merge_kernel(a_ref, c_ref, gm_ref, x_ref, wb_ref, wo_ref, o_ref):
    d = x_ref.shape[1]
    ya = jnp.dot(a_ref[0, 0], wb_ref[0], preferred_element_type=F32)
    yc = jnp.dot(c_ref[...], wb_ref[1], preferred_element_type=F32)
    merged = gm_ref[:, 0:d].astype(F32) * ya + gm_ref[:, d:2 * d].astype(F32) * yc
    o_ref[...] = x_ref[...] + jnp.dot(merged.astype(BF16), wo_ref[...], preferred_element_type=F32)


def _merge(a, c, gm, x, wb, wo, rows):
    n, d = x.shape
    row_spec = lambda width: pl.BlockSpec((rows, width), lambda i: (i, 0))
    half = a.shape[2] // rows
    nq = 2 * half

    def a_index(i):
        j = i % nq
        return i // nq, j // half, jnp.where(j < half, j, nq - 1 - j), 0

    return pl.pallas_call(
        _merge_kernel,
        grid=(n // rows,),
        in_specs=[
            pl.BlockSpec((1, 1, rows, a.shape[3]), a_index),
            row_spec(c.shape[1]), row_spec(gm.shape[1]), row_spec(d),
            pl.BlockSpec(wb.shape, lambda i: (0, 0, 0)),
            pl.BlockSpec(wo.shape, lambda i: (0, 0)),
        ],
        out_specs=row_spec(d),
        out_shape=jax.ShapeDtypeStruct((n, d), F32),
        compiler_params=pltpu.CompilerParams(
            dimension_semantics=("arbitrary",), vmem_limit_bytes=VMEM_LIMIT),
        name="merge",
    )(a, c, gm, x, wb, wo)


def kernel(x, meta_tokens, rel_bias, norm_g, w_in, q_norm_g, k_norm_g, lambda_q1, lambda_k1,
           lambda_q2, lambda_k2, subln_g, conv_w, w_branch, w_out):
    b, s, d = x.shape
    assert norm_g.shape[0] == 1, "single layer only"
    assert meta_tokens.shape[0] == N_META
    tile = min(ATTN_TILE, s // 2)
    rows = min(ROW_TILE, tile)
    assert s % (2 * tile) == 0 and s % rows == 0 and tile % MXU_DIM == 0

    w_bf = w_in[0].astype(BF16)
    groups = SEG // HEAD_DIM
    qg = jnp.tile(q_norm_g[0].astype(F32) * (HEAD_DIM ** -0.5 * LOG2E), groups)[None]
    kg = jnp.tile(k_norm_g[0].astype(F32), groups)[None]
    gidx = np.arange(MXU_DIM) // HEAD_DIM
    gmean = jnp.asarray((gidx[:, None] == gidx[None, :]).astype(np.float32) / HEAD_DIM, BF16)
    lamv = jnp.stack([lambda_q1[0], lambda_k1[0], lambda_q2[0], lambda_k2[0]]).astype(F32)
    rb = rel_bias.astype(F32).reshape(N_BUCKETS, 2 * ATTN_HEADS)
    ng = norm_g.astype(F32)

    kmeta, vmeta, umeta, lam, t0, t1, tm = _prologue(
        rb, meta_tokens.astype(F32), ng, w_bf, kg, gmean, lamv)
    q, k, v, ga, c, gm = _in_proj(x, ng, w_bf, qg, kg, gmean, conv_w[0].astype(F32), umeta, rows)
    sg = subln_g.astype(F32) * (1.0 - LAM_INIT)
    a = _attention(q, k, v, kmeta, vmeta, ga, t0, t1, tm, lam, sg, tile)
    out = _merge(a, c.reshape(b * s, -1), gm.reshape(b * s, -1),
                 x.reshape(b * s, d), w_branch[0].astype(BF16), w_out[0].astype(BF16), tile)
    return out.reshape(b, s, d)
```

```python
import functools
import math

import numpy as np
import jax
import jax.numpy as jnp
from jax import lax
from jax.experimental import pallas as pl
from jax.experimental.pallas import tpu as pltpu

N_META = 16
ATTN_HEADS = 4
HEAD_DIM = 64
V_DIM = 2 * HEAD_DIM
ATTN_WIDTH = ATTN_HEADS * V_DIM
CONV_WIDTH = 512
CONV_K = 3
N_BRANCH = 2
N_BUCKETS = 32
MAX_DISTANCE = 128
EPS = 1e-6
NEG_INF = -1e30
LAM_INIT = 0.8 - 0.6 * math.exp(-0.3 * 0)
LOG2E = math.log2(math.e)

COL_Q, COL_K, COL_V, COL_GA, COL_CB, COL_CC, COL_CH, COL_GC, COL_GM = (
    0, 512, 1024, 1536, 2048, 2560, 3072, 3584, 4096)
SEG = 512

LANES = 128
MXU_DIM = 256
VMEM_LIMIT = 56 * 1024 * 1024

ATTN_TILE = 1024
ROW_TILE = 1024

F32 = jnp.float32
BF16 = jnp.bfloat16


def _bucket_thresholds():
    max_exact = N_BUCKETS // 2
    n = np.arange(0, 2 * MAX_DISTANCE)
    nf = np.maximum(n, max_exact).astype(np.float32)
    large = max_exact + (np.log(nf / np.float32(max_exact)) / np.float32(math.log(MAX_DISTANCE / max_exact))
                         * np.float32(N_BUCKETS - max_exact)).astype(np.int32)
    bucket = np.where(n < max_exact, n, np.minimum(large, N_BUCKETS - 1))
    return [int(np.argmax(bucket >= b)) for b in range(N_BUCKETS)]


_THR = _bucket_thresholds()
assert _THR[N_BUCKETS - 1] <= LANES


def _sigmoid(x):
    return 0.5 * jnp.tanh(0.5 * x) + 0.5


def _rms_rows(x, g):
    return x * lax.rsqrt(jnp.mean(x * x, axis=-1, keepdims=True) + EPS) * g


def _group_rms(x, gmean, g):
    sq = (x * x).astype(BF16)
    ms = jnp.concatenate([
        jnp.dot(sq[:, lo:lo + MXU_DIM], gmean, preferred_element_type=F32)
        for lo in range(0, x.shape[1], MXU_DIM)], axis=1)
    return x * lax.rsqrt(ms + EPS) * g


def _prologue_kernel(rb_ref, meta_ref, ng_ref, wk_ref, wv_ref, wcc_ref, wch_ref, kg_ref, gmean_ref,
                     lamv_ref, kmeta_ref, vmeta_ref, umeta_ref, lam_ref, t0_ref, t1_ref, tm_ref):
    xn = _rms_rows(meta_ref[...], ng_ref[...]).astype(BF16)
    k = jnp.dot(xn, wk_ref[...], preferred_element_type=F32)
    k = _group_rms(k, gmean_ref[...], kg_ref[...])
    v = jnp.dot(xn, wv_ref[...], preferred_element_type=F32)
    cc = jnp.dot(xn, wcc_ref[...], preferred_element_type=F32)
    ch = jnp.dot(xn, wch_ref[...], preferred_element_type=F32)
    kmeta_ref[...] = jnp.zeros(kmeta_ref.shape, BF16)
    vmeta_ref[...] = jnp.zeros(vmeta_ref.shape, BF16)
    kmeta_ref[0:N_META, :] = k.astype(BF16)
    vmeta_ref[0:N_META, :] = v.astype(BF16)
    umeta_ref[...] = cc * ch

    lv = lamv_ref[...]
    s1 = jnp.sum(lv[0:1] * lv[1:2], axis=-1, keepdims=True)
    s2 = jnp.sum(lv[2:3] * lv[3:4], axis=-1, keepdims=True)
    lam_ref[...] = jnp.broadcast_to(jnp.exp(s1) - jnp.exp(s2) + LAM_INIT, lam_ref.shape)

    row = lax.broadcasted_iota(jnp.int32, (LANES, LANES), 0)
    col = lax.broadcasted_iota(jnp.int32, (LANES, LANES), 1)
    d_diag = row - col
    d_sub = d_diag + LANES
    d_meta = d_diag + N_META

    def toeplitz(dist, hc):
        far = rb_ref[N_BUCKETS - 1, hc]
        t = jnp.full((LANES, LANES), (rb_ref[0, hc] - far) * LOG2E, F32)
        for b in range(1, N_BUCKETS):
            t = jnp.where(dist >= _THR[b], (rb_ref[b, hc] - far) * LOG2E, t)
        return t

    for hc in range(2 * ATTN_HEADS):
        t0_ref[hc] = jnp.where(d_diag >= 0, toeplitz(d_diag, hc), NEG_INF)
        t1_ref[hc] = toeplitz(d_sub, hc)
        tm_ref[hc] = jnp.where(col < N_META, toeplitz(d_meta, hc), NEG_INF)


def _prologue(rb, meta, ng, w_bf, kg, gmean, lamv):
    d = meta.shape[1]
    wspec = lambda j: pl.BlockSpec((d, SEG), lambda i, j=j: (0, j))
    full = lambda shape: pl.BlockSpec(shape, lambda i: (0,) * len(shape))
    blocks = jax.ShapeDtypeStruct((2 * ATTN_HEADS, LANES, LANES), F32)
    out_shape = (
        jax.ShapeDtypeStruct((LANES, SEG), BF16),
        jax.ShapeDtypeStruct((LANES, SEG), BF16),
        jax.ShapeDtypeStruct((N_META, SEG), F32),
        jax.ShapeDtypeStruct((1, LANES), F32),
        blocks,
        blocks,
        blocks,
    )
    return pl.pallas_call(
        _prologue_kernel,
        grid=(1,),
        in_specs=[
            pl.BlockSpec(memory_space=pltpu.SMEM),
            full(meta.shape), full(ng.shape),
            wspec(COL_K // SEG), wspec(COL_V // SEG), wspec(COL_CC // SEG), wspec(COL_CH // SEG),
            full(kg.shape), full(gmean.shape), full(lamv.shape),
        ],
        out_specs=tuple(full(s.shape) for s in out_shape),
        out_shape=out_shape,
        compiler_params=pltpu.CompilerParams(vmem_limit_bytes=VMEM_LIMIT),
        name="prologue",
    )(rb, meta, ng, w_bf, w_bf, w_bf, w_bf, kg, gmean, lamv)


def _in_proj_kernel(x_ref, ng_ref, w_ref, qg_ref, kg_ref, gmean_ref, cw_ref, umeta_ref,
                    q_ref, k_ref, v_ref, ga_ref, c_ref, gm_ref, ubuf, *, rows):
    t = pl.program_id(1)

    @pl.when(t == 0)
    def _():
        ubuf[0:8, :] = umeta_ref[N_META - 8:N_META, :]

    @pl.when(t > 0)
    def _():
        ubuf[0:8, :] = ubuf[rows:rows + 8, :]

    xn = _rms_rows(x_ref[0], ng_ref[...]).astype(BF16)

    def proj(lo, n=SEG):
        return jnp.dot(xn, w_ref[:, lo:lo + n], preferred_element_type=F32)

    for j in range(gm_ref.shape[2] // SEG):
        gm_ref[0, :, j * SEG:(j + 1) * SEG] = _sigmoid(proj(COL_GM + j * SEG)).astype(BF16)
    ga = proj(COL_GA)
    ga_ref[0] = (ga * _sigmoid(ga)).astype(BF16)

    u = proj(COL_CC) * proj(COL_CH)
    ubuf[8:8 + rows, :] = u
    cw = cw_ref[...]
    conv = cw[0:1] * ubuf[6:6 + rows, :] + cw[1:2] * ubuf[7:7 + rows, :] + cw[2:3] * u
    gc = proj(COL_GC)
    c_ref[0] = (proj(COL_CB) * conv * (gc * _sigmoid(gc))).astype(BF16)

    gmean = gmean_ref[...]
    q_ref[0] = _group_rms(proj(COL_Q), gmean, qg_ref[...]).astype(BF16)
    k_ref[0] = _group_rms(proj(COL_K), gmean, kg_ref[...]).astype(BF16)
    v_ref[0] = proj(COL_V).astype(BF16)


def _in_proj(x, ng, w_bf, qg, kg, gmean, cw, umeta, rows):
    b, s, d = x.shape
    ncols = w_bf.shape[1]
    gm_cols = ncols - COL_GM
    const = lambda shape: pl.BlockSpec(shape, lambda i, j: (0,) * len(shape))
    seg_out = pl.BlockSpec((1, rows, SEG), lambda i, j: (i, j, 0))
    out_shape = tuple(jax.ShapeDtypeStruct((b, s, SEG), BF16) for _ in range(5)) + (
        jax.ShapeDtypeStruct((b, s, gm_cols), BF16),)
    return pl.pallas_call(
        functools.partial(_in_proj_kernel, rows=rows),
        grid=(b, s // rows),
        in_specs=[
            pl.BlockSpec((1, rows, d), lambda i, j: (i, j, 0)),
            const(ng.shape),
            pl.BlockSpec(w_bf.shape, lambda i, j: (0, 0), pipeline_mode=pl.Buffered(1)),
            const(qg.shape), const(kg.shape), const(gmean.shape), const(cw.shape), const(umeta.shape),
        ],
        out_specs=(seg_out,) * 5 + (pl.BlockSpec((1, rows, gm_cols), lambda i, j: (i, j, 0)),),
        out_shape=out_shape,
        scratch_shapes=[pltpu.VMEM((rows + 8, SEG), F32)],
        compiler_params=pltpu.CompilerParams(
            dimension_semantics=("arbitrary", "arbitrary"), vmem_limit_bytes=VMEM_LIMIT),
        name="in_proj",
    )(x, ng, w_bf, qg, kg, gmean, cw, umeta)


def _attn_kernel(qa_ref, qb_ref, k_ref, v_ref, kmeta_ref, vmeta_ref, gaa_ref, gab_ref,
                 t0_ref, t1_ref, tm_ref, lam_ref, sg_ref, o_ref,
                 kbuf, vbuf, qm_buf, s_buf, own_buf, acc_buf, m_buf, *, tile, nq):
    pr = pl.program_id(2)
    nblk = tile // LANES
    n_far = nq - 1
    qts = (pr, nq - 1 - pr)
    slot0_items = nq // 2 - 1

    @pl.when(pr == 0)
    def _():
        kbuf[0:LANES, :] = kmeta_ref[...]
        kbuf[LANES:, :] = k_ref[0]
        vbuf[0:LANES, 0:V_DIM] = vmeta_ref[...]
        vbuf[LANES:, 0:V_DIM] = v_ref[0]
        vbuf[:, V_DIM:] = jnp.ones((vbuf.shape[0], V_DIM), BF16)

    lane = lax.broadcasted_iota(jnp.int32, (tile, V_DIM), 1)
    for z, q_ref in enumerate((qa_ref, qb_ref)):
        q = q_ref[0]
        zero = jnp.zeros_like(q)
        qm_buf[z, 0] = jnp.where(lane < HEAD_DIM, q, zero)
        qm_buf[z, 1] = jnp.where(lane >= HEAD_DIM, q, zero)
    lane_m = lax.broadcasted_iota(jnp.int32, (1, LANES), 1)
    mrow = jnp.where(lane_m < N_META, 0.0, NEG_INF)

    def scores(qm, kk):
        return lax.dot_general(qm, kk, (((1,), (1,)), ((), ())), preferred_element_type=F32)

    def rowmax(s):
        return jnp.max(s, axis=-1, keepdims=True)

    def lanes(x, n):
        return jnp.concatenate([x] * n, axis=1)


    def far_item(i):
        if isinstance(i, int) and i >= slot0_items:
            return 1, i - pr
        z = jnp.where(i >= pr, 1, 0)
        return z, i - z * pr

    def far_rows(kt):
        return pl.ds(pl.multiple_of(kt * tile, tile), tile)

    def store_scores(slot, i):
        z, kt = far_item(i)
        kk = kbuf[far_rows(kt), :]
        meta_mask = jnp.where(kt == 0, mrow, 0.0)
        for c in range(2):
            s = scores(qm_buf[z, c], kk)
            s_buf[slot, c] = jnp.concatenate([s[:, :LANES] + meta_mask, s[:, LANES:]], axis=1)

    def far_update(slot, i, last=False):
        z, kt = far_item(i)
        vv = vbuf[far_rows(kt), :]
        accs = []
        for c in range(2):
            m_old = m_buf[z, c]
            m_new = jnp.maximum(m_old, rowmax(s_buf[slot, c]))
            p = jnp.exp2((s_buf[slot, c] - lanes(m_new, nblk)).astype(BF16))
            acc = (lanes(jnp.exp2(m_old - m_new), 2) * acc_buf[z, c]
                   + jnp.dot(p, vv, preferred_element_type=F32))
            if last:
                accs.append(acc)
            else:
                m_buf[z, c] = m_new
                acc_buf[z, c] = acc
        return accs


    chunk_blocks = MXU_DIM // LANES
    chunks = [(lo, min(lo + chunk_blocks, nblk + 1), max(lo - 1, 0))
              for lo in range(0, nblk + 1, chunk_blocks)]

    def own_rows(z, lo, hi):
        start = qts[z] * tile + lo * LANES
        return pl.ds(pl.multiple_of(start, LANES), (hi - lo) * LANES)

    def store_own(z):
        for lo, hi, i_min in chunks:
            kk = kbuf[own_rows(z, lo, hi), :]
            for c in range(2):
                own_buf[c, i_min * LANES:, lo * LANES:hi * LANES] = scores(
                    qm_buf[z, c, i_min * LANES:, :], kk)

    def init_own(z):
        first = (qts[z] == 0) if z == 0 else None
        for c in range(2):
            t0, t1, tm = t0_ref[c], t1_ref[c], tm_ref[c]

            def biased(i, j):
                blk = own_buf[c, i * LANES:(i + 1) * LANES, j * LANES:(j + 1) * LANES]
                if j == i + 1:
                    return blk + t0
                if j == 0 and first is not None:
                    return blk + (jnp.where(first, tm, t1) if i == 0 else jnp.where(first, mrow, 0.0))
                return blk + t1 if j == i else blk

            pblk, m_rows = {}, []
            for i in range(nblk):
                row = [biased(i, j) for j in range(i + 2)]
                m_i = jnp.broadcast_to(rowmax(jnp.concatenate(row, axis=1)), (LANES, LANES))
                m_rows.append(m_i)
                for j, blk in enumerate(row):
                    pblk[i, j] = jnp.exp2((blk - m_i).astype(BF16))
            m_buf[z, c] = jnp.concatenate(m_rows, axis=0)

            zeros = jnp.zeros((LANES, LANES), BF16)
            outs = [None] * nblk
            for lo, hi, i_min in chunks:
                p = jnp.concatenate([
                    jnp.concatenate([pblk.get((i, j), zeros) for j in range(lo, hi)], axis=1)
                    for i in range(i_min, nblk)], axis=0)
                o = jnp.dot(p, vbuf[own_rows(z, lo, hi), :], preferred_element_type=F32)
                for i in range(i_min, nblk):
                    piece = o[(i - i_min) * LANES:(i - i_min + 1) * LANES]
                    outs[i] = piece if outs[i] is None else outs[i] + piece
            acc_buf[z, c] = jnp.concatenate(outs, axis=0)

    def finalize(z, accs, ga_ref):
        o = [acc[:, :V_DIM] / acc[:, V_DIM:] for acc in accs]
        a = _rms_rows(o[0] - lam_ref[...] * o[1], sg_ref[...])
        o_ref[0, z] = (a * ga_ref[0].astype(F32)).astype(BF16)

    store_own(0)
    store_scores(0, 0)
    init_own(0)
    store_own(1)
    init_own(1)

    def far_pair(j, carry):
        i = 2 * j
        store_scores(1, i + 1)
        far_update(0, i)
        store_scores(0, i + 2)
        far_update(1, i + 1)
        return carry

    n_pairs = max((n_far - 3) // 2, 0)
    lax.fori_loop(0, n_pairs, far_pair, 0)
    slot0_done = False
    for i in range(2 * n_pairs, n_far):
        if i >= slot0_items and not slot0_done:
            finalize(0, [acc_buf[0, 0], acc_buf[0, 1]], gaa_ref)
            slot0_done = True
        if i + 1 < n_far:
            store_scores((i + 1) % 2, i + 1)
        last = far_update(i % 2, i, last=i == n_far - 1)
    finalize(1, last, gab_ref)


def _attention(q, k, v, kmeta, vmeta, ga, t0, t1, tm, lam, sg, tile):
    b, s, _ = q.shape
    nq = s // tile
    assert nq % 2 == 0
    tile_a = pl.BlockSpec((1, tile, V_DIM), lambda i, h, p: (i, p, h))
    tile_b = pl.BlockSpec((1, tile, V_DIM), lambda i, h, p: (i, nq - 1 - p, h))
    seq_spec = pl.BlockSpec((1, s, V_DIM), lambda i, h, p: (i, 0, h))
    meta_spec = pl.BlockSpec((LANES, V_DIM), lambda i, h, p: (0, h))
    bias_spec = pl.BlockSpec((2, LANES, LANES), lambda i, h, p: (h, 0, 0))
    row_spec = pl.BlockSpec((1, LANES), lambda i, h, p: (0, 0))
    out = jax.ShapeDtypeStruct((b, 2, s // 2, ATTN_WIDTH), BF16)
    return pl.pallas_call(
        functools.partial(_attn_kernel, tile=tile, nq=nq),
        grid=(b, ATTN_HEADS, nq // 2),
        in_specs=[
            tile_a, tile_b, seq_spec, seq_spec, meta_spec, meta_spec, tile_a, tile_b,
            bias_spec, bias_spec, bias_spec, row_spec, row_spec,
        ],
        out_specs=pl.BlockSpec((1, 2, tile, V_DIM), lambda i, h, p: (i, 0, p, h)),
        out_shape=out,
        scratch_shapes=[
            pltpu.VMEM((LANES + s, V_DIM), BF16),
            pltpu.VMEM((LANES + s, 2 * V_DIM), BF16),
            pltpu.VMEM((2, 2, tile, V_DIM), BF16),
            pltpu.VMEM((2, 2, tile, tile), F32),
            pltpu.VMEM((2, tile, tile + LANES), F32),
            pltpu.VMEM((2, 2, tile, 2 * V_DIM), F32),
            pltpu.VMEM((2, 2, tile, LANES), F32),
        ],
        compiler_params=pltpu.CompilerParams(
            dimension_semantics=("arbitrary", "arbitrary", "arbitrary"), vmem_limit_bytes=VMEM_LIMIT),
        name="attention",
    )(q, q, k, v, kmeta, vmeta, ga, ga, t0, t1, tm, lam, sg)


def _attn_bh_kernel(q_ref, k_ref, v_ref, kmeta_ref, vmeta_ref, ga_ref, t0_ref, t1_ref, tm_ref,
                    lam_ref, sg_ref, o_ref, kbuf, vbuf, *, tile, nq):
    nblk = tile // LANES
    half = nq // 2

    kbuf[0:LANES, :] = kmeta_ref[...]
    kbuf[LANES:, :] = k_ref[0]
    vbuf[0:LANES, 0:V_DIM] = vmeta_ref[...]
    vbuf[LANES:, 0:V_DIM] = v_ref[0]
    vbuf[:, V_DIM:] = jnp.ones((vbuf.shape[0], V_DIM), BF16)

    lane = lax.broadcasted_iota(jnp.int32, (tile, V_DIM), 1)
    lane_m = lax.broadcasted_iota(jnp.int32, (1, LANES), 1)
    mrow = jnp.where(lane_m < N_META, 0.0, NEG_INF)

    def masked_q(qt):
        q = q_ref[0, qt * tile:(qt + 1) * tile, :]
        zero = jnp.zeros_like(q)
        return jnp.where(lane < HEAD_DIM, q, zero), jnp.where(lane >= HEAD_DIM, q, zero)

    def scores(qm, kk):
        return lax.dot_general(qm, kk, (((1,), (1,)), ((), ())), preferred_element_type=F32)

    def rowmax(s):
        return jnp.max(s, axis=-1, keepdims=True)

    def lanes(x, n):
        return jnp.concatenate([x] * n, axis=1)

    def far_scores(qm, kt):
        kk = kbuf[kt * tile:(kt + 1) * tile, :]
        out = []
        for c in range(2):
            s = scores(qm[c], kk)
            if kt == 0:
                s = jnp.concatenate([s[:, :LANES] + mrow, s[:, LANES:]], axis=1)
            out.append(s)
        return out

    def far_update(state, s, kt):
        vv = vbuf[kt * tile:(kt + 1) * tile, :]
        new = []
        for c in range(2):
            m_old, acc = state[c]
            m_new = jnp.maximum(m_old, rowmax(s[c]))
            p = jnp.exp2((s[c] - lanes(m_new, nblk)).astype(BF16))
            new.append((m_new, lanes(jnp.exp2(m_old - m_new), 2) * acc
                        + jnp.dot(p, vv, preferred_element_type=F32)))
        return new

    chunk_blocks = MXU_DIM // LANES
    chunks = [(lo, min(lo + chunk_blocks, nblk + 1), max(lo - 1, 0))
              for lo in range(0, nblk + 1, chunk_blocks)]

    def own_rows(qt, lo, hi):
        return slice(qt * tile + lo * LANES, qt * tile + hi * LANES)

    def own_scores(qm, qt):
        return [[scores(qm[c][i_min * LANES:], kbuf[own_rows(qt, lo, hi), :]) for lo, hi, i_min in chunks]
                for c in range(2)]

    def own_init(own, qt, hc):
        state = []
        for c in range(2):
            t0, t1, tm = t0_ref[hc + c], t1_ref[hc + c], tm_ref[hc + c]

            def biased(i, j):
                ci = j // chunk_blocks
                lo, _, i_min = chunks[ci]
                blk = own[c][ci][(i - i_min) * LANES:(i - i_min + 1) * LANES,
                                 (j - lo) * LANES:(j - lo + 1) * LANES]
                if j == i + 1:
                    return blk + t0
                if j == 0 and qt == 0:
                    return blk + (tm if i == 0 else mrow)
                return blk + t1 if j == i else blk

            pblk, m_rows = {}, []
            for i in range(nblk):
                row = [biased(i, j) for j in range(i + 2)]
                m_i = jnp.broadcast_to(rowmax(jnp.concatenate(row, axis=1)), (LANES, LANES))
                m_rows.append(m_i)
                for j, blk in enumerate(row):
                    pblk[i, j] = jnp.exp2((blk - m_i).astype(BF16))

            zeros = jnp.zeros((LANES, LANES), BF16)
            outs = [None] * nblk
            for lo, hi, i_min in chunks:
                p = jnp.concatenate([
                    jnp.concatenate([pblk.get((i, j), zeros) for j in range(lo, hi)], axis=1)
                    for i in range(i_min, nblk)], axis=0)
                o = jnp.dot(p, vbuf[own_rows(qt, lo, hi), :], preferred_element_type=F32)
                for i in range(i_min, nblk):
                    piece = o[(i - i_min) * LANES:(i - i_min + 1) * LANES]
                    outs[i] = piece if outs[i] is None else outs[i] + piece
            state.append((jnp.concatenate(m_rows, axis=0), jnp.concatenate(outs, axis=0)))
        return state

    def finalize(state, qt):
        o = [acc[:, :V_DIM] / acc[:, V_DIM:] for _, acc in state]
        a = _rms_rows(o[0] - lam_ref[...] * o[1], sg_ref[...])
        rows = slice(qt * tile, (qt + 1) * tile)
        a = (a * ga_ref[0, rows, :].astype(F32)).astype(BF16)
        if qt < half:
            o_ref[0, 0, rows, :] = a
        else:
            pos = nq - 1 - qt
            o_ref[0, 1, pos * tile:(pos + 1) * tile, :] = a

    for pr in range(half):
        qts = (pr, nq - 1 - pr)
        qm = [masked_q(qt) for qt in qts]
        items = [(z, kt) for z in range(2) for kt in range(qts[z])]
        own = [own_scores(qm[z], qts[z]) for z in range(2)]
        s_next = far_scores(qm[items[0][0]], items[0][1]) if items else None
        state = [own_init(own[z], qts[z], 0) for z in range(2)]
        left = [qts[0], qts[1]]
        for z in range(2):
            if left[z] == 0:
                finalize(state[z], qts[z])
        for n, (z, kt) in enumerate(items):
            s_cur = s_next
            if n + 1 < len(items):
                s_next = far_scores(qm[items[n + 1][0]], items[n + 1][1])
            state[z] = far_update(state[z], s_cur, kt)
            left[z] -= 1
            if left[z] == 0:
                finalize(state[z], qts[z])


def _attention_bh(q, k, v, kmeta, vmeta, ga, t0, t1, tm, lam, sg, tile):
    b, s, _ = q.shape
    nq = s // tile
    assert nq % 2 == 0
    seq_spec = pl.BlockSpec((1, s, V_DIM), lambda i, h: (i, 0, h))
    meta_spec = pl.BlockSpec((LANES, V_DIM), lambda i, h: (0, h))
    bias_spec = pl.BlockSpec((2, LANES, LANES), lambda i, h: (h, 0, 0))
    row_spec = pl.BlockSpec((1, LANES), lambda i, h: (0, 0))
    return pl.pallas_call(
        functools.partial(_attn_bh_kernel, tile=tile, nq=nq),
        grid=(b, ATTN_HEADS),
        in_specs=[
            seq_spec, seq_spec, seq_spec, meta_spec, meta_spec, seq_spec,
            bias_spec, bias_spec, bias_spec, row_spec, row_spec,
        ],
        out_specs=pl.BlockSpec((1, 2, s // 2, V_DIM), lambda i, h: (i, 0, 0, h)),
        out_shape=jax.ShapeDtypeStruct((b, 2, s // 2, ATTN_WIDTH), BF16),
        scratch_shapes=[
            pltpu.VMEM((LANES + s, V_DIM), BF16),
            pltpu.VMEM((LANES + s, 2 * V_DIM), BF16),
        ],
        compiler_params=pltpu.CompilerParams(
            dimension_semantics=("arbitrary", "arbitrary"), vmem_limit_bytes=VMEM_LIMIT),
        name="attention",
    )(q, k, v, kmeta, vmeta, ga, t0, t1, tm, lam, sg)


def _merge_kernel(a_ref, c_ref, gm_ref, x_ref, wb_ref, wo_ref, o_ref):
    d = x_ref.shape[1]
    ya = jnp.dot(a_ref[0, 0], wb_ref[0], preferred_element_type=F32)
    yc = jnp.dot(c_ref[...], wb_ref[1], preferred_element_type=F32)
    merged = gm_ref[:, 0:d].astype(F32) * ya + gm_ref[:, d:2 * d].astype(F32) * yc
    o_ref[...] = x_ref[...] + jnp.dot(merged.astype(BF16), wo_ref[...], preferred_element_type=F32)


def _merge(a, c, gm, x, wb, wo, rows):
    n, d = x.shape
    row_spec = lambda width: pl.BlockSpec((rows, width), lambda i: (i, 0))
    half = a.shape[2] // rows
    nq = 2 * half

    def a_index(i):
        j = i % nq
        return i // nq, j // half, jnp.where(j < half, j, nq - 1 - j), 0

    return pl.pallas_call(
        _merge_kernel,
        grid=(n // rows,),
        in_specs=[
            pl.BlockSpec((1, 1, rows, a.shape[3]), a_index),
            row_spec(c.shape[1]), row_spec(gm.shape[1]), row_spec(d),
            pl.BlockSpec(wb.shape, lambda i: (0, 0, 0)),
            pl.BlockSpec(wo.shape, lambda i: (0, 0)),
        ],
        out_specs=row_spec(d),
        out_shape=jax.ShapeDtypeStruct((n, d), F32),
        compiler_params=pltpu.CompilerParams(
            dimension_semantics=("arbitrary",), vmem_limit_bytes=VMEM_LIMIT),
        name="merge",
    )(a, c, gm, x, wb, wo)


def kernel(x, meta_tokens, rel_bias, norm_g, w_in, q_norm_g, k_norm_g, lambda_q1, lambda_k1,
           lambda_q2, lambda_k2, subln_g, conv_w, w_branch, w_out):
    b, s, d = x.shape
    assert norm_g.shape[0] == 1, "single layer only"
    assert meta_tokens.shape[0] == N_META
    tile = min(ATTN_TILE, s // 2)
    rows = min(ROW_TILE, tile)
    assert s % (2 * tile) == 0 and s % rows == 0 and tile % MXU_DIM == 0

    w_bf = w_in[0].astype(BF16)
    groups = SEG // HEAD_DIM
    qg = jnp.tile(q_norm_g[0].astype(F32) * (HEAD_DIM ** -0.5 * LOG2E), groups)[None]
    kg = jnp.tile(k_norm_g[0].astype(F32), groups)[None]
    gidx = np.arange(MXU_DIM) // HEAD_DIM
    gmean = jnp.asarray((gidx[:, None] == gidx[None, :]).astype(np.float32) / HEAD_DIM, BF16)
    lamv = jnp.stack([lambda_q1[0], lambda_k1[0], lambda_q2[0], lambda_k2[0]]).astype(F32)
    rb = rel_bias.astype(F32).reshape(N_BUCKETS, 2 * ATTN_HEADS)
    ng = norm_g.astype(F32)

    kmeta, vmeta, umeta, lam, t0, t1, tm = _prologue(
        rb, meta_tokens.astype(F32), ng, w_bf, kg, gmean, lamv)
    q, k, v, ga, c, gm = _in_proj(x, ng, w_bf, qg, kg, gmean, conv_w[0].astype(F32), umeta, rows)
    sg = subln_g.astype(F32) * (1.0 - LAM_INIT)
    a = _attention_bh(q, k, v, kmeta, vmeta, ga, t0, t1, tm, lam, sg, tile)
    out = _merge(a, c.reshape(b * s, -1), gm.reshape(b * s, -1),
                 x.reshape(b * s, d), w_branch[0].astype(BF16), w_out[0].astype(BF16), tile)
    return out.reshape(b, s, d)
```

```python
import functools
import math

import numpy as np
import jax
import jax.numpy as jnp
from jax import lax
from jax.experimental import pallas as pl
from jax.experimental.pallas import tpu as pltpu

N_META = 16
ATTN_HEADS = 4
HEAD_DIM = 64
V_DIM = 2 * HEAD_DIM
ATTN_WIDTH = ATTN_HEADS * V_DIM
CONV_WIDTH = 512
CONV_K = 3
N_BRANCH = 2
N_BUCKETS = 32
MAX_DISTANCE = 128
EPS = 1e-6
NEG_INF = -1e30
LAM_INIT = 0.8 - 0.6 * math.exp(-0.3 * 0)
LOG2E = math.log2(math.e)

COL_Q, COL_K, COL_V, COL_GA, COL_CB, COL_CC, COL_CH, COL_GC, COL_GM = (
    0, 512, 1024, 1536, 2048, 2560, 3072, 3584, 4096)
SEG = 512

LANES = 128
MXU_DIM = 256
VMEM_LIMIT = 56 * 1024 * 1024

ATTN_TILE = 1024
ROW_TILE = 1024

F32 = jnp.float32
BF16 = jnp.bfloat16


def _bucket_thresholds():
    max_exact = N_BUCKETS // 2
    n = np.arange(0, 2 * MAX_DISTANCE)
    nf = np.maximum(n, max_exact).astype(np.float32)
    large = max_exact + (np.log(nf / np.float32(max_exact)) / np.float32(math.log(MAX_DISTANCE / max_exact))
                         * np.float32(N_BUCKETS - max_exact)).astype(np.int32)
    bucket = np.where(n < max_exact, n, np.minimum(large, N_BUCKETS - 1))
    return [int(np.argmax(bucket >= b)) for b in range(N_BUCKETS)]


_THR = _bucket_thresholds()
assert _THR[N_BUCKETS - 1] <= LANES


def _sigmoid(x):
    return 0.5 * jnp.tanh(0.5 * x) + 0.5


def _rms_rows(x, g):
    return x * lax.rsqrt(jnp.mean(x * x, axis=-1, keepdims=True) + EPS) * g


def _group_rms(x, gmean, g):
    sq = (x * x).astype(BF16)
    ms = jnp.concatenate([
        jnp.dot(sq[:, lo:lo + MXU_DIM], gmean, preferred_element_type=F32)
        for lo in range(0, x.shape[1], MXU_DIM)], axis=1)
    return x * lax.rsqrt(ms + EPS) * g


def _prologue_kernel(rb_ref, meta_ref, ng_ref, wk_ref, wv_ref, wcc_ref, wch_ref, kg_ref, gmean_ref,
                     lamv_ref, kmeta_ref, vmeta_ref, umeta_ref, lam_ref, t0_ref, t1_ref, tm_ref):
    xn = _rms_rows(meta_ref[...], ng_ref[...]).astype(BF16)
    k = jnp.dot(xn, wk_ref[...], preferred_element_type=F32)
    k = _group_rms(k, gmean_ref[...], kg_ref[...])
    v = jnp.dot(xn, wv_ref[...], preferred_element_type=F32)
    cc = jnp.dot(xn, wcc_ref[...], preferred_element_type=F32)
    ch = jnp.dot(xn, wch_ref[...], preferred_element_type=F32)
    kmeta_ref[...] = jnp.zeros(kmeta_ref.shape, BF16)
    vmeta_ref[...] = jnp.zeros(vmeta_ref.shape, BF16)
    kmeta_ref[0:N_META, :] = k.astype(BF16)
    vmeta_ref[0:N_META, :] = v.astype(BF16)
    umeta_ref[...] = cc * ch

    lv = lamv_ref[...]
    s1 = jnp.sum(lv[0:1] * lv[1:2], axis=-1, keepdims=True)
    s2 = jnp.sum(lv[2:3] * lv[3:4], axis=-1, keepdims=True)
    lam_ref[...] = jnp.broadcast_to(jnp.exp(s1) - jnp.exp(s2) + LAM_INIT, lam_ref.shape)

    row = lax.broadcasted_iota(jnp.int32, (LANES, LANES), 0)
    col = lax.broadcasted_iota(jnp.int32, (LANES, LANES), 1)
    d_diag = row - col
    d_sub = d_diag + LANES
    d_meta = d_diag + N_META

    def toeplitz(dist, hc):
        far = rb_ref[N_BUCKETS - 1, hc]
        t = jnp.full((LANES, LANES), (rb_ref[0, hc] - far) * LOG2E, F32)
        for b in range(1, N_BUCKETS):
            t = jnp.where(dist >= _THR[b], (rb_ref[b, hc] - far) * LOG2E, t)
        return t

    for hc in range(2 * ATTN_HEADS):
        t0_ref[hc] = jnp.where(d_diag >= 0, toeplitz(d_diag, hc), NEG_INF)
        t1_ref[hc] = toeplitz(d_sub, hc)
        tm_ref[hc] = jnp.where(col < N_META, toeplitz(d_meta, hc), NEG_INF)


def _prologue(rb, meta, ng, w_bf, kg, gmean, lamv):
    d = meta.shape[1]
    wspec = lambda j: pl.BlockSpec((d, SEG), lambda i, j=j: (0, j))
    full = lambda shape: pl.BlockSpec(shape, lambda i: (0,) * len(shape))
    blocks = jax.ShapeDtypeStruct((2 * ATTN_HEADS, LANES, LANES), F32)
    out_shape = (
        jax.ShapeDtypeStruct((LANES, SEG), BF16),
        jax.ShapeDtypeStruct((LANES, SEG), BF16),
        jax.ShapeDtypeStruct((N_META, SEG), F32),
        jax.ShapeDtypeStruct((1, LANES), F32),
        blocks,
        blocks,
        blocks,
    )
    return pl.pallas_call(
        _prologue_kernel,
        grid=(1,),
        in_specs=[
            pl.BlockSpec(memory_space=pltpu.SMEM),
            full(meta.shape), full(ng.shape),
            wspec(COL_K // SEG), wspec(COL_V // SEG), wspec(COL_CC // SEG), wspec(COL_CH // SEG),
            full(kg.shape), full(gmean.shape), full(lamv.shape),
        ],
        out_specs=tuple(full(s.shape) for s in out_shape),
        out_shape=out_shape,
        compiler_params=pltpu.CompilerParams(vmem_limit_bytes=VMEM_LIMIT),
        name="prologue",
    )(rb, meta, ng, w_bf, w_bf, w_bf, w_bf, kg, gmean, lamv)


def _in_proj_kernel(x_ref, ng_ref, w_ref, qg_ref, kg_ref, gmean_ref, cw_ref, umeta_ref,
                    q_ref, k_ref, v_ref, ga_ref, c_ref, gm_ref, ubuf, *, rows):
    t = pl.program_id(1)

    @pl.when(t == 0)
    def _():
        ubuf[0:8, :] = umeta_ref[N_META - 8:N_META, :]

    @pl.when(t > 0)
    def _():
        ubuf[0:8, :] = ubuf[rows:rows + 8, :]

    xn = _rms_rows(x_ref[0], ng_ref[...]).astype(BF16)

    def proj(lo, n=SEG):
        return jnp.dot(xn, w_ref[:, lo:lo + n], preferred_element_type=F32)

    for j in range(gm_ref.shape[2] // SEG):
        gm_ref[0, :, j * SEG:(j + 1) * SEG] = _sigmoid(proj(COL_GM + j * SEG)).astype(BF16)
    ga = proj(COL_GA)
    ga_ref[0] = (ga * _sigmoid(ga)).astype(BF16)

    u = proj(COL_CC) * proj(COL_CH)
    ubuf[8:8 + rows, :] = u
    cw = cw_ref[...]
    conv = cw[0:1] * ubuf[6:6 + rows, :] + cw[1:2] * ubuf[7:7 + rows, :] + cw[2:3] * u
    gc = proj(COL_GC)
    c_ref[0] = (proj(COL_CB) * conv * (gc * _sigmoid(gc))).astype(BF16)

    gmean = gmean_ref[...]
    q_ref[0] = _group_rms(proj(COL_Q), gmean, qg_ref[...]).astype(BF16)
    k_ref[0] = _group_rms(proj(COL_K), gmean, kg_ref[...]).astype(BF16)
    v_ref[0] = proj(COL_V).astype(BF16)


def _in_proj(x, ng, w_bf, qg, kg, gmean, cw, umeta, rows):
    b, s, d = x.shape
    ncols = w_bf.shape[1]
    gm_cols = ncols - COL_GM
    const = lambda shape: pl.BlockSpec(shape, lambda i, j: (0,) * len(shape))
    seg_out = pl.BlockSpec((1, rows, SEG), lambda i, j: (i, j, 0))
    out_shape = tuple(jax.ShapeDtypeStruct((b, s, SEG), BF16) for _ in range(5)) + (
        jax.ShapeDtypeStruct((b, s, gm_cols), BF16),)
    return pl.pallas_call(
        functools.partial(_in_proj_kernel, rows=rows),
        grid=(b, s // rows),
        in_specs=[
            pl.BlockSpec((1, rows, d), lambda i, j: (i, j, 0)),
            const(ng.shape),
            pl.BlockSpec(w_bf.shape, lambda i, j: (0, 0), pipeline_mode=pl.Buffered(1)),
            const(qg.shape), const(kg.shape), const(gmean.shape), const(cw.shape), const(umeta.shape),
        ],
        out_specs=(seg_out,) * 5 + (pl.BlockSpec((1, rows, gm_cols), lambda i, j: (i, j, 0)),),
        out_shape=out_shape,
        scratch_shapes=[pltpu.VMEM((rows + 8, SEG), F32)],
        compiler_params=pltpu.CompilerParams(
            dimension_semantics=("arbitrary", "arbitrary"), vmem_limit_bytes=VMEM_LIMIT),
        name="in_proj",
    )(x, ng, w_bf, qg, kg, gmean, cw, umeta)


def _attn_kernel(q_ref, k_ref, v_ref, kmeta_ref, vmeta_ref, ga_ref, t0_ref, t1_ref, tm_ref,
                 lam_ref, sg_ref, o_ref, kbuf, vbuf, *, tile, nq):
    nblk = tile // LANES

    kbuf[0:LANES, :] = kmeta_ref[...]
    kbuf[LANES:, :] = k_ref[0]
    vbuf[0:LANES, 0:V_DIM] = vmeta_ref[...]
    vbuf[LANES:, 0:V_DIM] = v_ref[0]
    vbuf[:, V_DIM:] = jnp.ones((vbuf.shape[0], V_DIM), BF16)

    lane = lax.broadcasted_iota(jnp.int32, (tile, V_DIM), 1)
    lane_m = lax.broadcasted_iota(jnp.int32, (1, LANES), 1)
    mrow = jnp.where(lane_m < N_META, 0.0, NEG_INF)

    def masked_q(qt):
        q = q_ref[0, qt * tile:(qt + 1) * tile, :]
        zero = jnp.zeros_like(q)
        return jnp.where(lane < HEAD_DIM, q, zero), jnp.where(lane >= HEAD_DIM, q, zero)

    def scores(qm, kk):
        return lax.dot_general(qm, kk, (((1,), (1,)), ((), ())), preferred_element_type=F32)

    def rowmax(s):
        return jnp.max(s, axis=-1, keepdims=True)

    def lanes(x, n):
        return jnp.concatenate([x] * n, axis=1)

    def far_scores(qm, kt):
        kk = kbuf[kt * tile:(kt + 1) * tile, :]
        out = []
        for c in range(2):
            s = scores(qm[c], kk)
            if kt == 0:
                s = jnp.concatenate([s[:, :LANES] + mrow, s[:, LANES:]], axis=1)
            out.append(s)
        return out

    def far_update(state, s, kt):
        vv = vbuf[kt * tile:(kt + 1) * tile, :]
        new = []
        for c in range(2):
            m_old, acc = state[c]
            m_new = jnp.maximum(m_old, rowmax(s[c]))
            p = jnp.exp2((s[c] - lanes(m_new, nblk)).astype(BF16))
            new.append((m_new, lanes(jnp.exp2(m_old - m_new), 2) * acc
                        + jnp.dot(p, vv, preferred_element_type=F32)))
        return new

    chunk_blocks = MXU_DIM // LANES
    chunks = [(lo, min(lo + chunk_blocks, nblk + 1), max(lo - 1, 0))
              for lo in range(0, nblk + 1, chunk_blocks)]

    def own_rows(qt, lo, hi):
        return slice(qt * tile + lo * LANES, qt * tile + hi * LANES)

    def own_scores(qm, qt):
        return [[scores(qm[c][i_min * LANES:], kbuf[own_rows(qt, lo, hi), :]) for lo, hi, i_min in chunks]
                for c in range(2)]

    def own_init(own, qt):
        state = []
        for c in range(2):
            t0, t1, tm = t0_ref[c], t1_ref[c], tm_ref[c]

            def biased(i, j):
                ci = j // chunk_blocks
                lo, _, i_min = chunks[ci]
                blk = own[c][ci][(i - i_min) * LANES:(i - i_min + 1) * LANES,
                                 (j - lo) * LANES:(j - lo + 1) * LANES]
                if j == i + 1:
                    return blk + t0
                if j == 0 and qt == 0:
                    return blk + (tm if i == 0 else mrow)
                return blk + t1 if j == i else blk

            pblk, m_rows = {}, []
            for i in range(nblk):
                row = [biased(i, j) for j in range(i + 2)]
                m_i = jnp.broadcast_to(rowmax(jnp.concatenate(row, axis=1)), (LANES, LANES))
                m_rows.append(m_i)
                for j, blk in enumerate(row):
                    pblk[i, j] = jnp.exp2((blk - m_i).astype(BF16))

            zeros = jnp.zeros((LANES, LANES), BF16)
            outs = [None] * nblk
            for lo, hi, i_min in chunks:
                p = jnp.concatenate([
                    jnp.concatenate([pblk.get((i, j), zeros) for j in range(lo, hi)], axis=1)
                    for i in range(i_min, nblk)], axis=0)
                o = jnp.dot(p, vbuf[own_rows(qt, lo, hi), :], preferred_element_type=F32)
                for i in range(i_min, nblk):
                    piece = o[(i - i_min) * LANES:(i - i_min + 1) * LANES]
                    outs[i] = piece if outs[i] is None else outs[i] + piece
            state.append((jnp.concatenate(m_rows, axis=0), jnp.concatenate(outs, axis=0)))
        return state

    def finalize(state, qt):
        o = [acc[:, :V_DIM] / acc[:, V_DIM:] for _, acc in state]
        a = _rms_rows(o[0] - lam_ref[...] * o[1], sg_ref[...])
        rows = slice(qt * tile, (qt + 1) * tile)
        o_ref[0, rows, :] = (a * ga_ref[0, rows, :].astype(F32)).astype(BF16)

    for pr in range(nq // 2):
        qts = (pr, nq - 1 - pr)
        qm = [masked_q(qt) for qt in qts]
        items = [(z, kt) for z in range(2) for kt in range(qts[z])]
        own = own_scores(qm[0], qts[0])
        s_next = far_scores(qm[items[0][0]], items[0][1]) if items else None
        state = [own_init(own, qts[0])]
        own = own_scores(qm[1], qts[1])
        state.append(own_init(own, qts[1]))
        left = list(qts)
        for z in range(2):
            if left[z] == 0:
                finalize(state[z], qts[z])
        for n, (z, kt) in enumerate(items):
            s_cur = s_next
            if n + 1 < len(items):
                s_next = far_scores(qm[items[n + 1][0]], items[n + 1][1])
            state[z] = far_update(state[z], s_cur, kt)
            left[z] -= 1
            if left[z] == 0:
                finalize(state[z], qts[z])


def _attention(q, k, v, kmeta, vmeta, ga, t0, t1, tm, lam, sg, tile):
    b, s, _ = q.shape
    nq = s // tile
    seq_spec = pl.BlockSpec((1, s, V_DIM), lambda i, h: (i, 0, h))
    meta_spec = pl.BlockSpec((LANES, V_DIM), lambda i, h: (0, h))
    bias_spec = pl.BlockSpec((2, LANES, LANES), lambda i, h: (h, 0, 0))
    row_spec = pl.BlockSpec((1, LANES), lambda i, h: (0, 0))
    return pl.pallas_call(
        functools.partial(_attn_kernel, tile=tile, nq=nq),
        grid=(b, ATTN_HEADS),
        in_specs=[
            seq_spec, seq_spec, seq_spec, meta_spec, meta_spec, seq_spec,
            bias_spec, bias_spec, bias_spec, row_spec, row_spec,
        ],
        out_specs=seq_spec,
        out_shape=jax.ShapeDtypeStruct((b, s, ATTN_WIDTH), BF16),
        scratch_shapes=[
            pltpu.VMEM((LANES + s, V_DIM), BF16),
            pltpu.VMEM((LANES + s, 2 * V_DIM), BF16),
        ],
        compiler_params=pltpu.CompilerParams(
            dimension_semantics=("arbitrary", "arbitrary"), vmem_limit_bytes=VMEM_LIMIT),
        name="attention",
    )(q, k, v, kmeta, vmeta, ga, t0, t1, tm, lam, sg)


def _merge_kernel(a_ref, c_ref, gm_ref, x_ref, wb_ref, wo_ref, o_ref):
    d = x_ref.shape[1]
    ya = jnp.dot(a_ref[...], wb_ref[0], preferred_element_type=F32)
    yc = jnp.dot(c_ref[...], wb_ref[1], preferred_element_type=F32)
    merged = gm_ref[:, 0:d].astype(F32) * ya + gm_ref[:, d:2 * d].astype(F32) * yc
    o_ref[...] = x_ref[...] + jnp.dot(merged.astype(BF16), wo_ref[...], preferred_element_type=F32)


def _merge(a, c, gm, x, wb, wo, rows):
    n, d = x.shape
    row_spec = lambda width: pl.BlockSpec((rows, width), lambda i: (i, 0))
    return pl.pallas_call(
        _merge_kernel,
        grid=(n // rows,),
        in_specs=[
            row_spec(a.shape[1]), row_spec(c.shape[1]), row_spec(gm.shape[1]), row_spec(d),
            pl.BlockSpec(wb.shape, lambda i: (0, 0, 0)),
            pl.BlockSpec(wo.shape, lambda i: (0, 0)),
        ],
        out_specs=row_spec(d),
        out_shape=jax.ShapeDtypeStruct((n, d), F32),
        compiler_params=pltpu.CompilerParams(
            dimension_semantics=("arbitrary",), vmem_limit_bytes=VMEM_LIMIT),
        name="merge",
    )(a, c, gm, x, wb, wo)


def kernel(x, meta_tokens, rel_bias, norm_g, w_in, q_norm_g, k_norm_g, lambda_q1, lambda_k1,
           lambda_q2, lambda_k2, subln_g, conv_w, w_branch, w_out):
    b, s, d = x.shape
    assert norm_g.shape[0] == 1, "single layer only"
    assert meta_tokens.shape[0] == N_META
    tile = min(ATTN_TILE, s // 2)
    rows = min(ROW_TILE, s)
    assert s % (2 * tile) == 0 and s % rows == 0 and tile % MXU_DIM == 0

    w_bf = w_in[0].astype(BF16)
    groups = SEG // HEAD_DIM
    qg = jnp.tile(q_norm_g[0].astype(F32) * (HEAD_DIM ** -0.5 * LOG2E), groups)[None]
    kg = jnp.tile(k_norm_g[0].astype(F32), groups)[None]
    gidx = np.arange(MXU_DIM) // HEAD_DIM
    gmean = jnp.asarray((gidx[:, None] == gidx[None, :]).astype(np.float32) / HEAD_DIM, BF16)
    lamv = jnp.stack([lambda_q1[0], lambda_k1[0], lambda_q2[0], lambda_k2[0]]).astype(F32)
    rb = rel_bias.astype(F32).reshape(N_BUCKETS, 2 * ATTN_HEADS)
    ng = norm_g.astype(F32)

    kmeta, vmeta, umeta, lam, t0, t1, tm = _prologue(
        rb, meta_tokens.astype(F32), ng, w_bf, kg, gmean, lamv)
    q, k, v, ga, c, gm = _in_proj(x, ng, w_bf, qg, kg, gmean, conv_w[0].astype(F32), umeta, rows)
    sg = subln_g.astype(F32) * (1.0 - LAM_INIT)
    a = _attention(q, k, v, kmeta, vmeta, ga, t0, t1, tm, lam, sg, tile)
    out = _merge(a.reshape(b * s, -1), c.reshape(b * s, -1), gm.reshape(b * s, -1),
                 x.reshape(b * s, d), w_branch[0].astype(BF16), w_out[0].astype(BF16), rows)
    return out.reshape(b, s, d)
```

```python
import functools
import math

import numpy as np
import jax
import jax.numpy as jnp
from jax import lax
from jax.experimental import pallas as pl
from jax.experimental.pallas import tpu as pltpu

N_META = 16
ATTN_HEADS = 4
HEAD_DIM = 64
V_DIM = 2 * HEAD_DIM
ATTN_WIDTH = ATTN_HEADS * V_DIM
CONV_WIDTH = 512
CONV_K = 3
N_BRANCH = 2
N_BUCKETS = 32
MAX_DISTANCE = 128
EPS = 1e-6
NEG_INF = -1e30
LAM_INIT = 0.8 - 0.6 * math.exp(-0.3 * 0)
LOG2E = math.log2(math.e)

COL_Q, COL_K, COL_V, COL_GA, COL_CB, COL_CC, COL_CH, COL_GC, COL_GM = (
    0, 512, 1024, 1536, 2048, 2560, 3072, 3584, 4096)
SEG = 512

LANES = 128
MXU_DIM = 256
VMEM_LIMIT = 56 * 1024 * 1024

ATTN_TILE = 1024
ROW_TILE = 1024
FAR_SPAN = 2048

F32 = jnp.float32
BF16 = jnp.bfloat16


def _bucket_thresholds():
    max_exact = N_BUCKETS // 2
    n = np.arange(0, 2 * MAX_DISTANCE)
    nf = np.maximum(n, max_exact).astype(np.float32)
    large = max_exact + (np.log(nf / np.float32(max_exact)) / np.float32(math.log(MAX_DISTANCE / max_exact))
                         * np.float32(N_BUCKETS - max_exact)).astype(np.int32)
    bucket = np.where(n < max_exact, n, np.minimum(large, N_BUCKETS - 1))
    return [int(np.argmax(bucket >= b)) for b in range(N_BUCKETS)]


_THR = _bucket_thresholds()
assert _THR[N_BUCKETS - 1] <= LANES


def _sigmoid(x):
    return 0.5 * jnp.tanh(0.5 * x) + 0.5


def _rms_rows(x, g):
    return x * lax.rsqrt(jnp.mean(x * x, axis=-1, keepdims=True) + EPS) * g


def _group_rms(x, gmean, g):
    sq = (x * x).astype(BF16)
    ms = jnp.concatenate([
        jnp.dot(sq[:, lo:lo + MXU_DIM], gmean, preferred_element_type=F32)
        for lo in range(0, x.shape[1], MXU_DIM)], axis=1)
    return x * lax.rsqrt(ms + EPS) * g


def _prologue_kernel(rb_ref, meta_ref, ng_ref, wk_ref, wv_ref, wcc_ref, wch_ref, kg_ref, gmean_ref,
                     lamv_ref, kmeta_ref, vmeta_ref, umeta_ref, lam_ref, t0_ref, t1_ref, tm_ref):
    xn = _rms_rows(meta_ref[...], ng_ref[...]).astype(BF16)
    k = jnp.dot(xn, wk_ref[...], preferred_element_type=F32)
    k = _group_rms(k, gmean_ref[...], kg_ref[...])
    v = jnp.dot(xn, wv_ref[...], preferred_element_type=F32)
    cc = jnp.dot(xn, wcc_ref[...], preferred_element_type=F32)
    ch = jnp.dot(xn, wch_ref[...], preferred_element_type=F32)
    kmeta_ref[...] = jnp.zeros(kmeta_ref.shape, BF16)
    vmeta_ref[...] = jnp.zeros(vmeta_ref.shape, BF16)
    kmeta_ref[0:N_META, :] = k.astype(BF16)
    vmeta_ref[0:N_META, :] = v.astype(BF16)
    umeta_ref[...] = cc * ch

    lv = lamv_ref[...]
    s1 = jnp.sum(lv[0:1] * lv[1:2], axis=-1, keepdims=True)
    s2 = jnp.sum(lv[2:3] * lv[3:4], axis=-1, keepdims=True)
    lam_ref[...] = jnp.broadcast_to(jnp.exp(s1) - jnp.exp(s2) + LAM_INIT, lam_ref.shape)

    row = lax.broadcasted_iota(jnp.int32, (LANES, LANES), 0)
    col = lax.broadcasted_iota(jnp.int32, (LANES, LANES), 1)
    d_diag = row - col
    d_sub = d_diag + LANES
    d_meta = d_diag + N_META

    def toeplitz(dist, hc):
        far = rb_ref[N_BUCKETS - 1, hc]
        t = jnp.full((LANES, LANES), (rb_ref[0, hc] - far) * LOG2E, F32)
        for b in range(1, N_BUCKETS):
            t = jnp.where(dist >= _THR[b], (rb_ref[b, hc] - far) * LOG2E, t)
        return t

    for hc in range(2 * ATTN_HEADS):
        t0_ref[hc] = jnp.where(d_diag >= 0, toeplitz(d_diag, hc), NEG_INF)
        t1_ref[hc] = toeplitz(d_sub, hc)
        tm_ref[hc] = jnp.where(col < N_META, toeplitz(d_meta, hc), NEG_INF)


def _prologue(rb, meta, ng, w_bf, kg, gmean, lamv):
    d = meta.shape[1]
    wspec = lambda j: pl.BlockSpec((d, SEG), lambda i, j=j: (0, j))
    full = lambda shape: pl.BlockSpec(shape, lambda i: (0,) * len(shape))
    blocks = jax.ShapeDtypeStruct((2 * ATTN_HEADS, LANES, LANES), F32)
    out_shape = (
        jax.ShapeDtypeStruct((LANES, SEG), BF16),
        jax.ShapeDtypeStruct((LANES, SEG), BF16),
        jax.ShapeDtypeStruct((N_META, SEG), F32),
        jax.ShapeDtypeStruct((1, LANES), F32),
        blocks,
        blocks,
        blocks,
    )
    return pl.pallas_call(
        _prologue_kernel,
        grid=(1,),
        in_specs=[
            pl.BlockSpec(memory_space=pltpu.SMEM),
            full(meta.shape), full(ng.shape),
            wspec(COL_K // SEG), wspec(COL_V // SEG), wspec(COL_CC // SEG), wspec(COL_CH // SEG),
            full(kg.shape), full(gmean.shape), full(lamv.shape),
        ],
        out_specs=tuple(full(s.shape) for s in out_shape),
        out_shape=out_shape,
        compiler_params=pltpu.CompilerParams(vmem_limit_bytes=VMEM_LIMIT),
        name="prologue",
    )(rb, meta, ng, w_bf, w_bf, w_bf, w_bf, kg, gmean, lamv)


def _in_proj_kernel(x_ref, ng_ref, w_ref, qg_ref, kg_ref, gmean_ref, cw_ref, umeta_ref,
                    q_ref, k_ref, v_ref, ga_ref, c_ref, gm_ref, ubuf, *, rows):
    t = pl.program_id(1)

    @pl.when(t == 0)
    def _():
        ubuf[0:8, :] = umeta_ref[N_META - 8:N_META, :]

    @pl.when(t > 0)
    def _():
        ubuf[0:8, :] = ubuf[rows:rows + 8, :]

    xn = _rms_rows(x_ref[0], ng_ref[...]).astype(BF16)

    def proj(lo, n=SEG):
        return jnp.dot(xn, w_ref[:, lo:lo + n], preferred_element_type=F32)

    for j in range(gm_ref.shape[2] // SEG):
        gm_ref[0, :, j * SEG:(j + 1) * SEG] = _sigmoid(proj(COL_GM + j * SEG)).astype(BF16)
    ga = proj(COL_GA)
    ga_ref[0] = (ga * _sigmoid(ga)).astype(BF16)

    u = proj(COL_CC) * proj(COL_CH)
    ubuf[8:8 + rows, :] = u
    cw = cw_ref[...]
    conv = cw[0:1] * ubuf[6:6 + rows, :] + cw[1:2] * ubuf[7:7 + rows, :] + cw[2:3] * u
    gc = proj(COL_GC)
    c_ref[0] = (proj(COL_CB) * conv * (gc * _sigmoid(gc))).astype(BF16)

    gmean = gmean_ref[...]
    q_ref[0] = _group_rms(proj(COL_Q), gmean, qg_ref[...]).astype(BF16)
    k_ref[0] = _group_rms(proj(COL_K), gmean, kg_ref[...]).astype(BF16)
    v_ref[0] = proj(COL_V).astype(BF16)


def _in_proj(x, ng, w_bf, qg, kg, gmean, cw, umeta, rows):
    b, s, d = x.shape
    ncols = w_bf.shape[1]
    gm_cols = ncols - COL_GM
    const = lambda shape: pl.BlockSpec(shape, lambda i, j: (0,) * len(shape))
    seg_out = pl.BlockSpec((1, rows, SEG), lambda i, j: (i, j, 0))
    out_shape = tuple(jax.ShapeDtypeStruct((b, s, SEG), BF16) for _ in range(5)) + (
        jax.ShapeDtypeStruct((b, s, gm_cols), BF16),)
    return pl.pallas_call(
        functools.partial(_in_proj_kernel, rows=rows),
        grid=(b, s // rows),
        in_specs=[
            pl.BlockSpec((1, rows, d), lambda i, j: (i, j, 0)),
            const(ng.shape),
            pl.BlockSpec(w_bf.shape, lambda i, j: (0, 0), pipeline_mode=pl.Buffered(1)),
            const(qg.shape), const(kg.shape), const(gmean.shape), const(cw.shape), const(umeta.shape),
        ],
        out_specs=(seg_out,) * 5 + (pl.BlockSpec((1, rows, gm_cols), lambda i, j: (i, j, 0)),),
        out_shape=out_shape,
        scratch_shapes=[pltpu.VMEM((rows + 8, SEG), F32)],
        compiler_params=pltpu.CompilerParams(
            dimension_semantics=("arbitrary", "arbitrary"), vmem_limit_bytes=VMEM_LIMIT),
        name="in_proj",
    )(x, ng, w_bf, qg, kg, gmean, cw, umeta)


def _attn_kernel(q_ref, k_ref, v_ref, kmeta_ref, vmeta_ref, ga_ref, t0_ref, t1_ref, tm_ref,
                 lam_ref, sg_ref, o_ref, kbuf, vbuf, *, tile, nq):
    nblk = tile // LANES

    kbuf[0:LANES, :] = kmeta_ref[...]
    kbuf[LANES:, :] = k_ref[0]
    vbuf[0:LANES, 0:V_DIM] = vmeta_ref[...]
    vbuf[LANES:, 0:V_DIM] = v_ref[0]
    vbuf[:, V_DIM:] = jnp.ones((vbuf.shape[0], V_DIM), BF16)

    lane = lax.broadcasted_iota(jnp.int32, (tile, V_DIM), 1)
    lane_m = lax.broadcasted_iota(jnp.int32, (1, LANES), 1)
    mrow = jnp.where(lane_m < N_META, 0.0, NEG_INF)

    def masked_q(qt):
        q = q_ref[0, qt * tile:(qt + 1) * tile, :]
        zero = jnp.zeros_like(q)
        return jnp.where(lane < HEAD_DIM, q, zero), jnp.where(lane >= HEAD_DIM, q, zero)

    def scores(qm, kk):
        return lax.dot_general(qm, kk, (((1,), (1,)), ((), ())), preferred_element_type=F32)

    def rowmax(s):
        return jnp.max(s, axis=-1, keepdims=True)

    def lanes(x, n):
        return jnp.concatenate([x] * n, axis=1)

    def far_scores(qm, keys):
        kk = kbuf[keys, :]
        out = []
        for c in range(2):
            s = scores(qm[c], kk)
            if keys.start == 0:
                s = jnp.concatenate([s[:, :LANES] + mrow, s[:, LANES:]], axis=1)
            out.append(s)
        return out

    def far_update(state, s, keys):
        vv = vbuf[keys, :]
        new = []
        for c in range(2):
            m_old, acc = state[c]
            m_new = jnp.maximum(m_old, rowmax(s[c]))
            p = jnp.exp2((s[c] - lanes(m_new, (keys.stop - keys.start) // LANES)).astype(BF16))
            new.append((m_new, lanes(jnp.exp2(m_old - m_new), 2) * acc
                        + jnp.dot(p, vv, preferred_element_type=F32)))
        return new

    chunk_blocks = MXU_DIM // LANES
    chunks = [(lo, min(lo + chunk_blocks, nblk + 1), max(lo - 1, 0))
              for lo in range(0, nblk + 1, chunk_blocks)]

    def own_rows(qt, lo, hi):
        return slice(qt * tile + lo * LANES, qt * tile + hi * LANES)

    def own_scores(qm, qt):
        return [[scores(qm[c][i_min * LANES:], kbuf[own_rows(qt, lo, hi), :]) for lo, hi, i_min in chunks]
                for c in range(2)]

    def own_init(own, qt):
        state = []
        for c in range(2):
            t0, t1, tm = t0_ref[c], t1_ref[c], tm_ref[c]

            def biased(i, j):
                ci = j // chunk_blocks
                lo, _, i_min = chunks[ci]
                blk = own[c][ci][(i - i_min) * LANES:(i - i_min + 1) * LANES,
                                 (j - lo) * LANES:(j - lo + 1) * LANES]
                if j == i + 1:
                    return blk + t0
                if j == 0 and qt == 0:
                    return blk + (tm if i == 0 else mrow)
                return blk + t1 if j == i else blk

            pblk, m_rows = {}, []
            for i in range(nblk):
                row = [biased(i, j) for j in range(i + 2)]
                m_i = jnp.broadcast_to(rowmax(jnp.concatenate(row, axis=1)), (LANES, LANES))
                m_rows.append(m_i)
                for j, blk in enumerate(row):
                    pblk[i, j] = jnp.exp2((blk - m_i).astype(BF16))

            zeros = jnp.zeros((LANES, LANES), BF16)
            outs = [None] * nblk
            for lo, hi, i_min in chunks:
                p = jnp.concatenate([
                    jnp.concatenate([pblk.get((i, j), zeros) for j in range(lo, hi)], axis=1)
                    for i in range(i_min, nblk)], axis=0)
                o = jnp.dot(p, vbuf[own_rows(qt, lo, hi), :], preferred_element_type=F32)
                for i in range(i_min, nblk):
                    piece = o[(i - i_min) * LANES:(i - i_min + 1) * LANES]
                    outs[i] = piece if outs[i] is None else outs[i] + piece
            state.append((jnp.concatenate(m_rows, axis=0), jnp.concatenate(outs, axis=0)))
        return state

    def finalize(state, qt):
        o = [acc[:, :V_DIM] / acc[:, V_DIM:] for _, acc in state]
        a = _rms_rows(o[0] - lam_ref[...] * o[1], sg_ref[...])
        rows = slice(qt * tile, (qt + 1) * tile)
        o_ref[0, rows, :] = (a * ga_ref[0, rows, :].astype(F32)).astype(BF16)

    for pr in range(nq // 2):
        qts = (pr, nq - 1 - pr)
        qm = [masked_q(qt) for qt in qts]
        items = [(z, slice(lo, min(lo + FAR_SPAN, qts[z] * tile)))
                 for z in range(2) for lo in range(0, qts[z] * tile, FAR_SPAN)]
        own = own_scores(qm[0], qts[0])
        s_next = far_scores(qm[items[0][0]], items[0][1]) if items else None
        state = [own_init(own, qts[0])]
        own = own_scores(qm[1], qts[1])
        state.append(own_init(own, qts[1]))
        left = [sum(1 for item in items if item[0] == z) for z in range(2)]
        for z in range(2):
            if left[z] == 0:
                finalize(state[z], qts[z])
        for n, (z, keys) in enumerate(items):
            s_cur = s_next
            if n + 1 < len(items):
                s_next = far_scores(qm[items[n + 1][0]], items[n + 1][1])
            state[z] = far_update(state[z], s_cur, keys)
            left[z] -= 1
            if left[z] == 0:
                finalize(state[z], qts[z])


def _attention(q, k, v, kmeta, vmeta, ga, t0, t1, tm, lam, sg, tile):
    b, s, _ = q.shape
    nq = s // tile
    seq_spec = pl.BlockSpec((1, s, V_DIM), lambda i, h: (i, 0, h))
    meta_spec = pl.BlockSpec((LANES, V_DIM), lambda i, h: (0, h))
    bias_spec = pl.BlockSpec((2, LANES, LANES), lambda i, h: (h, 0, 0))
    row_spec = pl.BlockSpec((1, LANES), lambda i, h: (0, 0))
    return pl.pallas_call(
        functools.partial(_attn_kernel, tile=tile, nq=nq),
        grid=(b, ATTN_HEADS),
        in_specs=[
            seq_spec, seq_spec, seq_spec, meta_spec, meta_spec, seq_spec,
            bias_spec, bias_spec, bias_spec, row_spec, row_spec,
        ],
        out_specs=seq_spec,
        out_shape=jax.ShapeDtypeStruct((b, s, ATTN_WIDTH), BF16),
        scratch_shapes=[
            pltpu.VMEM((LANES + s, V_DIM), BF16),
            pltpu.VMEM((LANES + s, 2 * V_DIM), BF16),
        ],
        compiler_params=pltpu.CompilerParams(
            dimension_semantics=("arbitrary", "arbitrary"), vmem_limit_bytes=VMEM_LIMIT),
        name="attention",
    )(q, k, v, kmeta, vmeta, ga, t0, t1, tm, lam, sg)


def _merge_kernel(a_ref, c_ref, gm_ref, x_ref, wb_ref, wo_ref, o_ref):
    d = x_ref.shape[1]
    ya = jnp.dot(a_ref[...], wb_ref[0], preferred_element_type=F32)
    yc = jnp.dot(c_ref[...], wb_ref[1], preferred_element_type=F32)
    merged = gm_ref[:, 0:d].astype(F32) * ya + gm_ref[:, d:2 * d].astype(F32) * yc
    o_ref[...] = x_ref[...] + jnp.dot(merged.astype(BF16), wo_ref[...], preferred_element_type=F32)


def _merge(a, c, gm, x, wb, wo, rows):
    n, d = x.shape
    row_spec = lambda width: pl.BlockSpec((rows, width), lambda i: (i, 0))
    return pl.pallas_call(
        _merge_kernel,
        grid=(n // rows,),
        in_specs=[
            row_spec(a.shape[1]), row_spec(c.shape[1]), row_spec(gm.shape[1]), row_spec(d),
            pl.BlockSpec(wb.shape, lambda i: (0, 0, 0)),
            pl.BlockSpec(wo.shape, lambda i: (0, 0)),
        ],
        out_specs=row_spec(d),
        out_shape=jax.ShapeDtypeStruct((n, d), F32),
        compiler_params=pltpu.CompilerParams(
            dimension_semantics=("arbitrary",), vmem_limit_bytes=VMEM_LIMIT),
        name="merge",
    )(a, c, gm, x, wb, wo)


def kernel(x, meta_tokens, rel_bias, norm_g, w_in, q_norm_g, k_norm_g, lambda_q1, lambda_k1,
           lambda_q2, lambda_k2, subln_g, conv_w, w_branch, w_out):
    b, s, d = x.shape
    assert norm_g.shape[0] == 1, "single layer only"
    assert meta_tokens.shape[0] == N_META
    tile = min(ATTN_TILE, s // 2)
    rows = min(ROW_TILE, s)
    assert s % (2 * tile) == 0 and s % rows == 0 and tile % MXU_DIM == 0

    w_bf = w_in[0].astype(BF16)
    groups = SEG // HEAD_DIM
    qg = jnp.tile(q_norm_g[0].astype(F32) * (HEAD_DIM ** -0.5 * LOG2E), groups)[None]
    kg = jnp.tile(k_norm_g[0].astype(F32), groups)[None]
    gidx = np.arange(MXU_DIM) // HEAD_DIM
    gmean = jnp.asarray((gidx[:, None] == gidx[None, :]).astype(np.float32) / HEAD_DIM, BF16)
    lamv = jnp.stack([lambda_q1[0], lambda_k1[0], lambda_q2[0], lambda_k2[0]]).astype(F32)
    rb = rel_bias.astype(F32).reshape(N_BUCKETS, 2 * ATTN_HEADS)
    ng = norm_g.astype(F32)

    kmeta, vmeta, umeta, lam, t0, t1, tm = _prologue(
        rb, meta_tokens.astype(F32), ng, w_bf, kg, gmean, lamv)
    q, k, v, ga, c, gm = _in_proj(x, ng, w_bf, qg, kg, gmean, conv_w[0].astype(F32), umeta, rows)
    sg = subln_g.astype(F32) * (1.0 - LAM_INIT)
    a = _attention(q, k, v, kmeta, vmeta, ga, t0, t1, tm, lam, sg, tile)
    out = _merge(a.reshape(b * s, -1), c.reshape(b * s, -1), gm.reshape(b * s, -1),
                 x.reshape(b * s, d), w_branch[0].astype(BF16), w_out[0].astype(BF16), rows)
    return out.reshape(b, s, d)
```

```python
import functools
import math

import numpy as np
import jax
import jax.numpy as jnp
from jax import lax
from jax.experimental import pallas as pl
from jax.experimental.pallas import tpu as pltpu

N_META = 16
ATTN_HEADS = 4
HEAD_DIM = 64
V_DIM = 2 * HEAD_DIM
ATTN_WIDTH = ATTN_HEADS * V_DIM
CONV_WIDTH = 512
CONV_K = 3
N_BRANCH = 2
N_BUCKETS = 32
MAX_DISTANCE = 128
EPS = 1e-6
NEG_INF = -1e30
LAM_INIT = 0.8 - 0.6 * math.exp(-0.3 * 0)
LOG2E = math.log2(math.e)

COL_Q, COL_K, COL_V, COL_GA, COL_CB, COL_CC, COL_CH, COL_GC, COL_GM = (
    0, 512, 1024, 1536, 2048, 2560, 3072, 3584, 4096)
SEG = 512

LANES = 128
MXU_DIM = 256
VMEM_LIMIT = 56 * 1024 * 1024

ATTN_TILE = 1024
ROW_TILE = 1024
FAR_SPAN = 2048

F32 = jnp.float32
BF16 = jnp.bfloat16


def _bucket_thresholds():
    max_exact = N_BUCKETS // 2
    n = np.arange(0, 2 * MAX_DISTANCE)
    nf = np.maximum(n, max_exact).astype(np.float32)
    large = max_exact + (np.log(nf / np.float32(max_exact)) / np.float32(math.log(MAX_DISTANCE / max_exact))
                         * np.float32(N_BUCKETS - max_exact)).astype(np.int32)
    bucket = np.where(n < max_exact, n, np.minimum(large, N_BUCKETS - 1))
    return [int(np.argmax(bucket >= b)) for b in range(N_BUCKETS)]


_THR = _bucket_thresholds()
assert _THR[N_BUCKETS - 1] <= LANES


def _sigmoid(x):
    return 0.5 * jnp.tanh(0.5 * x) + 0.5


def _rms_rows(x, g):
    return x * lax.rsqrt(jnp.mean(x * x, axis=-1, keepdims=True) + EPS) * g


def _group_rms(x, gmean, g):
    sq = (x * x).astype(BF16)
    ms = jnp.concatenate([
        jnp.dot(sq[:, lo:lo + MXU_DIM], gmean, preferred_element_type=F32)
        for lo in range(0, x.shape[1], MXU_DIM)], axis=1)
    return x * lax.rsqrt(ms + EPS) * g


def _prologue_kernel(rb_ref, meta_ref, ng_ref, wk_ref, wv_ref, wcc_ref, wch_ref, kg_ref, gmean_ref,
                     lamv_ref, kmeta_ref, vmeta_ref, umeta_ref, lam_ref, t0_ref, t1_ref, tm_ref):
    xn = _rms_rows(meta_ref[...], ng_ref[...]).astype(BF16)
    def proj(w_ref):
        return jnp.dot(xn, w_ref[...].astype(BF16), preferred_element_type=F32)

    k = _group_rms(proj(wk_ref), gmean_ref[...], kg_ref[...])
    v = proj(wv_ref)
    cc = proj(wcc_ref)
    ch = proj(wch_ref)
    kmeta_ref[...] = jnp.zeros(kmeta_ref.shape, BF16)
    vmeta_ref[...] = jnp.zeros(vmeta_ref.shape, BF16)
    kmeta_ref[0:N_META, :] = k.astype(BF16)
    vmeta_ref[0:N_META, :] = v.astype(BF16)
    umeta_ref[...] = cc * ch

    lv = lamv_ref[...]
    s1 = jnp.sum(lv[0:1] * lv[1:2], axis=-1, keepdims=True)
    s2 = jnp.sum(lv[2:3] * lv[3:4], axis=-1, keepdims=True)
    lam_ref[...] = jnp.broadcast_to(jnp.exp(s1) - jnp.exp(s2) + LAM_INIT, lam_ref.shape)

    row = lax.broadcasted_iota(jnp.int32, (LANES, LANES), 0)
    col = lax.broadcasted_iota(jnp.int32, (LANES, LANES), 1)
    d_diag = row - col
    d_sub = d_diag + LANES
    d_meta = d_diag + N_META

    def toeplitz(dist, hc):
        far = rb_ref[N_BUCKETS - 1, hc]
        t = jnp.full((LANES, LANES), (rb_ref[0, hc] - far) * LOG2E, F32)
        for b in range(1, N_BUCKETS):
            t = jnp.where(dist >= _THR[b], (rb_ref[b, hc] - far) * LOG2E, t)
        return t

    for hc in range(2 * ATTN_HEADS):
        t0_ref[hc] = jnp.where(d_diag >= 0, toeplitz(d_diag, hc), NEG_INF)
        t1_ref[hc] = toeplitz(d_sub, hc)
        tm_ref[hc] = jnp.where(col < N_META, toeplitz(d_meta, hc), NEG_INF)


def _prologue(rb, meta, ng, w, kg, gmean, lamv):
    d = meta.shape[1]
    wspec = lambda j: pl.BlockSpec((d, SEG), lambda i, j=j: (0, j))
    full = lambda shape: pl.BlockSpec(shape, lambda i: (0,) * len(shape))
    blocks = jax.ShapeDtypeStruct((2 * ATTN_HEADS, LANES, LANES), F32)
    out_shape = (
        jax.ShapeDtypeStruct((LANES, SEG), BF16),
        jax.ShapeDtypeStruct((LANES, SEG), BF16),
        jax.ShapeDtypeStruct((N_META, SEG), F32),
        jax.ShapeDtypeStruct((1, LANES), F32),
        blocks,
        blocks,
        blocks,
    )
    return pl.pallas_call(
        _prologue_kernel,
        grid=(1,),
        in_specs=[
            pl.BlockSpec(memory_space=pltpu.SMEM),
            full(meta.shape), full(ng.shape),
            wspec(COL_K // SEG), wspec(COL_V // SEG), wspec(COL_CC // SEG), wspec(COL_CH // SEG),
            full(kg.shape), full(gmean.shape), full(lamv.shape),
        ],
        out_specs=tuple(full(s.shape) for s in out_shape),
        out_shape=out_shape,
        compiler_params=pltpu.CompilerParams(vmem_limit_bytes=VMEM_LIMIT),
        name="prologue",
    )(rb, meta, ng, w, w, w, w, kg, gmean, lamv)


def _in_proj_kernel(x_ref, ng_ref, w_hbm, qg_ref, kg_ref, gmean_ref, cw_ref, umeta_ref,
                    q_ref, k_ref, v_ref, ga_ref, c_ref, gm_ref, ubuf, w_ref, w_stage, w_sem, *, rows):
    t = pl.program_id(1)

    @pl.when((pl.program_id(0) == 0) & (t == 0))
    def _():
        nseg = w_ref.shape[1] // SEG

        def seg_copy(j):
            return pltpu.make_async_copy(
                w_hbm.at[:, pl.ds(j * SEG, SEG)], w_stage.at[j % 2], w_sem.at[j % 2])

        seg_copy(0).start()
        for j in range(nseg):
            if j + 1 < nseg:
                seg_copy(j + 1).start()
            seg_copy(j).wait()
            w_ref[:, j * SEG:(j + 1) * SEG] = w_stage[j % 2].astype(BF16)

    @pl.when(t == 0)
    def _():
        ubuf[0:8, :] = umeta_ref[N_META - 8:N_META, :]

    @pl.when(t > 0)
    def _():
        ubuf[0:8, :] = ubuf[rows:rows + 8, :]

    xn = _rms_rows(x_ref[0], ng_ref[...]).astype(BF16)

    def proj(lo, n=SEG):
        return jnp.dot(xn, w_ref[:, lo:lo + n], preferred_element_type=F32)

    for j in range(gm_ref.shape[2] // SEG):
        gm_ref[0, :, j * SEG:(j + 1) * SEG] = _sigmoid(proj(COL_GM + j * SEG)).astype(BF16)
    ga = proj(COL_GA)
    ga_ref[0] = (ga * _sigmoid(ga)).astype(BF16)

    u = proj(COL_CC) * proj(COL_CH)
    ubuf[8:8 + rows, :] = u
    cw = cw_ref[...]
    conv = cw[0:1] * ubuf[6:6 + rows, :] + cw[1:2] * ubuf[7:7 + rows, :] + cw[2:3] * u
    gc = proj(COL_GC)
    c_ref[0] = (proj(COL_CB) * conv * (gc * _sigmoid(gc))).astype(BF16)

    gmean = gmean_ref[...]
    q_ref[0] = _group_rms(proj(COL_Q), gmean, qg_ref[...]).astype(BF16)
    k_ref[0] = _group_rms(proj(COL_K), gmean, kg_ref[...]).astype(BF16)
    v_ref[0] = proj(COL_V).astype(BF16)


def _in_proj(x, ng, w, qg, kg, gmean, cw, umeta, rows):
    b, s, d = x.shape
    ncols = w.shape[1]
    gm_cols = ncols - COL_GM
    const = lambda shape: pl.BlockSpec(shape, lambda i, j: (0,) * len(shape))
    seg_out = pl.BlockSpec((1, rows, SEG), lambda i, j: (i, j, 0))
    out_shape = tuple(jax.ShapeDtypeStruct((b, s, SEG), BF16) for _ in range(5)) + (
        jax.ShapeDtypeStruct((b, s, gm_cols), BF16),)
    return pl.pallas_call(
        functools.partial(_in_proj_kernel, rows=rows),
        grid=(b, s // rows),
        in_specs=[
            pl.BlockSpec((1, rows, d), lambda i, j: (i, j, 0)),
            const(ng.shape),
            pl.BlockSpec(memory_space=pl.ANY),
            const(qg.shape), const(kg.shape), const(gmean.shape), const(cw.shape), const(umeta.shape),
        ],
        out_specs=(seg_out,) * 5 + (pl.BlockSpec((1, rows, gm_cols), lambda i, j: (i, j, 0)),),
        out_shape=out_shape,
        scratch_shapes=[
            pltpu.VMEM((rows + 8, SEG), F32),
            pltpu.VMEM((d, ncols), BF16),
            pltpu.VMEM((2, d, SEG), F32),
            pltpu.SemaphoreType.DMA((2,)),
        ],
        compiler_params=pltpu.CompilerParams(
            dimension_semantics=("arbitrary", "arbitrary"), vmem_limit_bytes=VMEM_LIMIT),
        name="in_proj",
    )(x, ng, w, qg, kg, gmean, cw, umeta)


def _attn_kernel(q_ref, k_ref, v_ref, kmeta_ref, vmeta_ref, ga_ref, t0_ref, t1_ref, tm_ref,
                 lam_ref, sg_ref, o_ref, kbuf, vbuf, *, tile, nq):
    nblk = tile // LANES

    kbuf[0:LANES, :] = kmeta_ref[...]
    kbuf[LANES:, :] = k_ref[0]
    vbuf[0:LANES, 0:V_DIM] = vmeta_ref[...]
    vbuf[LANES:, 0:V_DIM] = v_ref[0]
    vbuf[:, V_DIM:] = jnp.ones((vbuf.shape[0], V_DIM), BF16)

    lane = lax.broadcasted_iota(jnp.int32, (tile, V_DIM), 1)
    lane_m = lax.broadcasted_iota(jnp.int32, (1, LANES), 1)
    mrow = jnp.where(lane_m < N_META, 0.0, NEG_INF)

    def masked_q(qt):
        q = q_ref[0, qt * tile:(qt + 1) * tile, :]
        zero = jnp.zeros_like(q)
        return jnp.where(lane < HEAD_DIM, q, zero), jnp.where(lane >= HEAD_DIM, q, zero)

    def scores(qm, kk):
        return lax.dot_general(qm, kk, (((1,), (1,)), ((), ())), preferred_element_type=F32)

    def rowmax(s):
        return jnp.max(s, axis=-1, keepdims=True)

    def lanes(x, n):
        return jnp.concatenate([x] * n, axis=1)

    def far_scores(qm, keys):
        kk = kbuf[keys, :]
        out = []
        for c in range(2):
            s = scores(qm[c], kk)
            if keys.start == 0:
                s = jnp.concatenate([s[:, :LANES] + mrow, s[:, LANES:]], axis=1)
            out.append(s)
        return out

    def far_update(state, s, keys):
        vv = vbuf[keys, :]
        new = []
        for c in range(2):
            m_old, acc = state[c]
            m_new = jnp.maximum(m_old, rowmax(s[c]))
            p = jnp.exp2((s[c] - lanes(m_new, (keys.stop - keys.start) // LANES)).astype(BF16))
            new.append((m_new, lanes(jnp.exp2(m_old - m_new), 2) * acc
                        + jnp.dot(p, vv, preferred_element_type=F32)))
        return new

    chunk_blocks = MXU_DIM // LANES
    chunks = [(lo, min(lo + chunk_blocks, nblk + 1), max(lo - 1, 0))
              for lo in range(0, nblk + 1, chunk_blocks)]

    def own_rows(qt, lo, hi):
        return slice(qt * tile + lo * LANES, qt * tile + hi * LANES)

    def own_scores(qm, qt):
        return [[scores(qm[c][i_min * LANES:], kbuf[own_rows(qt, lo, hi), :]) for lo, hi, i_min in chunks]
                for c in range(2)]

    def own_init(own, qt):
        state = []
        for c in range(2):
            t0, t1, tm = t0_ref[c], t1_ref[c], tm_ref[c]

            def biased(i, j):
                ci = j // chunk_blocks
                lo, _, i_min = chunks[ci]
                blk = own[c][ci][(i - i_min) * LANES:(i - i_min + 1) * LANES,
                                 (j - lo) * LANES:(j - lo + 1) * LANES]
                if j == i + 1:
                    return blk + t0
                if j == 0 and qt == 0:
                    return blk + (tm if i == 0 else mrow)
                return blk + t1 if j == i else blk

            pblk, m_rows = {}, []
            for i in range(nblk):
                row = [biased(i, j) for j in range(i + 2)]
                m_i = jnp.broadcast_to(rowmax(jnp.concatenate(row, axis=1)), (LANES, LANES))
                m_rows.append(m_i)
                for j, blk in enumerate(row):
                    pblk[i, j] = jnp.exp2((blk - m_i).astype(BF16))

            zeros = jnp.zeros((LANES, LANES), BF16)
            outs = [None] * nblk
            for lo, hi, i_min in chunks:
                p = jnp.concatenate([
                    jnp.concatenate([pblk.get((i, j), zeros) for j in range(lo, hi)], axis=1)
                    for i in range(i_min, nblk)], axis=0)
                o = jnp.dot(p, vbuf[own_rows(qt, lo, hi), :], preferred_element_type=F32)
                for i in range(i_min, nblk):
                    piece = o[(i - i_min) * LANES:(i - i_min + 1) * LANES]
                    outs[i] = piece if outs[i] is None else outs[i] + piece
            state.append((jnp.concatenate(m_rows, axis=0), jnp.concatenate(outs, axis=0)))
        return state

    def finalize(state, qt):
        o = [acc[:, :V_DIM] / acc[:, V_DIM:] for _, acc in state]
        a = _rms_rows(o[0] - lam_ref[...] * o[1], sg_ref[...])
        rows = slice(qt * tile, (qt + 1) * tile)
        o_ref[0, rows, :] = (a * ga_ref[0, rows, :].astype(F32)).astype(BF16)

    for pr in range(nq // 2):
        qts = (pr, nq - 1 - pr)
        qm = [masked_q(qt) for qt in qts]
        items = [(z, slice(lo, min(lo + FAR_SPAN, qts[z] * tile)))
                 for z in range(2) for lo in range(0, qts[z] * tile, FAR_SPAN)]
        own = own_scores(qm[0], qts[0])
        s_next = far_scores(qm[items[0][0]], items[0][1]) if items else None
        state = [own_init(own, qts[0])]
        own = own_scores(qm[1], qts[1])
        state.append(own_init(own, qts[1]))
        left = [sum(1 for item in items if item[0] == z) for z in range(2)]
        for z in range(2):
            if left[z] == 0:
                finalize(state[z], qts[z])
        for n, (z, keys) in enumerate(items):
            s_cur = s_next
            if n + 1 < len(items):
                s_next = far_scores(qm[items[n + 1][0]], items[n + 1][1])
            state[z] = far_update(state[z], s_cur, keys)
            left[z] -= 1
            if left[z] == 0:
                finalize(state[z], qts[z])


def _attention(q, k, v, kmeta, vmeta, ga, t0, t1, tm, lam, sg, tile):
    b, s, _ = q.shape
    nq = s // tile
    seq_spec = pl.BlockSpec((1, s, V_DIM), lambda i, h: (i, 0, h))
    meta_spec = pl.BlockSpec((LANES, V_DIM), lambda i, h: (0, h))
    bias_spec = pl.BlockSpec((2, LANES, LANES), lambda i, h: (h, 0, 0))
    row_spec = pl.BlockSpec((1, LANES), lambda i, h: (0, 0))
    return pl.pallas_call(
        functools.partial(_attn_kernel, tile=tile, nq=nq),
        grid=(b, ATTN_HEADS),
        in_specs=[
            seq_spec, seq_spec, seq_spec, meta_spec, meta_spec, seq_spec,
            bias_spec, bias_spec, bias_spec, row_spec, row_spec,
        ],
        out_specs=seq_spec,
        out_shape=jax.ShapeDtypeStruct((b, s, ATTN_WIDTH), BF16),
        scratch_shapes=[
            pltpu.VMEM((LANES + s, V_DIM), BF16),
            pltpu.VMEM((LANES + s, 2 * V_DIM), BF16),
        ],
        compiler_params=pltpu.CompilerParams(
            dimension_semantics=("arbitrary", "arbitrary"), vmem_limit_bytes=VMEM_LIMIT),
        name="attention",
    )(q, k, v, kmeta, vmeta, ga, t0, t1, tm, lam, sg)


def _merge_kernel(a_ref, c_ref, gm_ref, x_ref, wb_ref, wo_ref, o_ref):
    d = x_ref.shape[1]
    ya = jnp.dot(a_ref[...], wb_ref[0], preferred_element_type=F32)
    yc = jnp.dot(c_ref[...], wb_ref[1], preferred_element_type=F32)
    merged = gm_ref[:, 0:d].astype(F32) * ya + gm_ref[:, d:2 * d].astype(F32) * yc
    o_ref[...] = x_ref[...] + jnp.dot(merged.astype(BF16), wo_ref[...], preferred_element_type=F32)


def _merge(a, c, gm, x, wb, wo, rows):
    n, d = x.shape
    row_spec = lambda width: pl.BlockSpec((rows, width), lambda i: (i, 0))
    return pl.pallas_call(
        _merge_kernel,
        grid=(n // rows,),
        in_specs=[
            row_spec(a.shape[1]), row_spec(c.shape[1]), row_spec(gm.shape[1]), row_spec(d),
            pl.BlockSpec(wb.shape, lambda i: (0, 0, 0)),
            pl.BlockSpec(wo.shape, lambda i: (0, 0)),
        ],
        out_specs=row_spec(d),
        out_shape=jax.ShapeDtypeStruct((n, d), F32),
        compiler_params=pltpu.CompilerParams(
            dimension_semantics=("arbitrary",), vmem_limit_bytes=VMEM_LIMIT),
        name="merge",
    )(a, c, gm, x, wb, wo)


def kernel(x, meta_tokens, rel_bias, norm_g, w_in, q_norm_g, k_norm_g, lambda_q1, lambda_k1,
           lambda_q2, lambda_k2, subln_g, conv_w, w_branch, w_out):
    b, s, d = x.shape
    assert norm_g.shape[0] == 1, "single layer only"
    assert meta_tokens.shape[0] == N_META
    tile = min(ATTN_TILE, s // 2)
    rows = min(ROW_TILE, s)
    assert s % (2 * tile) == 0 and s % rows == 0 and tile % MXU_DIM == 0

    w = w_in[0].astype(F32)
    groups = SEG // HEAD_DIM
    qg = jnp.tile(q_norm_g[0].astype(F32) * (HEAD_DIM ** -0.5 * LOG2E), groups)[None]
    kg = jnp.tile(k_norm_g[0].astype(F32), groups)[None]
    gidx = np.arange(MXU_DIM) // HEAD_DIM
    gmean = jnp.asarray((gidx[:, None] == gidx[None, :]).astype(np.float32) / HEAD_DIM, BF16)
    lamv = jnp.stack([lambda_q1[0], lambda_k1[0], lambda_q2[0], lambda_k2[0]]).astype(F32)
    rb = rel_bias.astype(F32).reshape(N_BUCKETS, 2 * ATTN_HEADS)
    ng = norm_g.astype(F32)

    kmeta, vmeta, umeta, lam, t0, t1, tm = _prologue(
        rb, meta_tokens.astype(F32), ng, w, kg, gmean, lamv)
    q, k, v, ga, c, gm = _in_proj(x, ng, w, qg, kg, gmean, conv_w[0].astype(F32), umeta, rows)
    sg = subln_g.astype(F32) * (1.0 - LAM_INIT)
    a = _attention(q, k, v, kmeta, vmeta, ga, t0, t1, tm, lam, sg, tile)
    out = _merge(a.reshape(b * s, -1), c.reshape(b * s, -1), gm.reshape(b * s, -1),
                 x.reshape(b * s, d), w_branch[0].astype(BF16), w_out[0].astype(BF16), rows)
    return out.reshape(b, s, d)
```

```python
import functools
import math

import numpy as np
import jax
import jax.numpy as jnp
from jax import lax
from jax.experimental import pallas as pl
from jax.experimental.pallas import tpu as pltpu

N_META = 16
ATTN_HEADS = 4
HEAD_DIM = 64
V_DIM = 2 * HEAD_DIM
ATTN_WIDTH = ATTN_HEADS * V_DIM
CONV_WIDTH = 512
CONV_K = 3
N_BRANCH = 2
N_BUCKETS = 32
MAX_DISTANCE = 128
EPS = 1e-6
NEG_INF = -1e30
LAM_INIT = 0.8 - 0.6 * math.exp(-0.3 * 0)
LOG2E = math.log2(math.e)

COL_Q, COL_K, COL_V, COL_GA, COL_CB, COL_CC, COL_CH, COL_GC, COL_GM = (
    0, 512, 1024, 1536, 2048, 2560, 3072, 3584, 4096)
SEG = 512

LANES = 128
MXU_DIM = 256
VMEM_LIMIT = 56 * 1024 * 1024

ATTN_TILE = 1024
ROW_TILE = 1024
FAR_SPAN = 2048

F32 = jnp.float32
BF16 = jnp.bfloat16


def _bucket_thresholds():
    max_exact = N_BUCKETS // 2
    n = np.arange(0, 2 * MAX_DISTANCE)
    nf = np.maximum(n, max_exact).astype(np.float32)
    large = max_exact + (np.log(nf / np.float32(max_exact)) / np.float32(math.log(MAX_DISTANCE / max_exact))
                         * np.float32(N_BUCKETS - max_exact)).astype(np.int32)
    bucket = np.where(n < max_exact, n, np.minimum(large, N_BUCKETS - 1))
    return [int(np.argmax(bucket >= b)) for b in range(N_BUCKETS)]


_THR = _bucket_thresholds()
assert _THR[N_BUCKETS - 1] <= LANES


def _sigmoid(x):
    return 0.5 * jnp.tanh(0.5 * x) + 0.5


def _rms_rows(x, g):
    return x * lax.rsqrt(jnp.mean(x * x, axis=-1, keepdims=True) + EPS) * g


def _group_rms(x, gmean, g):
    sq = (x * x).astype(BF16)
    ms = jnp.concatenate([
        jnp.dot(sq[:, lo:lo + MXU_DIM], gmean, preferred_element_type=F32)
        for lo in range(0, x.shape[1], MXU_DIM)], axis=1)
    return x * lax.rsqrt(ms + EPS) * g


def _prologue_kernel(rb_ref, meta_ref, ng_ref, wk_ref, wv_ref, wcc_ref, wch_ref, kg_ref, gmean_ref,
                     lamv_ref, kmeta_ref, vmeta_ref, umeta_ref, lam_ref, t0_ref, t1_ref, tm_ref):
    xn = _rms_rows(meta_ref[...], ng_ref[...]).astype(BF16)
    k = jnp.dot(xn, wk_ref[...], preferred_element_type=F32)
    k = _group_rms(k, gmean_ref[...], kg_ref[...])
    v = jnp.dot(xn, wv_ref[...], preferred_element_type=F32)
    cc = jnp.dot(xn, wcc_ref[...], preferred_element_type=F32)
    ch = jnp.dot(xn, wch_ref[...], preferred_element_type=F32)
    kmeta_ref[...] = jnp.zeros(kmeta_ref.shape, BF16)
    vmeta_ref[...] = jnp.zeros(vmeta_ref.shape, BF16)
    kmeta_ref[0:N_META, :] = k.astype(BF16)
    vmeta_ref[0:N_META, :] = v.astype(BF16)
    umeta_ref[...] = cc * ch

    lv = lamv_ref[...]
    s1 = jnp.sum(lv[0:1] * lv[1:2], axis=-1, keepdims=True)
    s2 = jnp.sum(lv[2:3] * lv[3:4], axis=-1, keepdims=True)
    lam_ref[...] = jnp.broadcast_to(jnp.exp(s1) - jnp.exp(s2) + LAM_INIT, lam_ref.shape)

    row = lax.broadcasted_iota(jnp.int32, (LANES, LANES), 0)
    col = lax.broadcasted_iota(jnp.int32, (LANES, LANES), 1)
    d_diag = row - col
    d_sub = d_diag + LANES
    d_meta = d_diag + N_META

    def toeplitz(dist, hc):
        far = rb_ref[N_BUCKETS - 1, hc]
        t = jnp.full((LANES, LANES), (rb_ref[0, hc] - far) * LOG2E, F32)
        for b in range(1, N_BUCKETS):
            t = jnp.where(dist >= _THR[b], (rb_ref[b, hc] - far) * LOG2E, t)
        return t

    for hc in range(2 * ATTN_HEADS):
        t0_ref[hc] = jnp.where(d_diag >= 0, toeplitz(d_diag, hc), NEG_INF)
        t1_ref[hc] = toeplitz(d_sub, hc)
        tm_ref[hc] = jnp.where(col < N_META, toeplitz(d_meta, hc), NEG_INF)


def _prologue(rb, meta, ng, w_bf, kg, gmean, lamv):
    d = meta.shape[1]
    wspec = lambda j: pl.BlockSpec((d, SEG), lambda i, j=j: (0, j))
    full = lambda shape: pl.BlockSpec(shape, lambda i: (0,) * len(shape))
    blocks = jax.ShapeDtypeStruct((2 * ATTN_HEADS, LANES, LANES), F32)
    out_shape = (
        jax.ShapeDtypeStruct((LANES, SEG), BF16),
        jax.ShapeDtypeStruct((LANES, SEG), BF16),
        jax.ShapeDtypeStruct((N_META, SEG), F32),
        jax.ShapeDtypeStruct((1, LANES), F32),
        blocks,
        blocks,
        blocks,
    )
    return pl.pallas_call(
        _prologue_kernel,
        grid=(1,),
        in_specs=[
            pl.BlockSpec(memory_space=pltpu.SMEM),
            full(meta.shape), full(ng.shape),
            wspec(COL_K // SEG), wspec(COL_V // SEG), wspec(COL_CC // SEG), wspec(COL_CH // SEG),
            full(kg.shape), full(gmean.shape), full(lamv.shape),
        ],
        out_specs=tuple(full(s.shape) for s in out_shape),
        out_shape=out_shape,
        compiler_params=pltpu.CompilerParams(vmem_limit_bytes=VMEM_LIMIT),
        name="prologue",
    )(rb, meta, ng, w_bf, w_bf, w_bf, w_bf, kg, gmean, lamv)


def _in_proj_kernel(x_ref, ng_ref, w_ref, qg_ref, kg_ref, gmean_ref, cw_ref, umeta_ref,
                    q_ref, k_ref, v_ref, ga_ref, c_ref, gm_ref, ubuf, *, rows):
    t = pl.program_id(1)

    @pl.when(t == 0)
    def _():
        ubuf[0:8, :] = umeta_ref[N_META - 8:N_META, :]

    @pl.when(t > 0)
    def _():
        ubuf[0:8, :] = ubuf[rows:rows + 8, :]

    xn = _rms_rows(x_ref[0], ng_ref[...]).astype(BF16)

    def proj(lo, n=SEG):
        return jnp.dot(xn, w_ref[:, lo:lo + n], preferred_element_type=F32)

    for j in range(gm_ref.shape[2] // SEG):
        gm_ref[0, :, j * SEG:(j + 1) * SEG] = _sigmoid(proj(COL_GM + j * SEG)).astype(BF16)
    ga = proj(COL_GA)
    ga_ref[0] = (ga * _sigmoid(ga)).astype(BF16)

    u = proj(COL_CC) * proj(COL_CH)
    ubuf[8:8 + rows, :] = u
    cw = cw_ref[...]
    conv = cw[0:1] * ubuf[6:6 + rows, :] + cw[1:2] * ubuf[7:7 + rows, :] + cw[2:3] * u
    gc = proj(COL_GC)
    c_ref[0] = (proj(COL_CB) * conv * (gc * _sigmoid(gc))).astype(BF16)

    gmean = gmean_ref[...]
    q_ref[0] = _group_rms(proj(COL_Q), gmean, qg_ref[...]).astype(BF16)
    k_ref[0] = _group_rms(proj(COL_K), gmean, kg_ref[...]).astype(BF16)
    v_ref[0] = proj(COL_V).astype(BF16)


def _in_proj(x, ng, w_bf, qg, kg, gmean, cw, umeta, rows):
    b, s, d = x.shape
    ncols = w_bf.shape[1]
    gm_cols = ncols - COL_GM
    const = lambda shape: pl.BlockSpec(shape, lambda i, j: (0,) * len(shape))
    seg_out = pl.BlockSpec((1, rows, SEG), lambda i, j: (i, j, 0))
    out_shape = tuple(jax.ShapeDtypeStruct((b, s, SEG), BF16) for _ in range(5)) + (
        jax.ShapeDtypeStruct((b, s, gm_cols), BF16),)
    return pl.pallas_call(
        functools.partial(_in_proj_kernel, rows=rows),
        grid=(b, s // rows),
        in_specs=[
            pl.BlockSpec((1, rows, d), lambda i, j: (i, j, 0)),
            const(ng.shape),
            pl.BlockSpec(w_bf.shape, lambda i, j: (0, 0), pipeline_mode=pl.Buffered(1)),
            const(qg.shape), const(kg.shape), const(gmean.shape), const(cw.shape), const(umeta.shape),
        ],
        out_specs=(seg_out,) * 5 + (pl.BlockSpec((1, rows, gm_cols), lambda i, j: (i, j, 0)),),
        out_shape=out_shape,
        scratch_shapes=[pltpu.VMEM((rows + 8, SEG), F32)],
        compiler_params=pltpu.CompilerParams(
            dimension_semantics=("arbitrary", "arbitrary"), vmem_limit_bytes=VMEM_LIMIT),
        name="in_proj",
    )(x, ng, w_bf, qg, kg, gmean, cw, umeta)


def _attn_kernel(q_ref, k_ref, v_ref, kmeta_ref, vmeta_ref, ga_ref, t0_ref, t1_ref, tm_ref,
                 lam_ref, sg_ref, o_ref, *, tile, nq):
    nblk = tile // LANES

    def staged(meta_ref, x_ref, rows):
        if rows.start == 0:
            return jnp.concatenate([meta_ref[...], x_ref[0, 0:rows.stop - LANES, :]], axis=0)
        return x_ref[0, rows.start - LANES:rows.stop - LANES, :]

    def key_rows(rows):
        return staged(kmeta_ref, k_ref, rows)

    def value_rows(rows):
        vv = staged(vmeta_ref, v_ref, rows)
        return jnp.concatenate([vv, jnp.ones(vv.shape, BF16)], axis=1)

    lane = lax.broadcasted_iota(jnp.int32, (tile, V_DIM), 1)
    lane_m = lax.broadcasted_iota(jnp.int32, (1, LANES), 1)
    mrow = jnp.where(lane_m < N_META, 0.0, NEG_INF)

    def masked_q(qt):
        q = q_ref[0, qt * tile:(qt + 1) * tile, :]
        zero = jnp.zeros_like(q)
        return jnp.where(lane < HEAD_DIM, q, zero), jnp.where(lane >= HEAD_DIM, q, zero)

    def scores(qm, kk):
        return lax.dot_general(qm, kk, (((1,), (1,)), ((), ())), preferred_element_type=F32)

    def rowmax(s):
        return jnp.max(s, axis=-1, keepdims=True)

    def lanes(x, n):
        return jnp.concatenate([x] * n, axis=1)

    def far_scores(qm, keys):
        kk = key_rows(keys)
        out = []
        for c in range(2):
            s = scores(qm[c], kk)
            if keys.start == 0:
                s = jnp.concatenate([s[:, :LANES] + mrow, s[:, LANES:]], axis=1)
            out.append(s)
        return out

    def far_update(state, s, keys):
        vv = value_rows(keys)
        new = []
        for c in range(2):
            m_old, acc = state[c]
            m_new = jnp.maximum(m_old, rowmax(s[c]))
            p = jnp.exp2((s[c] - lanes(m_new, (keys.stop - keys.start) // LANES)).astype(BF16))
            new.append((m_new, lanes(jnp.exp2(m_old - m_new), 2) * acc
                        + jnp.dot(p, vv, preferred_element_type=F32)))
        return new

    chunk_blocks = MXU_DIM // LANES
    chunks = [(lo, min(lo + chunk_blocks, nblk + 1), max(lo - 1, 0))
              for lo in range(0, nblk + 1, chunk_blocks)]

    def own_rows(qt, lo, hi):
        return slice(qt * tile + lo * LANES, qt * tile + hi * LANES)

    def own_scores(qm, qt):
        return [[scores(qm[c][i_min * LANES:], key_rows(own_rows(qt, lo, hi))) for lo, hi, i_min in chunks]
                for c in range(2)]

    def own_init(own, qt):
        state = []
        for c in range(2):
            t0, t1, tm = t0_ref[c], t1_ref[c], tm_ref[c]

            def biased(i, j):
                ci = j // chunk_blocks
                lo, _, i_min = chunks[ci]
                blk = own[c][ci][(i - i_min) * LANES:(i - i_min + 1) * LANES,
                                 (j - lo) * LANES:(j - lo + 1) * LANES]
                if j == i + 1:
                    return blk + t0
                if j == 0 and qt == 0:
                    return blk + (tm if i == 0 else mrow)
                return blk + t1 if j == i else blk

            pblk, m_rows = {}, []
            for i in range(nblk):
                row = [biased(i, j) for j in range(i + 2)]
                m_i = jnp.broadcast_to(rowmax(jnp.concatenate(row, axis=1)), (LANES, LANES))
                m_rows.append(m_i)
                for j, blk in enumerate(row):
                    pblk[i, j] = jnp.exp2((blk - m_i).astype(BF16))

            zeros = jnp.zeros((LANES, LANES), BF16)
            outs = [None] * nblk
            for lo, hi, i_min in chunks:
                p = jnp.concatenate([
                    jnp.concatenate([pblk.get((i, j), zeros) for j in range(lo, hi)], axis=1)
                    for i in range(i_min, nblk)], axis=0)
                o = jnp.dot(p, value_rows(own_rows(qt, lo, hi)), preferred_element_type=F32)
                for i in range(i_min, nblk):
                    piece = o[(i - i_min) * LANES:(i - i_min + 1) * LANES]
                    outs[i] = piece if outs[i] is None else outs[i] + piece
            state.append((jnp.concatenate(m_rows, axis=0), jnp.concatenate(outs, axis=0)))
        return state

    def finalize(state, qt):
        o = [acc[:, :V_DIM] / acc[:, V_DIM:] for _, acc in state]
        a = _rms_rows(o[0] - lam_ref[...] * o[1], sg_ref[...])
        rows = slice(qt * tile, (qt + 1) * tile)
        o_ref[0, rows, :] = (a * ga_ref[0, rows, :].astype(F32)).astype(BF16)

    for pr in range(nq // 2):
        qts = (pr, nq - 1 - pr)
        qm = [masked_q(qt) for qt in qts]
        items = [(z, slice(lo, min(lo + FAR_SPAN, qts[z] * tile)))
                 for z in range(2) for lo in range(0, qts[z] * tile, FAR_SPAN)]
        own = own_scores(qm[0], qts[0])
        s_next = far_scores(qm[items[0][0]], items[0][1]) if items else None
        state = [own_init(own, qts[0])]
        own = own_scores(qm[1], qts[1])
        state.append(own_init(own, qts[1]))
        left = [sum(1 for item in items if item[0] == z) for z in range(2)]
        for z in range(2):
            if left[z] == 0:
                finalize(state[z], qts[z])
        for n, (z, keys) in enumerate(items):
            s_cur = s_next
            if n + 1 < len(items):
                s_next = far_scores(qm[items[n + 1][0]], items[n + 1][1])
            state[z] = far_update(state[z], s_cur, keys)
            left[z] -= 1
            if left[z] == 0:
                finalize(state[z], qts[z])


def _attention(q, k, v, kmeta, vmeta, ga, t0, t1, tm, lam, sg, tile):
    b, s, _ = q.shape
    nq = s // tile
    seq_spec = pl.BlockSpec((1, s, V_DIM), lambda i, h: (i, 0, h))
    meta_spec = pl.BlockSpec((LANES, V_DIM), lambda i, h: (0, h))
    bias_spec = pl.BlockSpec((2, LANES, LANES), lambda i, h: (h, 0, 0))
    row_spec = pl.BlockSpec((1, LANES), lambda i, h: (0, 0))
    return pl.pallas_call(
        functools.partial(_attn_kernel, tile=tile, nq=nq),
        grid=(b, ATTN_HEADS),
        in_specs=[
            seq_spec, seq_spec, seq_spec, meta_spec, meta_spec, seq_spec,
            bias_spec, bias_spec, bias_spec, row_spec, row_spec,
        ],
        out_specs=seq_spec,
        out_shape=jax.ShapeDtypeStruct((b, s, ATTN_WIDTH), BF16),
        compiler_params=pltpu.CompilerParams(
            dimension_semantics=("arbitrary", "arbitrary"), vmem_limit_bytes=VMEM_LIMIT),
        name="attention",
    )(q, k, v, kmeta, vmeta, ga, t0, t1, tm, lam, sg)


def _merge_kernel(a_ref, c_ref, gm_ref, x_ref, wb_ref, wo_ref, o_ref):
    d = x_ref.shape[1]
    ya = jnp.dot(a_ref[...], wb_ref[0], preferred_element_type=F32)
    yc = jnp.dot(c_ref[...], wb_ref[1], preferred_element_type=F32)
    merged = gm_ref[:, 0:d].astype(F32) * ya + gm_ref[:, d:2 * d].astype(F32) * yc
    o_ref[...] = x_ref[...] + jnp.dot(merged.astype(BF16), wo_ref[...], preferred_element_type=F32)


def _merge(a, c, gm, x, wb, wo, rows):
    n, d = x.shape
    row_spec = lambda width: pl.BlockSpec((rows, width), lambda i: (i, 0))
    return pl.pallas_call(
        _merge_kernel,
        grid=(n // rows,),
        in_specs=[
            row_spec(a.shape[1]), row_spec(c.shape[1]), row_spec(gm.shape[1]), row_spec(d),
            pl.BlockSpec(wb.shape, lambda i: (0, 0, 0)),
            pl.BlockSpec(wo.shape, lambda i: (0, 0)),
        ],
        out_specs=row_spec(d),
        out_shape=jax.ShapeDtypeStruct((n, d), F32),
        compiler_params=pltpu.CompilerParams(
            dimension_semantics=("arbitrary",), vmem_limit_bytes=VMEM_LIMIT),
        name="merge",
    )(a, c, gm, x, wb, wo)


def kernel(x, meta_tokens, rel_bias, norm_g, w_in, q_norm_g, k_norm_g, lambda_q1, lambda_k1,
           lambda_q2, lambda_k2, subln_g, conv_w, w_branch, w_out):
    b, s, d = x.shape
    assert norm_g.shape[0] == 1, "single layer only"
    assert meta_tokens.shape[0] == N_META
    tile = min(ATTN_TILE, s // 2)
    rows = min(ROW_TILE, s)
    assert s % (2 * tile) == 0 and s % rows == 0 and tile % MXU_DIM == 0

    w_bf = w_in[0].astype(BF16)
    groups = SEG // HEAD_DIM
    qg = jnp.tile(q_norm_g[0].astype(F32) * (HEAD_DIM ** -0.5 * LOG2E), groups)[None]
    kg = jnp.tile(k_norm_g[0].astype(F32), groups)[None]
    gidx = np.arange(MXU_DIM) // HEAD_DIM
    gmean = jnp.asarray((gidx[:, None] == gidx[None, :]).astype(np.float32) / HEAD_DIM, BF16)
    lamv = jnp.stack([lambda_q1[0], lambda_k1[0], lambda_q2[0], lambda_k2[0]]).astype(F32)
    rb = rel_bias.astype(F32).reshape(N_BUCKETS, 2 * ATTN_HEADS)
    ng = norm_g.astype(F32)

    kmeta, vmeta, umeta, lam, t0, t1, tm = _prologue(
        rb, meta_tokens.astype(F32), ng, w_bf, kg, gmean, lamv)
    q, k, v, ga, c, gm = _in_proj(x, ng, w_bf, qg, kg, gmean, conv_w[0].astype(F32), umeta, rows)
    sg = subln_g.astype(F32) * (1.0 - LAM_INIT)
    a = _attention(q, k, v, kmeta, vmeta, ga, t0, t1, tm, lam, sg, tile)
    out = _merge(a.reshape(b * s, -1), c.reshape(b * s, -1), gm.reshape(b * s, -1),
                 x.reshape(b * s, d), w_branch[0].astype(BF16), w_out[0].astype(BF16), rows)
    return out.reshape(b, s, d)
```

```python
import functools
import math

import numpy as np
import jax
import jax.numpy as jnp
from jax import lax
from jax.experimental import pallas as pl
from jax.experimental.pallas import tpu as pltpu

N_META = 16
ATTN_HEADS = 4
HEAD_DIM = 64
V_DIM = 2 * HEAD_DIM
ATTN_WIDTH = ATTN_HEADS * V_DIM
CONV_WIDTH = 512
CONV_K = 3
N_BRANCH = 2
N_BUCKETS = 32
MAX_DISTANCE = 128
EPS = 1e-6
NEG_INF = -1e30
LAM_INIT = 0.8 - 0.6 * math.exp(-0.3 * 0)
LOG2E = math.log2(math.e)

COL_Q, COL_K, COL_V, COL_GA, COL_CB, COL_CC, COL_CH, COL_GC, COL_GM = (
    0, 512, 1024, 1536, 2048, 2560, 3072, 3584, 4096)
SEG = 512

LANES = 128
MXU_DIM = 256
VMEM_LIMIT = 56 * 1024 * 1024

ATTN_TILE = 1024
ROW_TILE = 1024
FAR_SPAN = 2048

F32 = jnp.float32
BF16 = jnp.bfloat16


def _bucket_thresholds():
    max_exact = N_BUCKETS // 2
    n = np.arange(0, 2 * MAX_DISTANCE)
    nf = np.maximum(n, max_exact).astype(np.float32)
    large = max_exact + (np.log(nf / np.float32(max_exact)) / np.float32(math.log(MAX_DISTANCE / max_exact))
                         * np.float32(N_BUCKETS - max_exact)).astype(np.int32)
    bucket = np.where(n < max_exact, n, np.minimum(large, N_BUCKETS - 1))
    return [int(np.argmax(bucket >= b)) for b in range(N_BUCKETS)]


_THR = _bucket_thresholds()
assert _THR[N_BUCKETS - 1] <= LANES


def _sigmoid(x):
    return 0.5 * jnp.tanh(0.5 * x) + 0.5


def _rms_rows(x, g):
    return x * lax.rsqrt(jnp.mean(x * x, axis=-1, keepdims=True) + EPS) * g


def _group_rms(x, gmean, g):
    sq = (x * x).astype(BF16)
    ms = jnp.concatenate([
        jnp.dot(sq[:, lo:lo + MXU_DIM], gmean, preferred_element_type=F32)
        for lo in range(0, x.shape[1], MXU_DIM)], axis=1)
    return x * lax.rsqrt(ms + EPS) * g


def _prologue_kernel(rb_ref, meta_ref, ng_ref, wk_ref, wv_ref, wcc_ref, wch_ref, kg_ref, gmean_ref,
                     lamv_ref, kmeta_ref, vmeta_ref, umeta_ref, lam_ref, t0_ref, t1_ref, tm_ref):
    xn = _rms_rows(meta_ref[...], ng_ref[...]).astype(BF16)
    k = jnp.dot(xn, wk_ref[...], preferred_element_type=F32)
    k = _group_rms(k, gmean_ref[...], kg_ref[...])
    v = jnp.dot(xn, wv_ref[...], preferred_element_type=F32)
    cc = jnp.dot(xn, wcc_ref[...], preferred_element_type=F32)
    ch = jnp.dot(xn, wch_ref[...], preferred_element_type=F32)
    kmeta_ref[...] = jnp.zeros(kmeta_ref.shape, BF16)
    vmeta_ref[...] = jnp.zeros(vmeta_ref.shape, BF16)
    kmeta_ref[0:N_META, :] = k.astype(BF16)
    vmeta_ref[0:N_META, :] = v.astype(BF16)
    umeta_ref[...] = cc * ch

    lv = lamv_ref[...]
    s1 = jnp.sum(lv[0:1] * lv[1:2], axis=-1, keepdims=True)
    s2 = jnp.sum(lv[2:3] * lv[3:4], axis=-1, keepdims=True)
    lam_ref[...] = jnp.broadcast_to(jnp.exp(s1) - jnp.exp(s2) + LAM_INIT, lam_ref.shape)

    row = lax.broadcasted_iota(jnp.int32, (LANES, LANES), 0)
    col = lax.broadcasted_iota(jnp.int32, (LANES, LANES), 1)
    d_diag = row - col
    d_sub = d_diag + LANES
    d_meta = d_diag + N_META

    def toeplitz(dist, hc):
        far = rb_ref[N_BUCKETS - 1, hc]
        t = jnp.full((LANES, LANES), (rb_ref[0, hc] - far) * LOG2E, F32)
        for b in range(1, N_BUCKETS):
            t = jnp.where(dist >= _THR[b], (rb_ref[b, hc] - far) * LOG2E, t)
        return t

    for hc in range(2 * ATTN_HEADS):
        t0_ref[hc] = jnp.where(d_diag >= 0, toeplitz(d_diag, hc), NEG_INF)
        t1_ref[hc] = toeplitz(d_sub, hc)
        tm_ref[hc] = jnp.where(col < N_META, toeplitz(d_meta, hc), NEG_INF)


def _prologue(rb, meta, ng, w_bf, kg, gmean, lamv):
    d = meta.shape[1]
    wspec = lambda j: pl.BlockSpec((d, SEG), lambda i, j=j: (0, j))
    full = lambda shape: pl.BlockSpec(shape, lambda i: (0,) * len(shape))
    blocks = jax.ShapeDtypeStruct((2 * ATTN_HEADS, LANES, LANES), F32)
    out_shape = (
        jax.ShapeDtypeStruct((LANES, SEG), BF16),
        jax.ShapeDtypeStruct((LANES, SEG), BF16),
        jax.ShapeDtypeStruct((N_META, SEG), F32),
        jax.ShapeDtypeStruct((1, LANES), F32),
        blocks,
        blocks,
        blocks,
    )
    return pl.pallas_call(
        _prologue_kernel,
        grid=(1,),
        in_specs=[
            pl.BlockSpec(memory_space=pltpu.SMEM),
            full(meta.shape), full(ng.shape),
            wspec(COL_K // SEG), wspec(COL_V // SEG), wspec(COL_CC // SEG), wspec(COL_CH // SEG),
            full(kg.shape), full(gmean.shape), full(lamv.shape),
        ],
        out_specs=tuple(full(s.shape) for s in out_shape),
        out_shape=out_shape,
        compiler_params=pltpu.CompilerParams(vmem_limit_bytes=VMEM_LIMIT),
        name="prologue",
    )(rb, meta, ng, w_bf, w_bf, w_bf, w_bf, kg, gmean, lamv)


def _in_proj_kernel(x_ref, ng_ref, w_ref, qg_ref, kg_ref, gmean_ref, cw_ref, umeta_ref,
                    q_ref, k_ref, v_ref, ga_ref, c_ref, gm_ref, ubuf, *, rows):
    t = pl.program_id(1)

    @pl.when(t == 0)
    def _():
        ubuf[0:8, :] = umeta_ref[N_META - 8:N_META, :]

    @pl.when(t > 0)
    def _():
        ubuf[0:8, :] = ubuf[rows:rows + 8, :]

    xn = _rms_rows(x_ref[0], ng_ref[...]).astype(BF16)

    def proj(lo, n=SEG):
        return jnp.dot(xn, w_ref[:, lo:lo + n], preferred_element_type=F32)

    for j in range(gm_ref.shape[2] // SEG):
        gm_ref[0, :, j * SEG:(j + 1) * SEG] = _sigmoid(proj(COL_GM + j * SEG)).astype(BF16)
    ga = proj(COL_GA)
    ga_ref[0] = (ga * _sigmoid(ga)).astype(BF16)

    u = proj(COL_CC) * proj(COL_CH)
    ubuf[8:8 + rows, :] = u
    cw = cw_ref[...]
    conv = cw[0:1] * ubuf[6:6 + rows, :] + cw[1:2] * ubuf[7:7 + rows, :] + cw[2:3] * u
    gc = proj(COL_GC)
    c_ref[0] = (proj(COL_CB) * conv * (gc * _sigmoid(gc))).astype(BF16)

    gmean = gmean_ref[...]
    q_ref[0] = _group_rms(proj(COL_Q), gmean, qg_ref[...]).astype(BF16)
    k_ref[0] = _group_rms(proj(COL_K), gmean, kg_ref[...]).astype(BF16)
    v_ref[0] = proj(COL_V).astype(BF16)


def _in_proj(x, ng, w_bf, qg, kg, gmean, cw, umeta, rows):
    b, s, d = x.shape
    ncols = w_bf.shape[1]
    gm_cols = ncols - COL_GM
    const = lambda shape: pl.BlockSpec(shape, lambda i, j: (0,) * len(shape))
    seg_out = pl.BlockSpec((1, rows, SEG), lambda i, j: (i, j, 0))
    out_shape = tuple(jax.ShapeDtypeStruct((b, s, SEG), BF16) for _ in range(5)) + (
        jax.ShapeDtypeStruct((b, s, gm_cols), BF16),)
    return pl.pallas_call(
        functools.partial(_in_proj_kernel, rows=rows),
        grid=(b, s // rows),
        in_specs=[
            pl.BlockSpec((1, rows, d), lambda i, j: (i, j, 0)),
            const(ng.shape),
            pl.BlockSpec(w_bf.shape, lambda i, j: (0, 0), pipeline_mode=pl.Buffered(1)),
            const(qg.shape), const(kg.shape), const(gmean.shape), const(cw.shape), const(umeta.shape),
        ],
        out_specs=(seg_out,) * 5 + (pl.BlockSpec((1, rows, gm_cols), lambda i, j: (i, j, 0)),),
        out_shape=out_shape,
        scratch_shapes=[pltpu.VMEM((rows + 8, SEG), F32)],
        compiler_params=pltpu.CompilerParams(
            dimension_semantics=("arbitrary", "arbitrary"), vmem_limit_bytes=VMEM_LIMIT),
        name="in_proj",
    )(x, ng, w_bf, qg, kg, gmean, cw, umeta)


def _attn_kernel(q_ref, k_ref, v_ref, kmeta_ref, vmeta_ref, ga_ref, t0_ref, t1_ref, tm_ref,
                 lam_ref, sg_ref, o_ref, *, tile, nq):
    nblk = tile // LANES

    def staged(meta_ref, x_ref, rows):
        if rows.start == 0:
            return jnp.concatenate([meta_ref[...], x_ref[0, 0:rows.stop - LANES, :]], axis=0)
        return x_ref[0, rows.start - LANES:rows.stop - LANES, :]

    def key_rows(rows):
        return staged(kmeta_ref, k_ref, rows)

    def value_rows(rows):
        vv = staged(vmeta_ref, v_ref, rows)
        return jnp.concatenate([vv, jnp.ones(vv.shape, BF16)], axis=1)

    lane = lax.broadcasted_iota(jnp.int32, (tile, V_DIM), 1)
    lane_m = lax.broadcasted_iota(jnp.int32, (1, LANES), 1)
    mrow = jnp.where(lane_m < N_META, 0.0, NEG_INF)

    def masked_q(qt):
        q = q_ref[0, qt * tile:(qt + 1) * tile, :]
        zero = jnp.zeros_like(q)
        return jnp.where(lane < HEAD_DIM, q, zero), jnp.where(lane >= HEAD_DIM, q, zero)

    def scores(qm, kk):
        return lax.dot_general(qm, kk, (((1,), (1,)), ((), ())), preferred_element_type=F32)

    def rowmax(s):
        return jnp.max(s, axis=-1, keepdims=True)

    def lanes(x, n):
        return jnp.concatenate([x] * n, axis=1)

    def far_scores(qm, keys):
        kk = key_rows(keys)
        out = []
        for c in range(2):
            s = scores(qm[c], kk)
            if keys.start == 0:
                s = jnp.concatenate([s[:, :LANES] + mrow, s[:, LANES:]], axis=1)
            out.append(s)
        return out

    def far_update(state, s, keys):
        vv = value_rows(keys)
        new = []
        for c in range(2):
            m_old, acc = state[c]
            m_new = jnp.maximum(m_old, rowmax(s[c]))
            p = jnp.exp2((s[c] - lanes(m_new, (keys.stop - keys.start) // LANES)).astype(BF16))
            new.append((m_new, lanes(jnp.exp2(m_old - m_new), 2) * acc
                        + jnp.dot(p, vv, preferred_element_type=F32)))
        return new

    chunk_blocks = MXU_DIM // LANES
    chunks = [(lo, min(lo + chunk_blocks, nblk + 1), max(lo - 1, 0))
              for lo in range(0, nblk + 1, chunk_blocks)]

    def own_rows(qt, lo, hi):
        return slice(qt * tile + lo * LANES, qt * tile + hi * LANES)

    def own_scores(qm, qt):
        return [[scores(qm[c][i_min * LANES:], key_rows(own_rows(qt, lo, hi))) for lo, hi, i_min in chunks]
                for c in range(2)]

    def own_init(own, qt):
        state = []
        for c in range(2):
            t0, t1, tm = t0_ref[c], t1_ref[c], tm_ref[c]

            def biased(i, j):
                ci = j // chunk_blocks
                lo, _, i_min = chunks[ci]
                blk = own[c][ci][(i - i_min) * LANES:(i - i_min + 1) * LANES,
                                 (j - lo) * LANES:(j - lo + 1) * LANES]
                if j == i + 1:
                    return blk + t0
                if j == 0 and qt == 0:
                    return blk + (tm if i == 0 else mrow)
                return blk + t1 if j == i else blk

            pblk, m_rows = {}, []
            for i in range(nblk):
                row = [biased(i, j) for j in range(i + 2)]
                m_i = jnp.broadcast_to(rowmax(jnp.concatenate(row, axis=1)), (LANES, LANES))
                m_rows.append(m_i)
                for j, blk in enumerate(row):
                    pblk[i, j] = jnp.exp2((blk - m_i).astype(BF16))

            zeros = jnp.zeros((LANES, LANES), BF16)
            outs = [None] * nblk
            for lo, hi, i_min in chunks:
                p = jnp.concatenate([
                    jnp.concatenate([pblk.get((i, j), zeros) for j in range(lo, hi)], axis=1)
                    for i in range(i_min, nblk)], axis=0)
                o = jnp.dot(p, value_rows(own_rows(qt, lo, hi)), preferred_element_type=F32)
                for i in range(i_min, nblk):
                    piece = o[(i - i_min) * LANES:(i - i_min + 1) * LANES]
                    outs[i] = piece if outs[i] is None else outs[i] + piece
            state.append((jnp.concatenate(m_rows, axis=0), jnp.concatenate(outs, axis=0)))
        return state

    def finalize(state, qt):
        o = [acc[:, :V_DIM] / acc[:, V_DIM:] for _, acc in state]
        a = _rms_rows(o[0] - lam_ref[...] * o[1], sg_ref[...])
        rows = slice(qt * tile, (qt + 1) * tile)
        o_ref[0, rows, :] = (a * ga_ref[0, rows, :].astype(F32)).astype(BF16)

    for pr in range(nq // 2):
        qts = (pr, nq - 1 - pr)
        qm = [masked_q(qt) for qt in qts]
        items = [(z, slice(lo, min(lo + FAR_SPAN, qts[z] * tile)))
                 for z in range(2) for lo in range(0, qts[z] * tile, FAR_SPAN)]
        state = [own_init(own_scores(qm[z], qts[z]), qts[z]) for z in range(2)]
        left = [sum(1 for item in items if item[0] == z) for z in range(2)]
        for z in range(2):
            if left[z] == 0:
                finalize(state[z], qts[z])
        for z, keys in items:
            state[z] = far_update(state[z], far_scores(qm[z], keys), keys)
            left[z] -= 1
            if left[z] == 0:
                finalize(state[z], qts[z])


def _attention(q, k, v, kmeta, vmeta, ga, t0, t1, tm, lam, sg, tile):
    b, s, _ = q.shape
    nq = s // tile
    seq_spec = pl.BlockSpec((1, s, V_DIM), lambda i, h: (i, 0, h))
    meta_spec = pl.BlockSpec((LANES, V_DIM), lambda i, h: (0, h))
    bias_spec = pl.BlockSpec((2, LANES, LANES), lambda i, h: (h, 0, 0))
    row_spec = pl.BlockSpec((1, LANES), lambda i, h: (0, 0))
    return pl.pallas_call(
        functools.partial(_attn_kernel, tile=tile, nq=nq),
        grid=(b, ATTN_HEADS),
        in_specs=[
            seq_spec, seq_spec, seq_spec, meta_spec, meta_spec, seq_spec,
            bias_spec, bias_spec, bias_spec, row_spec, row_spec,
        ],
        out_specs=seq_spec,
        out_shape=jax.ShapeDtypeStruct((b, s, ATTN_WIDTH), BF16),
        compiler_params=pltpu.CompilerParams(
            dimension_semantics=("arbitrary", "arbitrary"), vmem_limit_bytes=VMEM_LIMIT),
        name="attention",
    )(q, k, v, kmeta, vmeta, ga, t0, t1, tm, lam, sg)


def _merge_kernel(a_ref, c_ref, gm_ref, x_ref, wb_ref, wo_ref, o_ref):
    d = x_ref.shape[1]
    ya = jnp.dot(a_ref[...], wb_ref[0], preferred_element_type=F32)
    yc = jnp.dot(c_ref[...], wb_ref[1], preferred_element_type=F32)
    merged = gm_ref[:, 0:d].astype(F32) * ya + gm_ref[:, d:2 * d].astype(F32) * yc
    o_ref[...] = x_ref[...] + jnp.dot(merged.astype(BF16), wo_ref[...], preferred_element_type=F32)


def _merge(a, c, gm, x, wb, wo, rows):
    n, d = x.shape
    row_spec = lambda width: pl.BlockSpec((rows, width), lambda i: (i, 0))
    return pl.pallas_call(
        _merge_kernel,
        grid=(n // rows,),
        in_specs=[
            row_spec(a.shape[1]), row_spec(c.shape[1]), row_spec(gm.shape[1]), row_spec(d),
            pl.BlockSpec(wb.shape, lambda i: (0, 0, 0)),
            pl.BlockSpec(wo.shape, lambda i: (0, 0)),
        ],
        out_specs=row_spec(d),
        out_shape=jax.ShapeDtypeStruct((n, d), F32),
        compiler_params=pltpu.CompilerParams(
            dimension_semantics=("arbitrary",), vmem_limit_bytes=VMEM_LIMIT),
        name="merge",
    )(a, c, gm, x, wb, wo)


def kernel(x, meta_tokens, rel_bias, norm_g, w_in, q_norm_g, k_norm_g, lambda_q1, lambda_k1,
           lambda_q2, lambda_k2, subln_g, conv_w, w_branch, w_out):
    b, s, d = x.shape
    assert norm_g.shape[0] == 1, "single layer only"
    assert meta_tokens.shape[0] == N_META
    tile = min(ATTN_TILE, s // 2)
    rows = min(ROW_TILE, s)
    assert s % (2 * tile) == 0 and s % rows == 0 and tile % MXU_DIM == 0

    w_bf = w_in[0].astype(BF16)
    groups = SEG // HEAD_DIM
    qg = jnp.tile(q_norm_g[0].astype(F32) * (HEAD_DIM ** -0.5 * LOG2E), groups)[None]
    kg = jnp.tile(k_norm_g[0].astype(F32), groups)[None]
    gidx = np.arange(MXU_DIM) // HEAD_DIM
    gmean = jnp.asarray((gidx[:, None] == gidx[None, :]).astype(np.float32) / HEAD_DIM, BF16)
    lamv = jnp.stack([lambda_q1[0], lambda_k1[0], lambda_q2[0], lambda_k2[0]]).astype(F32)
    rb = rel_bias.astype(F32).reshape(N_BUCKETS, 2 * ATTN_HEADS)
    ng = norm_g.astype(F32)

    kmeta, vmeta, umeta, lam, t0, t1, tm = _prologue(
        rb, meta_tokens.astype(F32), ng, w_bf, kg, gmean, lamv)
    q, k, v, ga, c, gm = _in_proj(x, ng, w_bf, qg, kg, gmean, conv_w[0].astype(F32), umeta, rows)
    sg = subln_g.astype(F32) * (1.0 - LAM_INIT)
    a = _attention(q, k, v, kmeta, vmeta, ga, t0, t1, tm, lam, sg, tile)
    out = _merge(a.reshape(b * s, -1), c.reshape(b * s, -1), gm.reshape(b * s, -1),
                 x.reshape(b * s, d), w_branch[0].astype(BF16), w_out[0].astype(BF16), rows)
    return out.reshape(b, s, d)
```

```python
import functools
import math

import numpy as np
import jax
import jax.numpy as jnp
from jax import lax
from jax.experimental import pallas as pl
from jax.experimental.pallas import tpu as pltpu

N_META = 16
ATTN_HEADS = 4
HEAD_DIM = 64
V_DIM = 2 * HEAD_DIM
ATTN_WIDTH = ATTN_HEADS * V_DIM
CONV_WIDTH = 512
CONV_K = 3
N_BRANCH = 2
N_BUCKETS = 32
MAX_DISTANCE = 128
EPS = 1e-6
NEG_INF = -1e30
LAM_INIT = 0.8 - 0.6 * math.exp(-0.3 * 0)
LOG2E = math.log2(math.e)

COL_Q, COL_K, COL_V, COL_GA, COL_CB, COL_CC, COL_CH, COL_GC, COL_GM = (
    0, 512, 1024, 1536, 2048, 2560, 3072, 3584, 4096)
SEG = 512

LANES = 128
MXU_DIM = 256
VMEM_LIMIT = 56 * 1024 * 1024

ATTN_TILE = 1024
ROW_TILE = 1024
FAR_SPAN = 3072

F32 = jnp.float32
BF16 = jnp.bfloat16


def _bucket_thresholds():
    max_exact = N_BUCKETS // 2
    n = np.arange(0, 2 * MAX_DISTANCE)
    nf = np.maximum(n, max_exact).astype(np.float32)
    large = max_exact + (np.log(nf / np.float32(max_exact)) / np.float32(math.log(MAX_DISTANCE / max_exact))
                         * np.float32(N_BUCKETS - max_exact)).astype(np.int32)
    bucket = np.where(n < max_exact, n, np.minimum(large, N_BUCKETS - 1))
    return [int(np.argmax(bucket >= b)) for b in range(N_BUCKETS)]


_THR = _bucket_thresholds()
assert _THR[N_BUCKETS - 1] <= LANES


def _sigmoid(x):
    return 0.5 * jnp.tanh(0.5 * x) + 0.5


def _rms_rows(x, g):
    return x * lax.rsqrt(jnp.mean(x * x, axis=-1, keepdims=True) + EPS) * g


def _group_rms(x, gmean, g):
    sq = (x * x).astype(BF16)
    ms = jnp.concatenate([
        jnp.dot(sq[:, lo:lo + MXU_DIM], gmean, preferred_element_type=F32)
        for lo in range(0, x.shape[1], MXU_DIM)], axis=1)
    return x * lax.rsqrt(ms + EPS) * g


def _prologue_kernel(rb_ref, meta_ref, ng_ref, wk_ref, wv_ref, wcc_ref, wch_ref, kg_ref, gmean_ref,
                     lamv_ref, kmeta_ref, vmeta_ref, umeta_ref, lam_ref, t0_ref, t1_ref, tm_ref):
    xn = _rms_rows(meta_ref[...], ng_ref[...]).astype(BF16)
    k = jnp.dot(xn, wk_ref[...], preferred_element_type=F32)
    k = _group_rms(k, gmean_ref[...], kg_ref[...])
    v = jnp.dot(xn, wv_ref[...], preferred_element_type=F32)
    cc = jnp.dot(xn, wcc_ref[...], preferred_element_type=F32)
    ch = jnp.dot(xn, wch_ref[...], preferred_element_type=F32)
    kmeta_ref[...] = jnp.zeros(kmeta_ref.shape, BF16)
    vmeta_ref[...] = jnp.zeros(vmeta_ref.shape, BF16)
    kmeta_ref[0:N_META, :] = k.astype(BF16)
    vmeta_ref[0:N_META, :] = v.astype(BF16)
    umeta_ref[...] = cc * ch

    lv = lamv_ref[...]
    s1 = jnp.sum(lv[0:1] * lv[1:2], axis=-1, keepdims=True)
    s2 = jnp.sum(lv[2:3] * lv[3:4], axis=-1, keepdims=True)
    lam_ref[...] = jnp.broadcast_to(jnp.exp(s1) - jnp.exp(s2) + LAM_INIT, lam_ref.shape)

    row = lax.broadcasted_iota(jnp.int32, (LANES, LANES), 0)
    col = lax.broadcasted_iota(jnp.int32, (LANES, LANES), 1)
    d_diag = row - col
    d_sub = d_diag + LANES
    d_meta = d_diag + N_META

    def toeplitz(dist, hc):
        far = rb_ref[N_BUCKETS - 1, hc]
        t = jnp.full((LANES, LANES), (rb_ref[0, hc] - far) * LOG2E, F32)
        for b in range(1, N_BUCKETS):
            t = jnp.where(dist >= _THR[b], (rb_ref[b, hc] - far) * LOG2E, t)
        return t

    for hc in range(2 * ATTN_HEADS):
        t0_ref[hc] = jnp.where(d_diag >= 0, toeplitz(d_diag, hc), NEG_INF)
        t1_ref[hc] = toeplitz(d_sub, hc)
        tm_ref[hc] = jnp.where(col < N_META, toeplitz(d_meta, hc), NEG_INF)


def _prologue(rb, meta, ng, w_bf, kg, gmean, lamv):
    d = meta.shape[1]
    wspec = lambda j: pl.BlockSpec((d, SEG), lambda i, j=j: (0, j))
    full = lambda shape: pl.BlockSpec(shape, lambda i: (0,) * len(shape))
    blocks = jax.ShapeDtypeStruct((2 * ATTN_HEADS, LANES, LANES), F32)
    out_shape = (
        jax.ShapeDtypeStruct((LANES, SEG), BF16),
        jax.ShapeDtypeStruct((LANES, SEG), BF16),
        jax.ShapeDtypeStruct((N_META, SEG), F32),
        jax.ShapeDtypeStruct((1, LANES), F32),
        blocks,
        blocks,
        blocks,
    )
    return pl.pallas_call(
        _prologue_kernel,
        grid=(1,),
        in_specs=[
            pl.BlockSpec(memory_space=pltpu.SMEM),
            full(meta.shape), full(ng.shape),
            wspec(COL_K // SEG), wspec(COL_V // SEG), wspec(COL_CC // SEG), wspec(COL_CH // SEG),
            full(kg.shape), full(gmean.shape), full(lamv.shape),
        ],
        out_specs=tuple(full(s.shape) for s in out_shape),
        out_shape=out_shape,
        compiler_params=pltpu.CompilerParams(vmem_limit_bytes=VMEM_LIMIT),
        name="prologue",
    )(rb, meta, ng, w_bf, w_bf, w_bf, w_bf, kg, gmean, lamv)


def _in_proj_kernel(x_ref, ng_ref, w_ref, qg_ref, kg_ref, gmean_ref, cw_ref, umeta_ref,
                    q_ref, k_ref, v_ref, ga_ref, c_ref, gm_ref, ubuf, *, rows):
    t = pl.program_id(1)

    @pl.when(t == 0)
    def _():
        ubuf[0:8, :] = umeta_ref[N_META - 8:N_META, :]

    @pl.when(t > 0)
    def _():
        ubuf[0:8, :] = ubuf[rows:rows + 8, :]

    xn = _rms_rows(x_ref[0], ng_ref[...]).astype(BF16)

    def proj(lo, n=SEG):
        return jnp.dot(xn, w_ref[:, lo:lo + n], preferred_element_type=F32)

    for j in range(gm_ref.shape[2] // SEG):
        gm_ref[0, :, j * SEG:(j + 1) * SEG] = _sigmoid(proj(COL_GM + j * SEG)).astype(BF16)
    ga = proj(COL_GA)
    ga_ref[0] = (ga * _sigmoid(ga)).astype(BF16)

    u = proj(COL_CC) * proj(COL_CH)
    ubuf[8:8 + rows, :] = u
    cw = cw_ref[...]
    conv = cw[0:1] * ubuf[6:6 + rows, :] + cw[1:2] * ubuf[7:7 + rows, :] + cw[2:3] * u
    gc = proj(COL_GC)
    c_ref[0] = (proj(COL_CB) * conv * (gc * _sigmoid(gc))).astype(BF16)

    gmean = gmean_ref[...]
    q_ref[0] = _group_rms(proj(COL_Q), gmean, qg_ref[...]).astype(BF16)
    k_ref[0] = _group_rms(proj(COL_K), gmean, kg_ref[...]).astype(BF16)
    v_ref[0] = proj(COL_V).astype(BF16)


def _in_proj(x, ng, w_bf, qg, kg, gmean, cw, umeta, rows):
    b, s, d = x.shape
    ncols = w_bf.shape[1]
    gm_cols = ncols - COL_GM
    const = lambda shape: pl.BlockSpec(shape, lambda i, j: (0,) * len(shape))
    seg_out = pl.BlockSpec((1, rows, SEG), lambda i, j: (i, j, 0))
    out_shape = tuple(jax.ShapeDtypeStruct((b, s, SEG), BF16) for _ in range(5)) + (
        jax.ShapeDtypeStruct((b, s, gm_cols), BF16),)
    return pl.pallas_call(
        functools.partial(_in_proj_kernel, rows=rows),
        grid=(b, s // rows),
        in_specs=[
            pl.BlockSpec((1, rows, d), lambda i, j: (i, j, 0)),
            const(ng.shape),
            pl.BlockSpec(w_bf.shape, lambda i, j: (0, 0), pipeline_mode=pl.Buffered(1)),
            const(qg.shape), const(kg.shape), const(gmean.shape), const(cw.shape), const(umeta.shape),
        ],
        out_specs=(seg_out,) * 5 + (pl.BlockSpec((1, rows, gm_cols), lambda i, j: (i, j, 0)),),
        out_shape=out_shape,
        scratch_shapes=[pltpu.VMEM((rows + 8, SEG), F32)],
        compiler_params=pltpu.CompilerParams(
            dimension_semantics=("arbitrary", "arbitrary"), vmem_limit_bytes=VMEM_LIMIT),
        name="in_proj",
    )(x, ng, w_bf, qg, kg, gmean, cw, umeta)


def _attn_kernel(q_ref, k_ref, v_ref, kmeta_ref, vmeta_ref, ga_ref, t0_ref, t1_ref, tm_ref,
                 lam_ref, sg_ref, o_ref, *, tile, nq):
    nblk = tile // LANES

    def staged(meta_ref, x_ref, rows):
        if rows.start == 0:
            return jnp.concatenate([meta_ref[...], x_ref[0, 0:rows.stop - LANES, :]], axis=0)
        return x_ref[0, rows.start - LANES:rows.stop - LANES, :]

    def key_rows(rows):
        return staged(kmeta_ref, k_ref, rows)

    def value_rows(rows):
        vv = staged(vmeta_ref, v_ref, rows)
        return jnp.concatenate([vv, jnp.ones(vv.shape, BF16)], axis=1)

    lane = lax.broadcasted_iota(jnp.int32, (tile, V_DIM), 1)
    lane_m = lax.broadcasted_iota(jnp.int32, (1, LANES), 1)
    mrow = jnp.where(lane_m < N_META, 0.0, NEG_INF)

    def masked_q(qt):
        q = q_ref[0, qt * tile:(qt + 1) * tile, :]
        zero = jnp.zeros_like(q)
        return jnp.where(lane < HEAD_DIM, q, zero), jnp.where(lane >= HEAD_DIM, q, zero)

    def scores(qm, kk):
        return lax.dot_general(qm, kk, (((1,), (1,)), ((), ())), preferred_element_type=F32)

    def rowmax(s):
        return jnp.max(s, axis=-1, keepdims=True)

    def lanes(x, n):
        return jnp.concatenate([x] * n, axis=1)

    def far_scores(qm, keys):
        kk = key_rows(keys)
        out = []
        for c in range(2):
            s = scores(qm[c], kk)
            if keys.start == 0:
                s = jnp.concatenate([s[:, :LANES] + mrow, s[:, LANES:]], axis=1)
            out.append(s)
        return out

    def far_update(state, s, keys):
        vv = value_rows(keys)
        new = []
        for c in range(2):
            m_old, acc = state[c]
            m_new = jnp.maximum(m_old, rowmax(s[c]))
            p = jnp.exp2((s[c] - lanes(m_new, (keys.stop - keys.start) // LANES)).astype(BF16))
            new.append((m_new, lanes(jnp.exp2(m_old - m_new), 2) * acc
                        + jnp.dot(p, vv, preferred_element_type=F32)))
        return new

    chunk_blocks = MXU_DIM // LANES
    chunks = [(lo, min(lo + chunk_blocks, nblk + 1), max(lo - 1, 0))
              for lo in range(0, nblk + 1, chunk_blocks)]

    def own_rows(qt, lo, hi):
        return slice(qt * tile + lo * LANES, qt * tile + hi * LANES)

    def own_scores(qm, qt):
        return [[scores(qm[c][i_min * LANES:], key_rows(own_rows(qt, lo, hi))) for lo, hi, i_min in chunks]
                for c in range(2)]

    def own_init(own, qt):
        state = []
        for c in range(2):
            t0, t1, tm = t0_ref[c], t1_ref[c], tm_ref[c]

            def biased(i, j):
                ci = j // chunk_blocks
                lo, _, i_min = chunks[ci]
                blk = own[c][ci][(i - i_min) * LANES:(i - i_min + 1) * LANES,
                                 (j - lo) * LANES:(j - lo + 1) * LANES]
                if j == i + 1:
                    return blk + t0
                if j == 0 and qt == 0:
                    return blk + (tm if i == 0 else mrow)
                return blk + t1 if j == i else blk

            pblk, m_rows = {}, []
            for i in range(nblk):
                row = [biased(i, j) for j in range(i + 2)]
                m_i = jnp.broadcast_to(rowmax(jnp.concatenate(row, axis=1)), (LANES, LANES))
                m_rows.append(m_i)
                for j, blk in enumerate(row):
                    pblk[i, j] = jnp.exp2((blk - m_i).astype(BF16))

            zeros = jnp.zeros((LANES, LANES), BF16)
            outs = [None] * nblk
            for lo, hi, i_min in chunks:
                p = jnp.concatenate([
                    jnp.concatenate([pblk.get((i, j), zeros) for j in range(lo, hi)], axis=1)
                    for i in range(i_min, nblk)], axis=0)
                o = jnp.dot(p, value_rows(own_rows(qt, lo, hi)), preferred_element_type=F32)
                for i in range(i_min, nblk):
                    piece = o[(i - i_min) * LANES:(i - i_min + 1) * LANES]
                    outs[i] = piece if outs[i] is None else outs[i] + piece
            state.append((jnp.concatenate(m_rows, axis=0), jnp.concatenate(outs, axis=0)))
        return state

    def finalize(state, qt):
        o = [acc[:, :V_DIM] / acc[:, V_DIM:] for _, acc in state]
        a = _rms_rows(o[0] - lam_ref[...] * o[1], sg_ref[...])
        rows = slice(qt * tile, (qt + 1) * tile)
        o_ref[0, rows, :] = (a * ga_ref[0, rows, :].astype(F32)).astype(BF16)

    for pr in range(nq // 2):
        qts = (pr, nq - 1 - pr)
        qm = [masked_q(qt) for qt in qts]
        items = [(z, slice(lo, min(lo + FAR_SPAN, qts[z] * tile)))
                 for z in range(2) for lo in range(0, qts[z] * tile, FAR_SPAN)]
        state = [own_init(own_scores(qm[z], qts[z]), qts[z]) for z in range(2)]
        left = [sum(1 for item in items if item[0] == z) for z in range(2)]
        for z in range(2):
            if left[z] == 0:
                finalize(state[z], qts[z])
        for z, keys in items:
            state[z] = far_update(state[z], far_scores(qm[z], keys), keys)
            left[z] -= 1
            if left[z] == 0:
                finalize(state[z], qts[z])


def _attention(q, k, v, kmeta, vmeta, ga, t0, t1, tm, lam, sg, tile):
    b, s, _ = q.shape
    nq = s // tile
    seq_spec = pl.BlockSpec((1, s, V_DIM), lambda i, h: (i, 0, h))
    meta_spec = pl.BlockSpec((LANES, V_DIM), lambda i, h: (0, h))
    bias_spec = pl.BlockSpec((2, LANES, LANES), lambda i, h: (h, 0, 0))
    row_spec = pl.BlockSpec((1, LANES), lambda i, h: (0, 0))
    return pl.pallas_call(
        functools.partial(_attn_kernel, tile=tile, nq=nq),
        grid=(b, ATTN_HEADS),
        in_specs=[
            seq_spec, seq_spec, seq_spec, meta_spec, meta_spec, seq_spec,
            bias_spec, bias_spec, bias_spec, row_spec, row_spec,
        ],
        out_specs=seq_spec,
        out_shape=jax.ShapeDtypeStruct((b, s, ATTN_WIDTH), BF16),
        compiler_params=pltpu.CompilerParams(
            dimension_semantics=("arbitrary", "arbitrary"), vmem_limit_bytes=VMEM_LIMIT),
        name="attention",
    )(q, k, v, kmeta, vmeta, ga, t0, t1, tm, lam, sg)


def _merge_kernel(a_ref, c_ref, gm_ref, x_ref, wb_ref, wo_ref, o_ref):
    d = x_ref.shape[1]
    ya = jnp.dot(a_ref[...], wb_ref[0], preferred_element_type=F32)
    yc = jnp.dot(c_ref[...], wb_ref[1], preferred_element_type=F32)
    merged = gm_ref[:, 0:d].astype(F32) * ya + gm_ref[:, d:2 * d].astype(F32) * yc
    o_ref[...] = x_ref[...] + jnp.dot(merged.astype(BF16), wo_ref[...], preferred_element_type=F32)


def _merge(a, c, gm, x, wb, wo, rows):
    n, d = x.shape
    row_spec = lambda width: pl.BlockSpec((rows, width), lambda i: (i, 0))
    return pl.pallas_call(
        _merge_kernel,
        grid=(n // rows,),
        in_specs=[
            row_spec(a.shape[1]), row_spec(c.shape[1]), row_spec(gm.shape[1]), row_spec(d),
            pl.BlockSpec(wb.shape, lambda i: (0, 0, 0)),
            pl.BlockSpec(wo.shape, lambda i: (0, 0)),
        ],
        out_specs=row_spec(d),
        out_shape=jax.ShapeDtypeStruct((n, d), F32),
        compiler_params=pltpu.CompilerParams(
            dimension_semantics=("arbitrary",), vmem_limit_bytes=VMEM_LIMIT),
        name="merge",
    )(a, c, gm, x, wb, wo)


def kernel(x, meta_tokens, rel_bias, norm_g, w_in, q_norm_g, k_norm_g, lambda_q1, lambda_k1,
           lambda_q2, lambda_k2, subln_g, conv_w, w_branch, w_out):
    b, s, d = x.shape
    assert norm_g.shape[0] == 1, "single layer only"
    assert meta_tokens.shape[0] == N_META
    tile = min(ATTN_TILE, s // 2)
    rows = min(ROW_TILE, s)
    assert s % (2 * tile) == 0 and s % rows == 0 and tile % MXU_DIM == 0

    w_bf = w_in[0].astype(BF16)
    groups = SEG // HEAD_DIM
    qg = jnp.tile(q_norm_g[0].astype(F32) * (HEAD_DIM ** -0.5 * LOG2E), groups)[None]
    kg = jnp.tile(k_norm_g[0].astype(F32), groups)[None]
    gidx = np.arange(MXU_DIM) // HEAD_DIM
    gmean = jnp.asarray((gidx[:, None] == gidx[None, :]).astype(np.float32) / HEAD_DIM, BF16)
    lamv = jnp.stack([lambda_q1[0], lambda_k1[0], lambda_q2[0], lambda_k2[0]]).astype(F32)
    rb = rel_bias.astype(F32).reshape(N_BUCKETS, 2 * ATTN_HEADS)
    ng = norm_g.astype(F32)

    kmeta, vmeta, umeta, lam, t0, t1, tm = _prologue(
        rb, meta_tokens.astype(F32), ng, w_bf, kg, gmean, lamv)
    q, k, v, ga, c, gm = _in_proj(x, ng, w_bf, qg, kg, gmean, conv_w[0].astype(F32), umeta, rows)
    sg = subln_g.astype(F32) * (1.0 - LAM_INIT)
    a = _attention(q, k, v, kmeta, vmeta, ga, t0, t1, tm, lam, sg, tile)
    out = _merge(a.reshape(b * s, -1), c.reshape(b * s, -1), gm.reshape(b * s, -1),
                 x.reshape(b * s, d), w_branch[0].astype(BF16), w_out[0].astype(BF16), rows)
    return out.reshape(b, s, d)
```

```python
import functools
import math

import numpy as np
import jax
import jax.numpy as jnp
from jax import lax
from jax.experimental import pallas as pl
from jax.experimental.pallas import tpu as pltpu

N_META = 16
ATTN_HEADS = 4
HEAD_DIM = 64
V_DIM = 2 * HEAD_DIM
ATTN_WIDTH = ATTN_HEADS * V_DIM
CONV_WIDTH = 512
CONV_K = 3
N_BRANCH = 2
N_BUCKETS = 32
MAX_DISTANCE = 128
EPS = 1e-6
NEG_INF = -1e30
LAM_INIT = 0.8 - 0.6 * math.exp(-0.3 * 0)
LOG2E = math.log2(math.e)

COL_Q, COL_K, COL_V, COL_GA, COL_CB, COL_CC, COL_CH, COL_GC, COL_GM = (
    0, 512, 1024, 1536, 2048, 2560, 3072, 3584, 4096)
SEG = 512

LANES = 128
MXU_DIM = 256
VMEM_LIMIT = 56 * 1024 * 1024

ATTN_TILE = 1024
ROW_TILE = 1024
FAR_SPAN = 3072
OWN_EXTRA = 2048

F32 = jnp.float32
BF16 = jnp.bfloat16


def _bucket_thresholds():
    max_exact = N_BUCKETS // 2
    n = np.arange(0, 2 * MAX_DISTANCE)
    nf = np.maximum(n, max_exact).astype(np.float32)
    large = max_exact + (np.log(nf / np.float32(max_exact)) / np.float32(math.log(MAX_DISTANCE / max_exact))
                         * np.float32(N_BUCKETS - max_exact)).astype(np.int32)
    bucket = np.where(n < max_exact, n, np.minimum(large, N_BUCKETS - 1))
    return [int(np.argmax(bucket >= b)) for b in range(N_BUCKETS)]


_THR = _bucket_thresholds()
assert _THR[N_BUCKETS - 1] <= LANES


def _sigmoid(x):
    return 0.5 * jnp.tanh(0.5 * x) + 0.5


def _rms_rows(x, g):
    return x * lax.rsqrt(jnp.mean(x * x, axis=-1, keepdims=True) + EPS) * g


def _group_rms(x, gmean, g):
    sq = (x * x).astype(BF16)
    ms = jnp.concatenate([
        jnp.dot(sq[:, lo:lo + MXU_DIM], gmean, preferred_element_type=F32)
        for lo in range(0, x.shape[1], MXU_DIM)], axis=1)
    return x * lax.rsqrt(ms + EPS) * g


def _prologue_kernel(rb_ref, meta_ref, ng_ref, wk_ref, wv_ref, wcc_ref, wch_ref, kg_ref, gmean_ref,
                     lamv_ref, kmeta_ref, vmeta_ref, umeta_ref, lam_ref, t0_ref, t1_ref, tm_ref):
    xn = _rms_rows(meta_ref[...], ng_ref[...]).astype(BF16)
    k = jnp.dot(xn, wk_ref[...], preferred_element_type=F32)
    k = _group_rms(k, gmean_ref[...], kg_ref[...])
    v = jnp.dot(xn, wv_ref[...], preferred_element_type=F32)
    cc = jnp.dot(xn, wcc_ref[...], preferred_element_type=F32)
    ch = jnp.dot(xn, wch_ref[...], preferred_element_type=F32)
    kmeta_ref[...] = jnp.zeros(kmeta_ref.shape, BF16)
    vmeta_ref[...] = jnp.zeros(vmeta_ref.shape, BF16)
    kmeta_ref[0:N_META, :] = k.astype(BF16)
    vmeta_ref[0:N_META, :] = v.astype(BF16)
    umeta_ref[...] = cc * ch

    lv = lamv_ref[...]
    s1 = jnp.sum(lv[0:1] * lv[1:2], axis=-1, keepdims=True)
    s2 = jnp.sum(lv[2:3] * lv[3:4], axis=-1, keepdims=True)
    lam_ref[...] = jnp.broadcast_to(jnp.exp(s1) - jnp.exp(s2) + LAM_INIT, lam_ref.shape)

    row = lax.broadcasted_iota(jnp.int32, (LANES, LANES), 0)
    col = lax.broadcasted_iota(jnp.int32, (LANES, LANES), 1)
    d_diag = row - col
    d_sub = d_diag + LANES
    d_meta = d_diag + N_META

    def toeplitz(dist, hc):
        far = rb_ref[N_BUCKETS - 1, hc]
        t = jnp.full((LANES, LANES), (rb_ref[0, hc] - far) * LOG2E, F32)
        for b in range(1, N_BUCKETS):
            t = jnp.where(dist >= _THR[b], (rb_ref[b, hc] - far) * LOG2E, t)
        return t

    for hc in range(2 * ATTN_HEADS):
        t0_ref[hc] = jnp.where(d_diag >= 0, toeplitz(d_diag, hc), NEG_INF)
        t1_ref[hc] = toeplitz(d_sub, hc)
        tm_ref[hc] = jnp.where(col < N_META, toeplitz(d_meta, hc), NEG_INF)


def _prologue(rb, meta, ng, w_bf, kg, gmean, lamv):
    d = meta.shape[1]
    wspec = lambda j: pl.BlockSpec((d, SEG), lambda i, j=j: (0, j))
    full = lambda shape: pl.BlockSpec(shape, lambda i: (0,) * len(shape))
    blocks = jax.ShapeDtypeStruct((2 * ATTN_HEADS, LANES, LANES), F32)
    out_shape = (
        jax.ShapeDtypeStruct((LANES, SEG), BF16),
        jax.ShapeDtypeStruct((LANES, SEG), BF16),
        jax.ShapeDtypeStruct((N_META, SEG), F32),
        jax.ShapeDtypeStruct((1, LANES), F32),
        blocks,
        blocks,
        blocks,
    )
    return pl.pallas_call(
        _prologue_kernel,
        grid=(1,),
        in_specs=[
            pl.BlockSpec(memory_space=pltpu.SMEM),
            full(meta.shape), full(ng.shape),
            wspec(COL_K // SEG), wspec(COL_V // SEG), wspec(COL_CC // SEG), wspec(COL_CH // SEG),
            full(kg.shape), full(gmean.shape), full(lamv.shape),
        ],
        out_specs=tuple(full(s.shape) for s in out_shape),
        out_shape=out_shape,
        compiler_params=pltpu.CompilerParams(vmem_limit_bytes=VMEM_LIMIT),
        name="prologue",
    )(rb, meta, ng, w_bf, w_bf, w_bf, w_bf, kg, gmean, lamv)


def _in_proj_kernel(x_ref, ng_ref, w_ref, qg_ref, kg_ref, gmean_ref, cw_ref, umeta_ref,
                    q_ref, k_ref, v_ref, ga_ref, c_ref, gm_ref, ubuf, *, rows):
    t = pl.program_id(1)

    @pl.when(t == 0)
    def _():
        ubuf[0:8, :] = umeta_ref[N_META - 8:N_META, :]

    @pl.when(t > 0)
    def _():
        ubuf[0:8, :] = ubuf[rows:rows + 8, :]

    xn = _rms_rows(x_ref[0], ng_ref[...]).astype(BF16)

    def proj(lo, n=SEG):
        return jnp.dot(xn, w_ref[:, lo:lo + n], preferred_element_type=F32)

    for j in range(gm_ref.shape[2] // SEG):
        gm_ref[0, :, j * SEG:(j + 1) * SEG] = _sigmoid(proj(COL_GM + j * SEG)).astype(BF16)
    ga = proj(COL_GA)
    ga_ref[0] = (ga * _sigmoid(ga)).astype(BF16)

    u = proj(COL_CC) * proj(COL_CH)
    ubuf[8:8 + rows, :] = u
    cw = cw_ref[...]
    conv = cw[0:1] * ubuf[6:6 + rows, :] + cw[1:2] * ubuf[7:7 + rows, :] + cw[2:3] * u
    gc = proj(COL_GC)
    c_ref[0] = (proj(COL_CB) * conv * (gc * _sigmoid(gc))).astype(BF16)

    gmean = gmean_ref[...]
    q_ref[0] = _group_rms(proj(COL_Q), gmean, qg_ref[...]).astype(BF16)
    k_ref[0] = _group_rms(proj(COL_K), gmean, kg_ref[...]).astype(BF16)
    v_ref[0] = proj(COL_V).astype(BF16)


def _in_proj(x, ng, w_bf, qg, kg, gmean, cw, umeta, rows):
    b, s, d = x.shape
    ncols = w_bf.shape[1]
    gm_cols = ncols - COL_GM
    const = lambda shape: pl.BlockSpec(shape, lambda i, j: (0,) * len(shape))
    seg_out = pl.BlockSpec((1, rows, SEG), lambda i, j: (i, j, 0))
    out_shape = tuple(jax.ShapeDtypeStruct((b, s, SEG), BF16) for _ in range(5)) + (
        jax.ShapeDtypeStruct((b, s, gm_cols), BF16),)
    return pl.pallas_call(
        functools.partial(_in_proj_kernel, rows=rows),
        grid=(b, s // rows),
        in_specs=[
            pl.BlockSpec((1, rows, d), lambda i, j: (i, j, 0)),
            const(ng.shape),
            pl.BlockSpec(w_bf.shape, lambda i, j: (0, 0), pipeline_mode=pl.Buffered(1)),
            const(qg.shape), const(kg.shape), const(gmean.shape), const(cw.shape), const(umeta.shape),
        ],
        out_specs=(seg_out,) * 5 + (pl.BlockSpec((1, rows, gm_cols), lambda i, j: (i, j, 0)),),
        out_shape=out_shape,
        scratch_shapes=[pltpu.VMEM((rows + 8, SEG), F32)],
        compiler_params=pltpu.CompilerParams(
            dimension_semantics=("arbitrary", "arbitrary"), vmem_limit_bytes=VMEM_LIMIT),
        name="in_proj",
    )(x, ng, w_bf, qg, kg, gmean, cw, umeta)


def _attn_kernel(q_ref, k_ref, v_ref, kmeta_ref, vmeta_ref, ga_ref, t0_ref, t1_ref, tm_ref,
                 lam_ref, sg_ref, o_ref, *, tile, nq):
    nblk = tile // LANES

    def staged(meta_ref, x_ref, rows):
        if rows.start == 0:
            return jnp.concatenate([meta_ref[...], x_ref[0, 0:rows.stop - LANES, :]], axis=0)
        return x_ref[0, rows.start - LANES:rows.stop - LANES, :]

    def key_rows(rows):
        return staged(kmeta_ref, k_ref, rows)

    def value_rows(rows):
        vv = staged(vmeta_ref, v_ref, rows)
        return jnp.concatenate([vv, jnp.ones(vv.shape, BF16)], axis=1)

    lane = lax.broadcasted_iota(jnp.int32, (tile, V_DIM), 1)
    lane_m = lax.broadcasted_iota(jnp.int32, (1, LANES), 1)
    mrow = jnp.where(lane_m < N_META, 0.0, NEG_INF)

    def masked_q(qt):
        q = q_ref[0, qt * tile:(qt + 1) * tile, :]
        zero = jnp.zeros_like(q)
        return jnp.where(lane < HEAD_DIM, q, zero), jnp.where(lane >= HEAD_DIM, q, zero)

    def scores(qm, kk):
        return lax.dot_general(qm, kk, (((1,), (1,)), ((), ())), preferred_element_type=F32)

    def rowmax(s):
        return jnp.max(s, axis=-1, keepdims=True)

    def lanes(x, n):
        return jnp.concatenate([x] * n, axis=1)

    def far_scores(qm, keys):
        kk = key_rows(keys)
        out = []
        for c in range(2):
            s = scores(qm[c], kk)
            if keys.start == 0:
                s = jnp.concatenate([s[:, :LANES] + mrow, s[:, LANES:]], axis=1)
            out.append(s)
        return out

    def far_update(state, s, keys):
        vv = value_rows(keys)
        new = []
        for c in range(2):
            m_old, acc = state[c]
            m_new = jnp.maximum(m_old, rowmax(s[c]))
            p = jnp.exp2((s[c] - lanes(m_new, (keys.stop - keys.start) // LANES)).astype(BF16))
            new.append((m_new, lanes(jnp.exp2(m_old - m_new), 2) * acc
                        + jnp.dot(p, vv, preferred_element_type=F32)))
        return new

    chunk_blocks = MXU_DIM // LANES

    def own_chunks(eb):
        ncol = nblk + 1 + eb
        return [(lo, min(lo + chunk_blocks, ncol), max(lo - eb - 1, 0))
                for lo in range(0, ncol, chunk_blocks)]

    def own_rows(qt, eb, lo, hi):
        start = qt * tile - eb * LANES
        return slice(start + lo * LANES, start + hi * LANES)

    def own_scores(qm, qt, eb):
        return [[scores(qm[c][i_min * LANES:], key_rows(own_rows(qt, eb, lo, hi)))
                 for lo, hi, i_min in own_chunks(eb)] for c in range(2)]

    def own_init(own, qt, eb):
        chunks = own_chunks(eb)
        from_row0 = qt * tile == eb * LANES
        state = []
        for c in range(2):
            t0, t1, tm = t0_ref[c], t1_ref[c], tm_ref[c]

            def biased(i, j):
                ci = j // chunk_blocks
                lo, _, i_min = chunks[ci]
                blk = own[c][ci][(i - i_min) * LANES:(i - i_min + 1) * LANES,
                                 (j - lo) * LANES:(j - lo + 1) * LANES]
                if j - eb == i + 1:
                    return blk + t0
                if j == 0 and from_row0:
                    return blk + (tm if (i == 0 and eb == 0) else mrow)
                return blk + t1 if j - eb == i else blk

            pblk, m_rows = {}, []
            for i in range(nblk):
                row = [biased(i, j) for j in range(i + 2 + eb)]
                m_i = jnp.broadcast_to(rowmax(jnp.concatenate(row, axis=1)), (LANES, LANES))
                m_rows.append(m_i)
                for j, blk in enumerate(row):
                    pblk[i, j] = jnp.exp2((blk - m_i).astype(BF16))

            zeros = jnp.zeros((LANES, LANES), BF16)
            outs = [None] * nblk
            for lo, hi, i_min in chunks:
                p = jnp.concatenate([
                    jnp.concatenate([pblk.get((i, j), zeros) for j in range(lo, hi)], axis=1)
                    for i in range(i_min, nblk)], axis=0)
                o = jnp.dot(p, value_rows(own_rows(qt, eb, lo, hi)), preferred_element_type=F32)
                for i in range(i_min, nblk):
                    piece = o[(i - i_min) * LANES:(i - i_min + 1) * LANES]
                    outs[i] = piece if outs[i] is None else outs[i] + piece
            state.append((jnp.concatenate(m_rows, axis=0), jnp.concatenate(outs, axis=0)))
        return state

    def finalize(state, qt):
        o = [acc[:, :V_DIM] / acc[:, V_DIM:] for _, acc in state]
        a = _rms_rows(o[0] - lam_ref[...] * o[1], sg_ref[...])
        rows = slice(qt * tile, (qt + 1) * tile)
        o_ref[0, rows, :] = (a * ga_ref[0, rows, :].astype(F32)).astype(BF16)

    for pr in range(nq // 2):
        qts = (pr, nq - 1 - pr)
        qm = [masked_q(qt) for qt in qts]
        ebs = [min(qt * tile, OWN_EXTRA) // LANES for qt in qts]
        items = [(z, slice(lo, min(lo + FAR_SPAN, qts[z] * tile - ebs[z] * LANES)))
                 for z in range(2) for lo in range(0, qts[z] * tile - ebs[z] * LANES, FAR_SPAN)]
        state = [own_init(own_scores(qm[z], qts[z], ebs[z]), qts[z], ebs[z]) for z in range(2)]
        left = [sum(1 for item in items if item[0] == z) for z in range(2)]
        for z in range(2):
            if left[z] == 0:
                finalize(state[z], qts[z])
        for z, keys in items:
            state[z] = far_update(state[z], far_scores(qm[z], keys), keys)
            left[z] -= 1
            if left[z] == 0:
                finalize(state[z], qts[z])


def _attention(q, k, v, kmeta, vmeta, ga, t0, t1, tm, lam, sg, tile):
    b, s, _ = q.shape
    nq = s // tile
    seq_spec = pl.BlockSpec((1, s, V_DIM), lambda i, h: (i, 0, h))
    meta_spec = pl.BlockSpec((LANES, V_DIM), lambda i, h: (0, h))
    bias_spec = pl.BlockSpec((2, LANES, LANES), lambda i, h: (h, 0, 0))
    row_spec = pl.BlockSpec((1, LANES), lambda i, h: (0, 0))
    return pl.pallas_call(
        functools.partial(_attn_kernel, tile=tile, nq=nq),
        grid=(b, ATTN_HEADS),
        in_specs=[
            seq_spec, seq_spec, seq_spec, meta_spec, meta_spec, seq_spec,
            bias_spec, bias_spec, bias_spec, row_spec, row_spec,
        ],
        out_specs=seq_spec,
        out_shape=jax.ShapeDtypeStruct((b, s, ATTN_WIDTH), BF16),
        compiler_params=pltpu.CompilerParams(
            dimension_semantics=("arbitrary", "arbitrary"), vmem_limit_bytes=VMEM_LIMIT),
        name="attention",
    )(q, k, v, kmeta, vmeta, ga, t0, t1, tm, lam, sg)


def _merge_kernel(a_ref, c_ref, gm_ref, x_ref, wb_ref, wo_ref, o_ref):
    d = x_ref.shape[1]
    ya = jnp.dot(a_ref[...], wb_ref[0], preferred_element_type=F32)
    yc = jnp.dot(c_ref[...], wb_ref[1], preferred_element_type=F32)
    merged = gm_ref[:, 0:d].astype(F32) * ya + gm_ref[:, d:2 * d].astype(F32) * yc
    o_ref[...] = x_ref[...] + jnp.dot(merged.astype(BF16), wo_ref[...], preferred_element_type=F32)


def _merge(a, c, gm, x, wb, wo, rows):
    n, d = x.shape
    row_spec = lambda width: pl.BlockSpec((rows, width), lambda i: (i, 0))
    return pl.pallas_call(
        _merge_kernel,
        grid=(n // rows,),
        in_specs=[
            row_spec(a.shape[1]), row_spec(c.shape[1]), row_spec(gm.shape[1]), row_spec(d),
            pl.BlockSpec(wb.shape, lambda i: (0, 0, 0)),
            pl.BlockSpec(wo.shape, lambda i: (0, 0)),
        ],
        out_specs=row_spec(d),
        out_shape=jax.ShapeDtypeStruct((n, d), F32),
        compiler_params=pltpu.CompilerParams(
            dimension_semantics=("arbitrary",), vmem_limit_bytes=VMEM_LIMIT),
        name="merge",
    )(a, c, gm, x, wb, wo)


def kernel(x, meta_tokens, rel_bias, norm_g, w_in, q_norm_g, k_norm_g, lambda_q1, lambda_k1,
           lambda_q2, lambda_k2, subln_g, conv_w, w_branch, w_out):
    b, s, d = x.shape
    assert norm_g.shape[0] == 1, "single layer only"
    assert meta_tokens.shape[0] == N_META
    tile = min(ATTN_TILE, s // 2)
    rows = min(ROW_TILE, s)
    assert s % (2 * tile) == 0 and s % rows == 0 and tile % MXU_DIM == 0

    w_bf = w_in[0].astype(BF16)
    groups = SEG // HEAD_DIM
    qg = jnp.tile(q_norm_g[0].astype(F32) * (HEAD_DIM ** -0.5 * LOG2E), groups)[None]
    kg = jnp.tile(k_norm_g[0].astype(F32), groups)[None]
    gidx = np.arange(MXU_DIM) // HEAD_DIM
    gmean = jnp.asarray((gidx[:, None] == gidx[None, :]).astype(np.float32) / HEAD_DIM, BF16)
    lamv = jnp.stack([lambda_q1[0], lambda_k1[0], lambda_q2[0], lambda_k2[0]]).astype(F32)
    rb = rel_bias.astype(F32).reshape(N_BUCKETS, 2 * ATTN_HEADS)
    ng = norm_g.astype(F32)

    kmeta, vmeta, umeta, lam, t0, t1, tm = _prologue(
        rb, meta_tokens.astype(F32), ng, w_bf, kg, gmean, lamv)
    q, k, v, ga, c, gm = _in_proj(x, ng, w_bf, qg, kg, gmean, conv_w[0].astype(F32), umeta, rows)
    sg = subln_g.astype(F32) * (1.0 - LAM_INIT)
    a = _attention(q, k, v, kmeta, vmeta, ga, t0, t1, tm, lam, sg, tile)
    out = _merge(a.reshape(b * s, -1), c.reshape(b * s, -1), gm.reshape(b * s, -1),
                 x.reshape(b * s, d), w_branch[0].astype(BF16), w_out[0].astype(BF16), rows)
    return out.reshape(b, s, d)
```

```python
import functools
import math

import numpy as np
import jax
import jax.numpy as jnp
from jax import lax
from jax.experimental import pallas as pl
from jax.experimental.pallas import tpu as pltpu

N_META = 16
ATTN_HEADS = 4
HEAD_DIM = 64
V_DIM = 2 * HEAD_DIM
ATTN_WIDTH = ATTN_HEADS * V_DIM
CONV_WIDTH = 512
CONV_K = 3
N_BRANCH = 2
N_BUCKETS = 32
MAX_DISTANCE = 128
EPS = 1e-6
NEG_INF = -1e30
LAM_INIT = 0.8 - 0.6 * math.exp(-0.3 * 0)
LOG2E = math.log2(math.e)

COL_Q, COL_K, COL_V, COL_GA, COL_CB, COL_CC, COL_CH, COL_GC, COL_GM = (
    0, 512, 1024, 1536, 2048, 2560, 3072, 3584, 4096)
SEG = 512

LANES = 128
MXU_DIM = 256
VMEM_LIMIT = 56 * 1024 * 1024

ATTN_TILE = 1024
ROW_TILE = 1024
FAR_SPAN = 3072
OWN_EXTRA = 3072

F32 = jnp.float32
BF16 = jnp.bfloat16


def _bucket_thresholds():
    max_exact = N_BUCKETS // 2
    n = np.arange(0, 2 * MAX_DISTANCE)
    nf = np.maximum(n, max_exact).astype(np.float32)
    large = max_exact + (np.log(nf / np.float32(max_exact)) / np.float32(math.log(MAX_DISTANCE / max_exact))
                         * np.float32(N_BUCKETS - max_exact)).astype(np.int32)
    bucket = np.where(n < max_exact, n, np.minimum(large, N_BUCKETS - 1))
    return [int(np.argmax(bucket >= b)) for b in range(N_BUCKETS)]


_THR = _bucket_thresholds()
assert _THR[N_BUCKETS - 1] <= LANES


def _sigmoid(x):
    return 0.5 * jnp.tanh(0.5 * x) + 0.5


def _rms_rows(x, g):
    return x * lax.rsqrt(jnp.mean(x * x, axis=-1, keepdims=True) + EPS) * g


def _group_rms(x, gmean, g):
    sq = (x * x).astype(BF16)
    ms = jnp.concatenate([
        jnp.dot(sq[:, lo:lo + MXU_DIM], gmean, preferred_element_type=F32)
        for lo in range(0, x.shape[1], MXU_DIM)], axis=1)
    return x * lax.rsqrt(ms + EPS) * g


def _prologue_kernel(rb_ref, meta_ref, ng_ref, wk_ref, wv_ref, wcc_ref, wch_ref, kg_ref, gmean_ref,
                     lamv_ref, kmeta_ref, vmeta_ref, umeta_ref, lam_ref, t0_ref, t1_ref, tm_ref):
    xn = _rms_rows(meta_ref[...], ng_ref[...]).astype(BF16)
    k = jnp.dot(xn, wk_ref[...], preferred_element_type=F32)
    k = _group_rms(k, gmean_ref[...], kg_ref[...])
    v = jnp.dot(xn, wv_ref[...], preferred_element_type=F32)
    cc = jnp.dot(xn, wcc_ref[...], preferred_element_type=F32)
    ch = jnp.dot(xn, wch_ref[...], preferred_element_type=F32)
    kmeta_ref[...] = jnp.zeros(kmeta_ref.shape, BF16)
    vmeta_ref[...] = jnp.zeros(vmeta_ref.shape, BF16)
    kmeta_ref[0:N_META, :] = k.astype(BF16)
    vmeta_ref[0:N_META, :] = v.astype(BF16)
    umeta_ref[...] = cc * ch

    lv = lamv_ref[...]
    s1 = jnp.sum(lv[0:1] * lv[1:2], axis=-1, keepdims=True)
    s2 = jnp.sum(lv[2:3] * lv[3:4], axis=-1, keepdims=True)
    lam_ref[...] = jnp.broadcast_to(jnp.exp(s1) - jnp.exp(s2) + LAM_INIT, lam_ref.shape)

    row = lax.broadcasted_iota(jnp.int32, (LANES, LANES), 0)
    col = lax.broadcasted_iota(jnp.int32, (LANES, LANES), 1)
    d_diag = row - col
    d_sub = d_diag + LANES
    d_meta = d_diag + N_META

    def toeplitz(dist, hc):
        far = rb_ref[N_BUCKETS - 1, hc]
        t = jnp.full((LANES, LANES), (rb_ref[0, hc] - far) * LOG2E, F32)
        for b in range(1, N_BUCKETS):
            t = jnp.where(dist >= _THR[b], (rb_ref[b, hc] - far) * LOG2E, t)
        return t

    for hc in range(2 * ATTN_HEADS):
        t0_ref[hc] = jnp.where(d_diag >= 0, toeplitz(d_diag, hc), NEG_INF)
        t1_ref[hc] = toeplitz(d_sub, hc)
        tm_ref[hc] = jnp.where(col < N_META, toeplitz(d_meta, hc), NEG_INF)


def _prologue(rb, meta, ng, w_bf, kg, gmean, lamv):
    d = meta.shape[1]
    wspec = lambda j: pl.BlockSpec((d, SEG), lambda i, j=j: (0, j))
    full = lambda shape: pl.BlockSpec(shape, lambda i: (0,) * len(shape))
    blocks = jax.ShapeDtypeStruct((2 * ATTN_HEADS, LANES, LANES), F32)
    out_shape = (
        jax.ShapeDtypeStruct((LANES, SEG), BF16),
        jax.ShapeDtypeStruct((LANES, SEG), BF16),
        jax.ShapeDtypeStruct((N_META, SEG), F32),
        jax.ShapeDtypeStruct((1, LANES), F32),
        blocks,
        blocks,
        blocks,
    )
    return pl.pallas_call(
        _prologue_kernel,
        grid=(1,),
        in_specs=[
            pl.BlockSpec(memory_space=pltpu.SMEM),
            full(meta.shape), full(ng.shape),
            wspec(COL_K // SEG), wspec(COL_V // SEG), wspec(COL_CC // SEG), wspec(COL_CH // SEG),
            full(kg.shape), full(gmean.shape), full(lamv.shape),
        ],
        out_specs=tuple(full(s.shape) for s in out_shape),
        out_shape=out_shape,
        compiler_params=pltpu.CompilerParams(vmem_limit_bytes=VMEM_LIMIT),
        name="prologue",
    )(rb, meta, ng, w_bf, w_bf, w_bf, w_bf, kg, gmean, lamv)


def _in_proj_kernel(x_ref, ng_ref, w_ref, qg_ref, kg_ref, gmean_ref, cw_ref, umeta_ref,
                    q_ref, k_ref, v_ref, ga_ref, c_ref, gm_ref, ubuf, *, rows):
    t = pl.program_id(1)

    @pl.when(t == 0)
    def _():
        ubuf[0:8, :] = umeta_ref[N_META - 8:N_META, :]

    @pl.when(t > 0)
    def _():
        ubuf[0:8, :] = ubuf[rows:rows + 8, :]

    xn = _rms_rows(x_ref[0], ng_ref[...]).astype(BF16)

    def proj(lo, n=SEG):
        return jnp.dot(xn, w_ref[:, lo:lo + n], preferred_element_type=F32)

    for j in range(gm_ref.shape[2] // SEG):
        gm_ref[0, :, j * SEG:(j + 1) * SEG] = _sigmoid(proj(COL_GM + j * SEG)).astype(BF16)
    ga = proj(COL_GA)
    ga_ref[0] = (ga * _sigmoid(ga)).astype(BF16)

    u = proj(COL_CC) * proj(COL_CH)
    ubuf[8:8 + rows, :] = u
    cw = cw_ref[...]
    conv = cw[0:1] * ubuf[6:6 + rows, :] + cw[1:2] * ubuf[7:7 + rows, :] + cw[2:3] * u
    gc = proj(COL_GC)
    c_ref[0] = (proj(COL_CB) * conv * (gc * _sigmoid(gc))).astype(BF16)

    gmean = gmean_ref[...]
    q_ref[0] = _group_rms(proj(COL_Q), gmean, qg_ref[...]).astype(BF16)
    k_ref[0] = _group_rms(proj(COL_K), gmean, kg_ref[...]).astype(BF16)
    v_ref[0] = proj(COL_V).astype(BF16)


def _in_proj(x, ng, w_bf, qg, kg, gmean, cw, umeta, rows):
    b, s, d = x.shape
    ncols = w_bf.shape[1]
    gm_cols = ncols - COL_GM
    const = lambda shape: pl.BlockSpec(shape, lambda i, j: (0,) * len(shape))
    seg_out = pl.BlockSpec((1, rows, SEG), lambda i, j: (i, j, 0))
    out_shape = tuple(jax.ShapeDtypeStruct((b, s, SEG), BF16) for _ in range(5)) + (
        jax.ShapeDtypeStruct((b, s, gm_cols), BF16),)
    return pl.pallas_call(
        functools.partial(_in_proj_kernel, rows=rows),
        grid=(b, s // rows),
        in_specs=[
            pl.BlockSpec((1, rows, d), lambda i, j: (i, j, 0)),
            const(ng.shape),
            pl.BlockSpec(w_bf.shape, lambda i, j: (0, 0), pipeline_mode=pl.Buffered(1)),
            const(qg.shape), const(kg.shape), const(gmean.shape), const(cw.shape), const(umeta.shape),
        ],
        out_specs=(seg_out,) * 5 + (pl.BlockSpec((1, rows, gm_cols), lambda i, j: (i, j, 0)),),
        out_shape=out_shape,
        scratch_shapes=[pltpu.VMEM((rows + 8, SEG), F32)],
        compiler_params=pltpu.CompilerParams(
            dimension_semantics=("arbitrary", "arbitrary"), vmem_limit_bytes=VMEM_LIMIT),
        name="in_proj",
    )(x, ng, w_bf, qg, kg, gmean, cw, umeta)


def _attn_kernel(q_ref, k_ref, v_ref, kmeta_ref, vmeta_ref, ga_ref, t0_ref, t1_ref, tm_ref,
                 lam_ref, sg_ref, o_ref, *, tile, nq):
    nblk = tile // LANES

    def staged(meta_ref, x_ref, rows):
        if rows.start == 0:
            return jnp.concatenate([meta_ref[...], x_ref[0, 0:rows.stop - LANES, :]], axis=0)
        return x_ref[0, rows.start - LANES:rows.stop - LANES, :]

    def key_rows(rows):
        return staged(kmeta_ref, k_ref, rows)

    def value_rows(rows):
        vv = staged(vmeta_ref, v_ref, rows)
        return jnp.concatenate([vv, jnp.ones(vv.shape, BF16)], axis=1)

    lane = lax.broadcasted_iota(jnp.int32, (tile, V_DIM), 1)
    lane_m = lax.broadcasted_iota(jnp.int32, (1, LANES), 1)
    mrow = jnp.where(lane_m < N_META, 0.0, NEG_INF)

    def masked_q(qt):
        q = q_ref[0, qt * tile:(qt + 1) * tile, :]
        zero = jnp.zeros_like(q)
        return jnp.where(lane < HEAD_DIM, q, zero), jnp.where(lane >= HEAD_DIM, q, zero)

    def scores(qm, kk):
        return lax.dot_general(qm, kk, (((1,), (1,)), ((), ())), preferred_element_type=F32)

    def rowmax(s):
        return jnp.max(s, axis=-1, keepdims=True)

    def lanes(x, n):
        return jnp.concatenate([x] * n, axis=1)

    def far_scores(qm, keys):
        kk = key_rows(keys)
        out = []
        for c in range(2):
            s = scores(qm[c], kk)
            if keys.start == 0:
                s = jnp.concatenate([s[:, :LANES] + mrow, s[:, LANES:]], axis=1)
            out.append(s)
        return out

    def far_update(state, s, keys):
        vv = value_rows(keys)
        new = []
        for c in range(2):
            m_old, acc = state[c]
            m_new = jnp.maximum(m_old, rowmax(s[c]))
            p = jnp.exp2((s[c] - lanes(m_new, (keys.stop - keys.start) // LANES)).astype(BF16))
            new.append((m_new, lanes(jnp.exp2(m_old - m_new), 2) * acc
                        + jnp.dot(p, vv, preferred_element_type=F32)))
        return new

    chunk_blocks = MXU_DIM // LANES

    def own_chunks(eb):
        ncol = nblk + 1 + eb
        return [(lo, min(lo + chunk_blocks, ncol), max(lo - eb - 1, 0))
                for lo in range(0, ncol, chunk_blocks)]

    def own_rows(qt, eb, lo, hi):
        start = qt * tile - eb * LANES
        return slice(start + lo * LANES, start + hi * LANES)

    def own_scores(qm, qt, eb):
        return [[scores(qm[c][i_min * LANES:], key_rows(own_rows(qt, eb, lo, hi)))
                 for lo, hi, i_min in own_chunks(eb)] for c in range(2)]

    def own_init(own, qt, eb):
        chunks = own_chunks(eb)
        from_row0 = qt * tile == eb * LANES
        state = []
        for c in range(2):
            t0, t1, tm = t0_ref[c], t1_ref[c], tm_ref[c]

            def biased(i, j):
                ci = j // chunk_blocks
                lo, _, i_min = chunks[ci]
                blk = own[c][ci][(i - i_min) * LANES:(i - i_min + 1) * LANES,
                                 (j - lo) * LANES:(j - lo + 1) * LANES]
                if j - eb == i + 1:
                    return blk + t0
                if j == 0 and from_row0:
                    return blk + (tm if (i == 0 and eb == 0) else mrow)
                return blk + t1 if j - eb == i else blk

            pblk, m_rows = {}, []
            for i in range(nblk):
                row = [biased(i, j) for j in range(i + 2 + eb)]
                m_i = jnp.broadcast_to(rowmax(jnp.concatenate(row, axis=1)), (LANES, LANES))
                m_rows.append(m_i)
                for j, blk in enumerate(row):
                    pblk[i, j] = jnp.exp2((blk - m_i).astype(BF16))

            zeros = jnp.zeros((LANES, LANES), BF16)
            outs = [None] * nblk
            for lo, hi, i_min in chunks:
                p = jnp.concatenate([
                    jnp.concatenate([pblk.get((i, j), zeros) for j in range(lo, hi)], axis=1)
                    for i in range(i_min, nblk)], axis=0)
                o = jnp.dot(p, value_rows(own_rows(qt, eb, lo, hi)), preferred_element_type=F32)
                for i in range(i_min, nblk):
                    piece = o[(i - i_min) * LANES:(i - i_min + 1) * LANES]
                    outs[i] = piece if outs[i] is None else outs[i] + piece
            state.append((jnp.concatenate(m_rows, axis=0), jnp.concatenate(outs, axis=0)))
        return state

    def finalize(state, qt):
        o = [acc[:, :V_DIM] / acc[:, V_DIM:] for _, acc in state]
        a = _rms_rows(o[0] - lam_ref[...] * o[1], sg_ref[...])
        rows = slice(qt * tile, (qt + 1) * tile)
        o_ref[0, rows, :] = (a * ga_ref[0, rows, :].astype(F32)).astype(BF16)

    for pr in range(nq // 2):
        qts = (pr, nq - 1 - pr)
        qm = [masked_q(qt) for qt in qts]
        ebs = [min(qt * tile, OWN_EXTRA) // LANES for qt in qts]
        items = [(z, slice(lo, min(lo + FAR_SPAN, qts[z] * tile - ebs[z] * LANES)))
                 for z in range(2) for lo in range(0, qts[z] * tile - ebs[z] * LANES, FAR_SPAN)]
        state = [own_init(own_scores(qm[z], qts[z], ebs[z]), qts[z], ebs[z]) for z in range(2)]
        left = [sum(1 for item in items if item[0] == z) for z in range(2)]
        for z in range(2):
            if left[z] == 0:
                finalize(state[z], qts[z])
        for z, keys in items:
            state[z] = far_update(state[z], far_scores(qm[z], keys), keys)
            left[z] -= 1
            if left[z] == 0:
                finalize(state[z], qts[z])


def _attention(q, k, v, kmeta, vmeta, ga, t0, t1, tm, lam, sg, tile):
    b, s, _ = q.shape
    nq = s // tile
    seq_spec = pl.BlockSpec((1, s, V_DIM), lambda i, h: (i, 0, h))
    meta_spec = pl.BlockSpec((LANES, V_DIM), lambda i, h: (0, h))
    bias_spec = pl.BlockSpec((2, LANES, LANES), lambda i, h: (h, 0, 0))
    row_spec = pl.BlockSpec((1, LANES), lambda i, h: (0, 0))
    return pl.pallas_call(
        functools.partial(_attn_kernel, tile=tile, nq=nq),
        grid=(b, ATTN_HEADS),
        in_specs=[
            seq_spec, seq_spec, seq_spec, meta_spec, meta_spec, seq_spec,
            bias_spec, bias_spec, bias_spec, row_spec, row_spec,
        ],
        out_specs=seq_spec,
        out_shape=jax.ShapeDtypeStruct((b, s, ATTN_WIDTH), BF16),
        compiler_params=pltpu.CompilerParams(
            dimension_semantics=("arbitrary", "arbitrary"), vmem_limit_bytes=VMEM_LIMIT),
        name="attention",
    )(q, k, v, kmeta, vmeta, ga, t0, t1, tm, lam, sg)


def _merge_kernel(a_ref, c_ref, gm_ref, x_ref, wb_ref, wo_ref, o_ref):
    d = x_ref.shape[1]
    ya = jnp.dot(a_ref[...], wb_ref[0], preferred_element_type=F32)
    yc = jnp.dot(c_ref[...], wb_ref[1], preferred_element_type=F32)
    merged = gm_ref[:, 0:d].astype(F32) * ya + gm_ref[:, d:2 * d].astype(F32) * yc
    o_ref[...] = x_ref[...] + jnp.dot(merged.astype(BF16), wo_ref[...], preferred_element_type=F32)


def _merge(a, c, gm, x, wb, wo, rows):
    n, d = x.shape
    row_spec = lambda width: pl.BlockSpec((rows, width), lambda i: (i, 0))
    return pl.pallas_call(
        _merge_kernel,
        grid=(n // rows,),
        in_specs=[
            row_spec(a.shape[1]), row_spec(c.shape[1]), row_spec(gm.shape[1]), row_spec(d),
            pl.BlockSpec(wb.shape, lambda i: (0, 0, 0)),
            pl.BlockSpec(wo.shape, lambda i: (0, 0)),
        ],
        out_specs=row_spec(d),
        out_shape=jax.ShapeDtypeStruct((n, d), F32),
        compiler_params=pltpu.CompilerParams(
            dimension_semantics=("arbitrary",), vmem_limit_bytes=VMEM_LIMIT),
        name="merge",
    )(a, c, gm, x, wb, wo)


def kernel(x, meta_tokens, rel_bias, norm_g, w_in, q_norm_g, k_norm_g, lambda_q1, lambda_k1,
           lambda_q2, lambda_k2, subln_g, conv_w, w_branch, w_out):
    b, s, d = x.shape
    assert norm_g.shape[0] == 1, "single layer only"
    assert meta_tokens.shape[0] == N_META
    tile = min(ATTN_TILE, s // 2)
    rows = min(ROW_TILE, s)
    assert s % (2 * tile) == 0 and s % rows == 0 and tile % MXU_DIM == 0

    w_bf = w_in[0].astype(BF16)
    groups = SEG // HEAD_DIM
    qg = jnp.tile(q_norm_g[0].astype(F32) * (HEAD_DIM ** -0.5 * LOG2E), groups)[None]
    kg = jnp.tile(k_norm_g[0].astype(F32), groups)[None]
    gidx = np.arange(MXU_DIM) // HEAD_DIM
    gmean = jnp.asarray((gidx[:, None] == gidx[None, :]).astype(np.float32) / HEAD_DIM, BF16)
    lamv = jnp.stack([lambda_q1[0], lambda_k1[0], lambda_q2[0], lambda_k2[0]]).astype(F32)
    rb = rel_bias.astype(F32).reshape(N_BUCKETS, 2 * ATTN_HEADS)
    ng = norm_g.astype(F32)

    kmeta, vmeta, umeta, lam, t0, t1, tm = _prologue(
        rb, meta_tokens.astype(F32), ng, w_bf, kg, gmean, lamv)
    q, k, v, ga, c, gm = _in_proj(x, ng, w_bf, qg, kg, gmean, conv_w[0].astype(F32), umeta, rows)
    sg = subln_g.astype(F32) * (1.0 - LAM_INIT)
    a = _attention(q, k, v, kmeta, vmeta, ga, t0, t1, tm, lam, sg, tile)
    out = _merge(a.reshape(b * s, -1), c.reshape(b * s, -1), gm.reshape(b * s, -1),
                 x.reshape(b * s, d), w_branch[0].astype(BF16), w_out[0].astype(BF16), rows)
    return out.reshape(b, s, d)
```

```python
import functools
import math

import numpy as np
import jax
import jax.numpy as jnp
from jax import lax
from jax.experimental import pallas as pl
from jax.experimental.pallas import tpu as pltpu

N_META = 16
ATTN_HEADS = 4
HEAD_DIM = 64
V_DIM = 2 * HEAD_DIM
ATTN_WIDTH = ATTN_HEADS * V_DIM
CONV_WIDTH = 512
CONV_K = 3
N_BRANCH = 2
N_BUCKETS = 32
MAX_DISTANCE = 128
EPS = 1e-6
NEG_INF = -1e30
LAM_INIT = 0.8 - 0.6 * math.exp(-0.3 * 0)
LOG2E = math.log2(math.e)

COL_Q, COL_K, COL_V, COL_GA, COL_CB, COL_CC, COL_CH, COL_GC, COL_GM = (
    0, 512, 1024, 1536, 2048, 2560, 3072, 3584, 4096)
SEG = 512

LANES = 128
MXU_DIM = 256
VMEM_LIMIT = 56 * 1024 * 1024

ATTN_TILE = 512
ROW_TILE = 1024
FAR_SPAN = 3072
OWN_EXTRA = 3584

F32 = jnp.float32
BF16 = jnp.bfloat16


def _bucket_thresholds():
    max_exact = N_BUCKETS // 2
    n = np.arange(0, 2 * MAX_DISTANCE)
    nf = np.maximum(n, max_exact).astype(np.float32)
    large = max_exact + (np.log(nf / np.float32(max_exact)) / np.float32(math.log(MAX_DISTANCE / max_exact))
                         * np.float32(N_BUCKETS - max_exact)).astype(np.int32)
    bucket = np.where(n < max_exact, n, np.minimum(large, N_BUCKETS - 1))
    return [int(np.argmax(bucket >= b)) for b in range(N_BUCKETS)]


_THR = _bucket_thresholds()
assert _THR[N_BUCKETS - 1] <= LANES


def _sigmoid(x):
    return 0.5 * jnp.tanh(0.5 * x) + 0.5


def _rms_rows(x, g):
    return x * lax.rsqrt(jnp.mean(x * x, axis=-1, keepdims=True) + EPS) * g


def _group_rms(x, gmean, g):
    sq = (x * x).astype(BF16)
    ms = jnp.concatenate([
        jnp.dot(sq[:, lo:lo + MXU_DIM], gmean, preferred_element_type=F32)
        for lo in range(0, x.shape[1], MXU_DIM)], axis=1)
    return x * lax.rsqrt(ms + EPS) * g


def _prologue_kernel(rb_ref, meta_ref, ng_ref, wk_ref, wv_ref, wcc_ref, wch_ref, kg_ref, gmean_ref,
                     lamv_ref, kmeta_ref, vmeta_ref, umeta_ref, lam_ref, t0_ref, t1_ref, tm_ref):
    xn = _rms_rows(meta_ref[...], ng_ref[...]).astype(BF16)
    k = jnp.dot(xn, wk_ref[...], preferred_element_type=F32)
    k = _group_rms(k, gmean_ref[...], kg_ref[...])
    v = jnp.dot(xn, wv_ref[...], preferred_element_type=F32)
    cc = jnp.dot(xn, wcc_ref[...], preferred_element_type=F32)
    ch = jnp.dot(xn, wch_ref[...], preferred_element_type=F32)
    kmeta_ref[...] = jnp.zeros(kmeta_ref.shape, BF16)
    vmeta_ref[...] = jnp.zeros(vmeta_ref.shape, BF16)
    kmeta_ref[0:N_META, :] = k.astype(BF16)
    vmeta_ref[0:N_META, :] = v.astype(BF16)
    umeta_ref[...] = cc * ch

    lv = lamv_ref[...]
    s1 = jnp.sum(lv[0:1] * lv[1:2], axis=-1, keepdims=True)
    s2 = jnp.sum(lv[2:3] * lv[3:4], axis=-1, keepdims=True)
    lam_ref[...] = jnp.broadcast_to(jnp.exp(s1) - jnp.exp(s2) + LAM_INIT, lam_ref.shape)

    row = lax.broadcasted_iota(jnp.int32, (LANES, LANES), 0)
    col = lax.broadcasted_iota(jnp.int32, (LANES, LANES), 1)
    d_diag = row - col
    d_sub = d_diag + LANES
    d_meta = d_diag + N_META

    def toeplitz(dist, hc):
        far = rb_ref[N_BUCKETS - 1, hc]
        t = jnp.full((LANES, LANES), (rb_ref[0, hc] - far) * LOG2E, F32)
        for b in range(1, N_BUCKETS):
            t = jnp.where(dist >= _THR[b], (rb_ref[b, hc] - far) * LOG2E, t)
        return t

    for hc in range(2 * ATTN_HEADS):
        t0_ref[hc] = jnp.where(d_diag >= 0, toeplitz(d_diag, hc), NEG_INF)
        t1_ref[hc] = toeplitz(d_sub, hc)
        tm_ref[hc] = jnp.where(col < N_META, toeplitz(d_meta, hc), NEG_INF)


def _prologue(rb, meta, ng, w_bf, kg, gmean, lamv):
    d = meta.shape[1]
    wspec = lambda j: pl.BlockSpec((d, SEG), lambda i, j=j: (0, j))
    full = lambda shape: pl.BlockSpec(shape, lambda i: (0,) * len(shape))
    blocks = jax.ShapeDtypeStruct((2 * ATTN_HEADS, LANES, LANES), F32)
    out_shape = (
        jax.ShapeDtypeStruct((LANES, SEG), BF16),
        jax.ShapeDtypeStruct((LANES, SEG), BF16),
        jax.ShapeDtypeStruct((N_META, SEG), F32),
        jax.ShapeDtypeStruct((1, LANES), F32),
        blocks,
        blocks,
        blocks,
    )
    return pl.pallas_call(
        _prologue_kernel,
        grid=(1,),
        in_specs=[
            pl.BlockSpec(memory_space=pltpu.SMEM),
            full(meta.shape), full(ng.shape),
            wspec(COL_K // SEG), wspec(COL_V // SEG), wspec(COL_CC // SEG), wspec(COL_CH // SEG),
            full(kg.shape), full(gmean.shape), full(lamv.shape),
        ],
        out_specs=tuple(full(s.shape) for s in out_shape),
        out_shape=out_shape,
        compiler_params=pltpu.CompilerParams(vmem_limit_bytes=VMEM_LIMIT),
        name="prologue",
    )(rb, meta, ng, w_bf, w_bf, w_bf, w_bf, kg, gmean, lamv)


def _in_proj_kernel(x_ref, ng_ref, w_ref, qg_ref, kg_ref, gmean_ref, cw_ref, umeta_ref,
                    q_ref, k_ref, v_ref, ga_ref, c_ref, gm_ref, ubuf, *, rows):
    t = pl.program_id(1)

    @pl.when(t == 0)
    def _():
        ubuf[0:8, :] = umeta_ref[N_META - 8:N_META, :]

    @pl.when(t > 0)
    def _():
        ubuf[0:8, :] = ubuf[rows:rows + 8, :]

    xn = _rms_rows(x_ref[0], ng_ref[...]).astype(BF16)

    def proj(lo, n=SEG):
        return jnp.dot(xn, w_ref[:, lo:lo + n], preferred_element_type=F32)

    for j in range(gm_ref.shape[2] // SEG):
        gm_ref[0, :, j * SEG:(j + 1) * SEG] = _sigmoid(proj(COL_GM + j * SEG)).astype(BF16)
    ga = proj(COL_GA)
    ga_ref[0] = (ga * _sigmoid(ga)).astype(BF16)

    u = proj(COL_CC) * proj(COL_CH)
    ubuf[8:8 + rows, :] = u
    cw = cw_ref[...]
    conv = cw[0:1] * ubuf[6:6 + rows, :] + cw[1:2] * ubuf[7:7 + rows, :] + cw[2:3] * u
    gc = proj(COL_GC)
    c_ref[0] = (proj(COL_CB) * conv * (gc * _sigmoid(gc))).astype(BF16)

    gmean = gmean_ref[...]
    q_ref[0] = _group_rms(proj(COL_Q), gmean, qg_ref[...]).astype(BF16)
    k_ref[0] = _group_rms(proj(COL_K), gmean, kg_ref[...]).astype(BF16)
    v_ref[0] = proj(COL_V).astype(BF16)


def _in_proj(x, ng, w_bf, qg, kg, gmean, cw, umeta, rows):
    b, s, d = x.shape
    ncols = w_bf.shape[1]
    gm_cols = ncols - COL_GM
    const = lambda shape: pl.BlockSpec(shape, lambda i, j: (0,) * len(shape))
    seg_out = pl.BlockSpec((1, rows, SEG), lambda i, j: (i, j, 0))
    out_shape = tuple(jax.ShapeDtypeStruct((b, s, SEG), BF16) for _ in range(5)) + (
        jax.ShapeDtypeStruct((b, s, gm_cols), BF16),)
    return pl.pallas_call(
        functools.partial(_in_proj_kernel, rows=rows),
        grid=(b, s // rows),
        in_specs=[
            pl.BlockSpec((1, rows, d), lambda i, j: (i, j, 0)),
            const(ng.shape),
            pl.BlockSpec(w_bf.shape, lambda i, j: (0, 0), pipeline_mode=pl.Buffered(1)),
            const(qg.shape), const(kg.shape), const(gmean.shape), const(cw.shape), const(umeta.shape),
        ],
        out_specs=(seg_out,) * 5 + (pl.BlockSpec((1, rows, gm_cols), lambda i, j: (i, j, 0)),),
        out_shape=out_shape,
        scratch_shapes=[pltpu.VMEM((rows + 8, SEG), F32)],
        compiler_params=pltpu.CompilerParams(
            dimension_semantics=("arbitrary", "arbitrary"), vmem_limit_bytes=VMEM_LIMIT),
        name="in_proj",
    )(x, ng, w_bf, qg, kg, gmean, cw, umeta)


def _attn_kernel(q_ref, k_ref, v_ref, kmeta_ref, vmeta_ref, ga_ref, t0_ref, t1_ref, tm_ref,
                 lam_ref, sg_ref, o_ref, *, tile, nq):
    nblk = tile // LANES

    def staged(meta_ref, x_ref, rows):
        if rows.start == 0:
            return jnp.concatenate([meta_ref[...], x_ref[0, 0:rows.stop - LANES, :]], axis=0)
        return x_ref[0, rows.start - LANES:rows.stop - LANES, :]

    def key_rows(rows):
        return staged(kmeta_ref, k_ref, rows)

    def value_rows(rows):
        vv = staged(vmeta_ref, v_ref, rows)
        return jnp.concatenate([vv, jnp.ones(vv.shape, BF16)], axis=1)

    lane = lax.broadcasted_iota(jnp.int32, (tile, V_DIM), 1)
    lane_m = lax.broadcasted_iota(jnp.int32, (1, LANES), 1)
    mrow = jnp.where(lane_m < N_META, 0.0, NEG_INF)

    def masked_q(qt):
        q = q_ref[0, qt * tile:(qt + 1) * tile, :]
        zero = jnp.zeros_like(q)
        return jnp.where(lane < HEAD_DIM, q, zero), jnp.where(lane >= HEAD_DIM, q, zero)

    def scores(qm, kk):
        return lax.dot_general(qm, kk, (((1,), (1,)), ((), ())), preferred_element_type=F32)

    def rowmax(s):
        return jnp.max(s, axis=-1, keepdims=True)

    def lanes(x, n):
        return jnp.concatenate([x] * n, axis=1)

    def far_scores(qm, keys):
        kk = key_rows(keys)
        out = []
        for c in range(2):
            s = scores(qm[c], kk)
            if keys.start == 0:
                s = jnp.concatenate([s[:, :LANES] + mrow, s[:, LANES:]], axis=1)
            out.append(s)
        return out

    def far_update(state, s, keys):
        vv = value_rows(keys)
        new = []
        for c in range(2):
            m_old, acc = state[c]
            m_new = jnp.maximum(m_old, rowmax(s[c]))
            p = jnp.exp2((s[c] - lanes(m_new, (keys.stop - keys.start) // LANES)).astype(BF16))
            new.append((m_new, lanes(jnp.exp2(m_old - m_new), 2) * acc
                        + jnp.dot(p, vv, preferred_element_type=F32)))
        return new

    chunk_blocks = MXU_DIM // LANES

    def own_chunks(eb):
        ncol = nblk + 1 + eb
        return [(lo, min(lo + chunk_blocks, ncol), max(lo - eb - 1, 0))
                for lo in range(0, ncol, chunk_blocks)]

    def own_rows(qt, eb, lo, hi):
        start = qt * tile - eb * LANES
        return slice(start + lo * LANES, start + hi * LANES)

    def own_scores(qm, qt, eb):
        return [[scores(qm[c][i_min * LANES:], key_rows(own_rows(qt, eb, lo, hi)))
                 for lo, hi, i_min in own_chunks(eb)] for c in range(2)]

    def own_init(own, qt, eb):
        chunks = own_chunks(eb)
        from_row0 = qt * tile == eb * LANES
        state = []
        for c in range(2):
            t0, t1, tm = t0_ref[c], t1_ref[c], tm_ref[c]

            def biased(i, j):
                ci = j // chunk_blocks
                lo, _, i_min = chunks[ci]
                blk = own[c][ci][(i - i_min) * LANES:(i - i_min + 1) * LANES,
                                 (j - lo) * LANES:(j - lo + 1) * LANES]
                if j - eb == i + 1:
                    return blk + t0
                if j == 0 and from_row0:
                    return blk + (tm if (i == 0 and eb == 0) else mrow)
                return blk + t1 if j - eb == i else blk

            pblk, m_rows = {}, []
            for i in range(nblk):
                row = [biased(i, j) for j in range(i + 2 + eb)]
                m_i = jnp.broadcast_to(rowmax(jnp.concatenate(row, axis=1)), (LANES, LANES))
                m_rows.append(m_i)
                for j, blk in enumerate(row):
                    pblk[i, j] = jnp.exp2((blk - m_i).astype(BF16))

            zeros = jnp.zeros((LANES, LANES), BF16)
            outs = [None] * nblk
            for lo, hi, i_min in chunks:
                p = jnp.concatenate([
                    jnp.concatenate([pblk.get((i, j), zeros) for j in range(lo, hi)], axis=1)
                    for i in range(i_min, nblk)], axis=0)
                o = jnp.dot(p, value_rows(own_rows(qt, eb, lo, hi)), preferred_element_type=F32)
                for i in range(i_min, nblk):
                    piece = o[(i - i_min) * LANES:(i - i_min + 1) * LANES]
                    outs[i] = piece if outs[i] is None else outs[i] + piece
            state.append((jnp.concatenate(m_rows, axis=0), jnp.concatenate(outs, axis=0)))
        return state

    def finalize(state, qt):
        o = [acc[:, :V_DIM] / acc[:, V_DIM:] for _, acc in state]
        a = _rms_rows(o[0] - lam_ref[...] * o[1], sg_ref[...])
        rows = slice(qt * tile, (qt + 1) * tile)
        o_ref[0, rows, :] = (a * ga_ref[0, rows, :].astype(F32)).astype(BF16)

    for qt in reversed(range(nq)):
        qm = masked_q(qt)
        eb = min(qt * tile, OWN_EXTRA) // LANES
        n_far = qt * tile - eb * LANES
        state = own_init(own_scores(qm, qt, eb), qt, eb)
        for lo in range(0, n_far, FAR_SPAN):
            keys = slice(lo, min(lo + FAR_SPAN, n_far))
            state = far_update(state, far_scores(qm, keys), keys)
        finalize(state, qt)


def _attention(q, k, v, kmeta, vmeta, ga, t0, t1, tm, lam, sg, tile):
    b, s, _ = q.shape
    nq = s // tile
    seq_spec = pl.BlockSpec((1, s, V_DIM), lambda i, h: (i, 0, h))
    meta_spec = pl.BlockSpec((LANES, V_DIM), lambda i, h: (0, h))
    bias_spec = pl.BlockSpec((2, LANES, LANES), lambda i, h: (h, 0, 0))
    row_spec = pl.BlockSpec((1, LANES), lambda i, h: (0, 0))
    return pl.pallas_call(
        functools.partial(_attn_kernel, tile=tile, nq=nq),
        grid=(b, ATTN_HEADS),
        in_specs=[
            seq_spec, seq_spec, seq_spec, meta_spec, meta_spec, seq_spec,
            bias_spec, bias_spec, bias_spec, row_spec, row_spec,
        ],
        out_specs=seq_spec,
        out_shape=jax.ShapeDtypeStruct((b, s, ATTN_WIDTH), BF16),
        compiler_params=pltpu.CompilerParams(
            dimension_semantics=("arbitrary", "arbitrary"), vmem_limit_bytes=VMEM_LIMIT),
        name="attention",
    )(q, k, v, kmeta, vmeta, ga, t0, t1, tm, lam, sg)


def _merge_kernel(a_ref, c_ref, gm_ref, x_ref, wb_ref, wo_ref, o_ref):
    d = x_ref.shape[1]
    ya = jnp.dot(a_ref[...], wb_ref[0], preferred_element_type=F32)
    yc = jnp.dot(c_ref[...], wb_ref[1], preferred_element_type=F32)
    merged = gm_ref[:, 0:d].astype(F32) * ya + gm_ref[:, d:2 * d].astype(F32) * yc
    o_ref[...] = x_ref[...] + jnp.dot(merged.astype(BF16), wo_ref[...], preferred_element_type=F32)


def _merge(a, c, gm, x, wb, wo, rows):
    n, d = x.shape
    row_spec = lambda width: pl.BlockSpec((rows, width), lambda i: (i, 0))
    return pl.pallas_call(
        _merge_kernel,
        grid=(n // rows,),
        in_specs=[
            row_spec(a.shape[1]), row_spec(c.shape[1]), row_spec(gm.shape[1]), row_spec(d),
            pl.BlockSpec(wb.shape, lambda i: (0, 0, 0)),
            pl.BlockSpec(wo.shape, lambda i: (0, 0)),
        ],
        out_specs=row_spec(d),
        out_shape=jax.ShapeDtypeStruct((n, d), F32),
        compiler_params=pltpu.CompilerParams(
            dimension_semantics=("arbitrary",), vmem_limit_bytes=VMEM_LIMIT),
        name="merge",
    )(a, c, gm, x, wb, wo)


def kernel(x, meta_tokens, rel_bias, norm_g, w_in, q_norm_g, k_norm_g, lambda_q1, lambda_k1,
           lambda_q2, lambda_k2, subln_g, conv_w, w_branch, w_out):
    b, s, d = x.shape
    assert norm_g.shape[0] == 1, "single layer only"
    assert meta_tokens.shape[0] == N_META
    tile = min(ATTN_TILE, s // 2)
    rows = min(ROW_TILE, s)
    assert s % (2 * tile) == 0 and s % rows == 0 and tile % MXU_DIM == 0

    w_bf = w_in[0].astype(BF16)
    groups = SEG // HEAD_DIM
    qg = jnp.tile(q_norm_g[0].astype(F32) * (HEAD_DIM ** -0.5 * LOG2E), groups)[None]
    kg = jnp.tile(k_norm_g[0].astype(F32), groups)[None]
    gidx = np.arange(MXU_DIM) // HEAD_DIM
    gmean = jnp.asarray((gidx[:, None] == gidx[None, :]).astype(np.float32) / HEAD_DIM, BF16)
    lamv = jnp.stack([lambda_q1[0], lambda_k1[0], lambda_q2[0], lambda_k2[0]]).astype(F32)
    rb = rel_bias.astype(F32).reshape(N_BUCKETS, 2 * ATTN_HEADS)
    ng = norm_g.astype(F32)

    kmeta, vmeta, umeta, lam, t0, t1, tm = _prologue(
        rb, meta_tokens.astype(F32), ng, w_bf, kg, gmean, lamv)
    q, k, v, ga, c, gm = _in_proj(x, ng, w_bf, qg, kg, gmean, conv_w[0].astype(F32), umeta, rows)
    sg = subln_g.astype(F32) * (1.0 - LAM_INIT)
    a = _attention(q, k, v, kmeta, vmeta, ga, t0, t1, tm, lam, sg, tile)
    out = _merge(a.reshape(b * s, -1), c.reshape(b * s, -1), gm.reshape(b * s, -1),
                 x.reshape(b * s, d), w_branch[0].astype(BF16), w_out[0].astype(BF16), rows)
    return out.reshape(b, s, d)
```
